```python
import math
import jax, jax.numpy as jnp
from jax import lax
import numpy as np

D_MODEL = 2048
BATCH = 4
SEQ = 2048
DEPTH = 1
DEC_BATCH = 128
DEC_SEQ = 8
PAST_LEN = 16384
PAGE_SIZE = 128

SSM_EXPAND = 2
D_INNER = SSM_EXPAND * D_MODEL
SSM_HEAD_DIM = 64
SSM_HEADS = D_INNER // SSM_HEAD_DIM
SSM_GROUPS = 8
SSM_STATE = 128
SSM_CONV = 4
SSM_CHUNK = 128
XBC_DIM = D_INNER + 2 * SSM_GROUPS * SSM_STATE
SC_DIM = D_MODEL
SC_CONV = 3
D_FF = 5632
FFN_CONV = 3
PLE_DIM = 256
EPS = 1e-6
IN_SPLITS = (D_INNER, XBC_DIM, SSM_HEADS, SC_DIM, SC_DIM, SC_DIM, D_MODEL, D_MODEL)
IN_DIM = sum(IN_SPLITS)

kernel_name = "hybrid_ssd_shortconv_convffn_step"


def split_last(t, sizes):
    idx = np.cumsum(sizes)[:-1].tolist()
    return jnp.split(t, idx, axis=-1)


def rmsnorm(x, g):
    xf = x.astype(jnp.float32)
    y = xf * lax.rsqrt(jnp.mean(xf * xf, axis=-1, keepdims=True) + EPS)
    return (y * g.astype(jnp.float32)).astype(x.dtype)


def causal_dwconv(x, past, w, b=None):
    K = w.shape[0]
    L = x.shape[1]
    xp = jnp.concatenate([past.astype(x.dtype), x], axis=1)
    y = xp[:, 0:L] * w[0]
    for k in range(1, K):
        y = y + xp[:, k:k + L] * w[k]
    if b is not None:
        y = y + b
    return y, xp[:, L:]


def ssd_scan(x, dt, A, Bm, Cm, h0):
    b, L = x.shape[0], x.shape[1]
    G, R, P, N = SSM_GROUPS, SSM_HEADS // SSM_GROUPS, SSM_HEAD_DIM, SSM_STATE
    Q = min(SSM_CHUNK, L)
    nc = -(-L // Q)
    pad = nc * Q - L
    if pad:
        padt = lambda t: jnp.pad(t, [(0, 0), (0, pad)] + [(0, 0)] * (t.ndim - 2))
        x, dt, Bm, Cm = padt(x), padt(dt), padt(Bm), padt(Cm)
    x = x.reshape(b, nc, Q, G, R, P)
    dt = dt.reshape(b, nc, Q, G, R)
    Bm = Bm.reshape(b, nc, Q, G, N)
    Cm = Cm.reshape(b, nc, Q, G, N)
    Acum = jnp.cumsum(dt * A.reshape(G, R), axis=2)
    seg = Acum[:, :, :, None] - Acum[:, :, None]
    causal = jnp.tril(jnp.ones((Q, Q), dtype=bool))[:, :, None, None]
    Lmat = jnp.exp(jnp.where(causal, seg, -jnp.inf))
    CB = jnp.einsum('bclgn,bcsgn->bclsg', Cm, Bm)
    Wm = CB[..., None] * Lmat * dt[:, :, None]
    y_diag = jnp.einsum('bclsgr,bcsgrp->bclgrp', Wm, x)
    decay_end = jnp.exp(Acum[:, :, -1:] - Acum)
    chunk_states = jnp.einsum('bclgn,bclgr,bclgrp->bcgrpn', Bm, decay_end * dt, x)
    chunk_decay = jnp.exp(Acum[:, :, -1])

    def step(h, inp):
        s, d = inp
        return d[..., None, None] * h + s, h

    hT, h_prev = lax.scan(step, h0.reshape(b, G, R, P, N),
                          (jnp.moveaxis(chunk_states, 1, 0), jnp.moveaxis(chunk_decay, 1, 0)))
    h_prev = jnp.moveaxis(h_prev, 0, 1)
    y_off = jnp.einsum('bclgn,bcgrpn,bclgr->bclgrp', Cm, h_prev, jnp.exp(Acum))
    y = (y_diag + y_off).reshape(b, nc * Q, SSM_HEADS, P)[:, :L]
    return y, hT.reshape(b, SSM_HEADS, P, N)


def trunk_layer(x, pe, ssm0, xbc0, sc0, ffn0,
                norm_mix, w_in, ssm_conv_w, ssm_conv_b, ssm_dt_bias, ssm_a_log, ssm_d, ssm_norm,
                w_ssm_out, sc_conv_w, w_sc_out, w_o, norm_ffn, w_ffn_gate, w_ffn_up,
                ffn_conv_w, ffn_conv_b, w_ffn_down, norm_ple, w_ple, w_ple_gate):
    f32 = jnp.float32
    b, L, _ = x.shape
    n1 = rmsnorm(x, norm_mix)
    proj = n1 @ w_in
    z, xbc, dt_raw, sc_b, sc_c, sc_h, gate_a, gate_b = split_last(proj, IN_SPLITS)

    xbc_c, xbc_state = causal_dwconv(xbc, xbc0, ssm_conv_w, ssm_conv_b)
    xbc_c = jax.nn.silu(xbc_c)
    xs, Bm, Cm = split_last(xbc_c, (D_INNER, SSM_GROUPS * SSM_STATE, SSM_GROUPS * SSM_STATE))
    xs_h = xs.reshape(b, L, SSM_HEADS, SSM_HEAD_DIM).astype(f32)
    dt = jax.nn.softplus(dt_raw.astype(f32) + ssm_dt_bias.astype(f32))
    A = -jnp.exp(ssm_a_log.astype(f32))
    y, h_new = ssd_scan(xs_h, dt, A,
                        Bm.reshape(b, L, SSM_GROUPS, SSM_STATE).astype(f32),
                        Cm.reshape(b, L, SSM_GROUPS, SSM_STATE).astype(f32),
                        ssm0.astype(f32))
    y = y + ssm_d.astype(f32)[:, None] * xs_h
    y = y.reshape(b, L, D_INNER) * jax.nn.silu(z.astype(f32))
    yg = y.reshape(b, L, SSM_GROUPS, D_INNER // SSM_GROUPS)
    yg = yg * lax.rsqrt(jnp.mean(yg * yg, axis=-1, keepdims=True) + EPS)
    y = (yg.reshape(b, L, D_INNER) * ssm_norm.astype(f32)).astype(x.dtype)
    branch_a = y @ w_ssm_out

    sc_conv, sc_state = causal_dwconv(sc_c * sc_h, sc0, sc_conv_w)
    branch_b = (sc_b * sc_conv) @ w_sc_out

    mixed = jax.nn.sigmoid(gate_a) * branch_a + jax.nn.sigmoid(gate_b) * branch_b
    x = x + mixed @ w_o

    n2 = rmsnorm(x, norm_ffn)
    g_up = n2 @ w_ffn_gate
    g_conv, ffn_state = causal_dwconv(g_up, ffn0, ffn_conv_w, ffn_conv_b)
    x = x + (jax.nn.silu(g_conv) * (n2 @ w_ffn_up)) @ w_ffn_down

    n3 = rmsnorm(x, norm_ple)
    x = x + (pe.astype(x.dtype) @ w_ple) * jax.nn.sigmoid(n3 @ w_ple_gate)
    return x, h_new.astype(ssm0.dtype), xbc_state, sc_state, ffn_state


def setup_inputs(seed: int = 0) -> dict:
    key = jax.random.key(seed)
    ks = iter(jax.random.split(key, 48))
    f32 = jnp.float32

    def nrm(shape, scale=1.0):
        return jax.random.normal(next(ks), shape, f32) * scale

    def gain(shape):
        return 1.0 + nrm(shape, 0.05)

    log_dt = jax.random.uniform(next(ks), (DEPTH, SSM_HEADS), f32,
                                minval=math.log(1e-3), maxval=math.log(1e-1))
    dt0 = jnp.exp(log_dt)
    dt_bias = dt0 + jnp.log(-jnp.expm1(-dt0))
    a_log = jnp.log(jax.random.uniform(next(ks), (DEPTH, SSM_HEADS), f32, minval=1.0, maxval=16.0))
    return {
        'x_prompt': nrm((BATCH, SEQ, D_MODEL)),
        'x_sample': nrm((DEC_BATCH, DEC_SEQ, D_MODEL)),
        'state_ssm': nrm((DEPTH, DEC_BATCH, SSM_HEADS, SSM_HEAD_DIM, SSM_STATE), 0.1),
        'state_ssm_conv': nrm((DEPTH, DEC_BATCH, SSM_CONV - 1, XBC_DIM)),
        'state_short_conv': nrm((DEPTH, DEC_BATCH, SC_CONV - 1, SC_DIM)),
        'state_ffn_conv': nrm((DEPTH, DEC_BATCH, FFN_CONV - 1, D_FF)),
        'p_prompt': nrm((DEPTH, BATCH, SEQ, PLE_DIM)),
        'p_sample': nrm((DEPTH, DEC_BATCH, DEC_SEQ, PLE_DIM)),
        'norm_mix': gain((DEPTH, D_MODEL)),
        'w_in': nrm((DEPTH, D_MODEL, IN_DIM), D_MODEL ** -0.5),
        'ssm_conv_w': nrm((DEPTH, SSM_CONV, XBC_DIM), 0.5),
        'ssm_conv_b': nrm((DEPTH, XBC_DIM), 0.02),
        'ssm_dt_bias': dt_bias,
        'ssm_a_log': a_log,
        'ssm_d': gain((DEPTH, SSM_HEADS)),
        'ssm_norm': gain((DEPTH, D_INNER)),
        'w_ssm_out': nrm((DEPTH, D_INNER, D_MODEL), D_INNER ** -0.5),
        'sc_conv_w': nrm((DEPTH, SC_CONV, SC_DIM), SC_CONV ** -0.5),
        'w_sc_out': nrm((DEPTH, SC_DIM, D_MODEL), SC_DIM ** -0.5),
        'w_o': nrm((DEPTH, D_MODEL, D_MODEL), D_MODEL ** -0.5),
        'norm_ffn': gain((DEPTH, D_MODEL)),
        'w_ffn_gate': nrm((DEPTH, D_MODEL, D_FF), D_MODEL ** -0.5),
        'w_ffn_up': nrm((DEPTH, D_MODEL, D_FF), D_MODEL ** -0.5),
        'ffn_conv_w': nrm((DEPTH, FFN_CONV, D_FF), FFN_CONV ** -0.5),
        'ffn_conv_b': nrm((DEPTH, D_FF), 0.02),
        'w_ffn_down': nrm((DEPTH, D_FF, D_MODEL), D_FF ** -0.5),
        'norm_ple': gain((DEPTH, D_MODEL)),
        'w_ple': nrm((DEPTH, PLE_DIM, D_MODEL), PLE_DIM ** -0.5),
        'w_ple_gate': nrm((DEPTH, D_MODEL, D_MODEL), D_MODEL ** -0.5),
        'norm_final': gain((D_MODEL,)),
    }


def reference(x_prompt, x_sample, state_ssm, state_ssm_conv, state_short_conv, state_ffn_conv,
              p_prompt, p_sample, norm_mix, w_in, ssm_conv_w, ssm_conv_b, ssm_dt_bias, ssm_a_log,
              ssm_d, ssm_norm, w_ssm_out, sc_conv_w, w_sc_out, w_o, norm_ffn, w_ffn_gate, w_ffn_up,
              ffn_conv_w, ffn_conv_b, w_ffn_down, norm_ple, w_ple, w_ple_gate, norm_final):
    dt_p = x_prompt.dtype
    yp, ys = x_prompt, x_sample
    ssm_p, ssm_s, xbc_p, xbc_s, sc_p, sc_s, ffn_p, ffn_s = [], [], [], [], [], [], [], []
    for i in range(DEPTH):
        lw = (norm_mix[i], w_in[i], ssm_conv_w[i], ssm_conv_b[i], ssm_dt_bias[i], ssm_a_log[i],
              ssm_d[i], ssm_norm[i], w_ssm_out[i], sc_conv_w[i], w_sc_out[i], w_o[i], norm_ffn[i],
              w_ffn_gate[i], w_ffn_up[i], ffn_conv_w[i], ffn_conv_b[i], w_ffn_down[i], norm_ple[i],
              w_ple[i], w_ple_gate[i])
        yp, s0, s1, s2, s3 = trunk_layer(
            yp, p_prompt[i],
            jnp.zeros((BATCH, SSM_HEADS, SSM_HEAD_DIM, SSM_STATE), dt_p),
            jnp.zeros((BATCH, SSM_CONV - 1, XBC_DIM), dt_p),
            jnp.zeros((BATCH, SC_CONV - 1, SC_DIM), dt_p),
            jnp.zeros((BATCH, FFN_CONV - 1, D_FF), dt_p),
            *lw)
        ssm_p.append(s0); xbc_p.append(s1); sc_p.append(s2); ffn_p.append(s3)
        ys, t0, t1, t2, t3 = trunk_layer(
            ys, p_sample[i], state_ssm[i], state_ssm_conv[i], state_short_conv[i], state_ffn_conv[i],
            *lw)
        ssm_s.append(t0); xbc_s.append(t1); sc_s.append(t2); ffn_s.append(t3)
    y_prompt = rmsnorm(yp, norm_final)
    y_sample = rmsnorm(ys, norm_final)
    return (y_prompt, y_sample,
            jnp.stack(ssm_p), jnp.stack(ssm_s),
            jnp.stack(xbc_p), jnp.stack(xbc_s),
            jnp.stack(sc_p), jnp.stack(sc_s),
            jnp.stack(ffn_p), jnp.stack(ffn_s))
```

```python
import functools

import numpy as np
import jax
import jax.numpy as jnp
from jax import lax
from jax.experimental import pallas as pl
from jax.experimental.pallas import tpu as pltpu

F32 = jnp.float32
BF16 = jnp.bfloat16

D_MODEL = 2048
D_INNER = 4096
SSM_HEADS = 64
SSM_HEAD_DIM = 64
SSM_GROUPS = 8
HEADS_PER_GROUP = SSM_HEADS // SSM_GROUPS
GROUP_WIDTH = D_INNER // SSM_GROUPS
SSM_STATE = 128
SSM_CHUNK = 128
SSM_CONV = 4
BC_DIM = 2 * SSM_GROUPS * SSM_STATE
XBC_DIM = D_INNER + BC_DIM
SC_DIM = D_MODEL
SC_CONV = 3
D_FF = 5632
FFN_CONV = 3
PLE_DIM = 256
EPS = 1e-6
DEC_SEQ = 8
PROJ_DIM = D_INNER + XBC_DIM + 5 * D_MODEL
DT_PAD = 128

SUBLANES = 8
VMEM_LIMIT = 52 * 1024 * 1024


def _cparams(sem):
    return pltpu.CompilerParams(dimension_semantics=sem, vmem_limit_bytes=VMEM_LIMIT)


def _dot(a, b):
    return jnp.dot(a, b, preferred_element_type=F32)


def _dot_nt(a, b):
    return lax.dot_general(a, b, (((1,), (1,)), ((), ())), preferred_element_type=F32)


def _sigmoid(x):
    return 1.0 / (1.0 + jnp.exp(-x))


def _silu(x):
    return x * _sigmoid(x)


def _softplus(x):
    return jnp.maximum(x, 0.0) + jnp.log1p(jnp.exp(-jnp.abs(x)))


def _rms(x, g):
    ms = jnp.mean(x * x, axis=-1, keepdims=True)
    return x * lax.rsqrt(ms + EPS) * g


def _split_bf16(a, terms):
    parts = []
    r = a
    for _ in range(terms - 1):
        p = r.astype(BF16)
        parts.append(p)
        r = r - p.astype(F32)
    parts.append(r.astype(BF16))
    return parts


def _dot_sel_right(a, sel, terms):
    out = None
    for p in _split_bf16(a, terms):
        d = _dot(p, sel)
        out = d if out is None else out + d
    return out


def _dot_sel_left(sel, a, terms):
    out = None
    for p in _split_bf16(a, terms):
        d = _dot(sel, p)
        out = d if out is None else out + d
    return out


def _shift_long(x, halo, k):
    r = pltpu.roll(x, k, 0)
    h = pltpu.roll(halo, k, 0)
    row = lax.broadcasted_iota(jnp.int32, halo.shape, 0)
    first = jnp.where(row < k, h, r[:SUBLANES])
    return jnp.concatenate([first, r[SUBLANES:]], axis=0)


def _shift_short(x, past, k, taps):
    n = x.shape[0]
    r = pltpu.roll(x, k, 0)
    s = (n + k - (taps - 1)) % n
    p = past if s == 0 else pltpu.roll(past, s, 0)
    t = lax.broadcasted_iota(jnp.int32, x.shape, 0) & (DEC_SEQ - 1)
    return jnp.where(t < k, p, r)


def _causal_conv(x, w, shift):
    taps = w.shape[0]
    y = w[taps - 1:taps] * x
    for k in range(1, taps):
        y = y + w[taps - 1 - k:taps - k] * shift(k)
    return y


def _in_proj_kernel(x_ref, g_ref, w_ref, wdt_ref, proj_ref, dt_ref, n1_scr):
    @pl.when(pl.program_id(1) == 0)
    def _():
        n1 = _rms(x_ref[...], g_ref[...]).astype(BF16)
        n1_scr[...] = n1
        dt_ref[...] = _dot(n1, wdt_ref[...])

    proj_ref[...] = _dot(n1_scr[...], w_ref[...]).astype(BF16)


def _in_proj(x, g, w, wdt, tm, tn):
    m = x.shape[0]
    return pl.pallas_call(
        _in_proj_kernel,
        grid=(m // tm, PROJ_DIM // tn),
        in_specs=[
            pl.BlockSpec((tm, D_MODEL), lambda i, j: (i, 0)),
            pl.BlockSpec((1, D_MODEL), lambda i, j: (0, 0)),
            pl.BlockSpec((D_MODEL, tn), lambda i, j: (0, j)),
            pl.BlockSpec((D_MODEL, DT_PAD), lambda i, j: (0, 0)),
        ],
        out_specs=[
            pl.BlockSpec((tm, tn), lambda i, j: (i, j)),
            pl.BlockSpec((tm, DT_PAD), lambda i, j: (i, 0)),
        ],
        out_shape=[
            jax.ShapeDtypeStruct((m, PROJ_DIM), BF16),
            jax.ShapeDtypeStruct((m, DT_PAD), F32),
        ],
        scratch_shapes=[pltpu.VMEM((tm, D_MODEL), BF16)],
        compiler_params=_cparams(("parallel", "arbitrary")),
        name="in_proj",
    )(x, g, w, wdt)


def _ssd_consts(q, short):
    l = np.arange(q)[:, None]
    s = np.arange(q)[None, :]
    if short:
        tri = (l // DEC_SEQ == s // DEC_SEQ) & (s <= l)
        end = s == (l // DEC_SEQ) * DEC_SEQ + DEC_SEQ - 1
    else:
        tri = s <= l
        end = np.broadcast_to(s == q - 1, (q, q))
    mask = np.where(tri, 0.0, -np.inf).astype(np.float32)
    h = np.arange(128)[None, :, None]
    g = np.arange(SSM_GROUPS)[:, None, None]
    c = np.arange(HEADS_PER_GROUP * 128)[None, None, :]
    sel = h == g * HEADS_PER_GROUP + c // 128
    c = np.arange(GROUP_WIDTH)[None, None, :]
    expand = h == g * HEADS_PER_GROUP + c // SSM_HEAD_DIM
    return (jnp.asarray(tri, BF16), jnp.asarray(end, BF16), jnp.asarray(mask),
            jnp.asarray(sel, BF16), jnp.asarray(expand, BF16))


def _ssd_kernel(*refs, short, q):
    (z_ref, xs_ref, bc_ref, dt_ref, tri_ref, end_ref, mask_ref, sel_ref, exp_ref,
     cwx_ref, cbx_ref, cwb_ref, cbb_ref, dtb_ref, a_ref, dexp_ref, gn_ref) = refs[:17]
    refs = refs[17:]
    if short:
        past_xs_ref, past_bc_ref, h0_ref = refs[:3]
        refs = refs[3:]
    y_ref, ht_ref = refs[:2]
    refs = refs[2:]
    (xs_scr, b_scr, c_scr, y_scr, dt_scr, ea_scr, we_scr, acum_scr, acumt_scr,
     cdec_scr) = refs[:10]
    refs = refs[10:]
    if short:
        eae_scr, xwt_scr = refs
    else:
        h_scr, tail_xs, tail_bc = refs

    step = pl.program_id(1)
    nstep = pl.num_programs(1)

    def conv_prelude():
        cwx = cwx_ref[...]
        cwb = cwb_ref[...]
        for g in range(SSM_GROUPS):
            cs = slice(g * GROUP_WIDTH, (g + 1) * GROUP_WIDTH)
            x = xs_ref[:, cs].astype(F32)
            if short:
                past = past_xs_ref[:, cs]
                shift = lambda k, x=x, past=past: _shift_short(x, past, k, SSM_CONV)
            else:
                halo = tail_xs[:, cs]
                shift = lambda k, x=x, halo=halo: _shift_long(x, halo, k)
            xs_scr[g] = _silu(cbx_ref[:, cs] + _causal_conv(x, cwx[:, cs], shift))
            if not short:
                tail_xs[:, cs] = x[q - SUBLANES:]
        for blk in range(BC_DIM // GROUP_WIDTH):
            cs = slice(blk * GROUP_WIDTH, (blk + 1) * GROUP_WIDTH)
            x = bc_ref[:, cs].astype(F32)
            if short:
                past = past_bc_ref[:, cs]
                shift = lambda k, x=x, past=past: _shift_short(x, past, k, SSM_CONV)
            else:
                halo = tail_bc[:, cs]
                shift = lambda k, x=x, halo=halo: _shift_long(x, halo, k)
            v = _silu(cbb_ref[:, cs] + _causal_conv(x, cwb[:, cs], shift))
            if not short:
                tail_bc[:, cs] = x[q - SUBLANES:]
            per_blk = GROUP_WIDTH // SSM_STATE
            dst = b_scr if blk < SSM_GROUPS // per_blk else c_scr
            for r in range(per_blk):
                dst[(blk * per_blk + r) % SSM_GROUPS] = v[:, r * SSM_STATE:(r + 1) * SSM_STATE]

    def dt_prelude():
        dt = _softplus(dt_ref[...] + dtb_ref[...])
        acum = _dot_sel_left(tri_ref[...], dt * a_ref[...], 3)
        aend = _dot_sel_left(end_ref[...], acum, 3)
        dt_scr[...] = dt
        acum_scr[...] = acum
        acumt_scr[...] = acum.T
        ea_scr[...] = jnp.exp(acum)
        we_scr[...] = jnp.exp(aend - acum) * dt

    def end_decay(last_row):
        row = lax.broadcasted_iota(jnp.int32, (q, 128), 0)
        onehot = jnp.where(row == last_row, 1.0, 0.0).astype(BF16)
        cdec_scr[...] = jnp.exp(_dot_sel_right(acumt_scr[...], onehot, 3))

    def head_rows(scr, g):
        rows = [jnp.broadcast_to(scr[pl.ds(g * HEADS_PER_GROUP + r, 1), :],
                                 (SSM_HEAD_DIM, 128)) for r in range(HEADS_PER_GROUP)]
        return jnp.concatenate(rows, axis=0)

    def diag_group(g):
        xs = xs_scr[g]
        cb = _dot_nt(c_scr[g].astype(BF16), b_scr[g].astype(BF16))
        expand = exp_ref[g]
        dt_e = _dot_sel_right(dt_scr[...], expand, 2)
        xdt = xs * dt_e
        colb = _dot_sel_right(acum_scr[...], sel_ref[g], 3)
        mask = mask_ref[...]
        lane = lax.broadcasted_iota(jnp.int32, (q, 128), 1)
        ys = []
        for pair in range(HEADS_PER_GROUP // 2):
            ws = []
            for half in range(2):
                r = 2 * pair + half
                rowb = acumt_scr[pl.ds(g * HEADS_PER_GROUP + r, 1), :]
                seg = colb[:, r * 128:(r + 1) * 128] - rowb + mask
                ws.append((cb * jnp.exp(seg)).astype(BF16))
            xp = xdt[:, pair * 128:(pair + 1) * 128]
            top = jnp.where(lane < SSM_HEAD_DIM, xp, 0.0).astype(BF16)
            bot = jnp.where(lane >= SSM_HEAD_DIM, xp, 0.0).astype(BF16)
            ys.append(_dot(jnp.concatenate(ws, axis=1), jnp.concatenate([top, bot], axis=0)))
        return jnp.concatenate(ys, axis=1), xs, expand

    def epilogue():
        for g in range(SSM_GROUPS):
            cs = slice(g * GROUP_WIDTH, (g + 1) * GROUP_WIDTH)
            y = y_scr[g] + dexp_ref[:, cs] * xs_scr[g]
            y = y * _silu(z_ref[:, cs].astype(F32))
            ms = jnp.mean(y * y, axis=-1, keepdims=True)
            y_ref[:, cs] = (y * lax.rsqrt(ms + EPS) * gn_ref[:, cs]).astype(BF16)

    if not short:
        @pl.when(step == 0)
        def _():
            h_scr[...] = jnp.zeros_like(h_scr)
            tail_xs[...] = jnp.zeros_like(tail_xs)
            tail_bc[...] = jnp.zeros_like(tail_bc)

        conv_prelude()
        dt_prelude()
        end_decay(q - 1)

        def group_body(g, carry):
            y_diag, xs, expand = diag_group(g)
            rows = pl.ds(pl.multiple_of(g * GROUP_WIDTH, GROUP_WIDTH), GROUP_WIDTH)
            hg = h_scr[rows, :]
            y_off = _dot_nt(c_scr[g].astype(BF16), hg.astype(BF16))
            y_scr[g] = y_diag + y_off * _dot_sel_right(ea_scr[...], expand, 2)
            xwt = (xs * _dot_sel_right(we_scr[...], expand, 2)).T.astype(BF16)
            st = _dot(xwt, b_scr[g].astype(BF16))
            h_scr[rows, :] = head_rows(cdec_scr, g) * hg + st
            return carry

        lax.fori_loop(0, SSM_GROUPS, group_body, 0)
        epilogue()

        @pl.when(step == nstep - 1)
        def _():
            ht_ref[0] = h_scr[...]
    else:
        @pl.when(step == 0)
        def _():
            conv_prelude()
            dt_prelude()

            def shared_body(g, carry):
                y_diag, xs, expand = diag_group(g)
                y_scr[g] = y_diag
                eae_scr[g] = _dot_sel_right(ea_scr[...], expand, 2)
                xwt_scr[g] = (xs * _dot_sel_right(we_scr[...], expand, 2)).T.astype(BF16)
                return carry

            lax.fori_loop(0, SSM_GROUPS, shared_body, 0)

        end_decay(step * DEC_SEQ + DEC_SEQ - 1)
        seq_rows = pl.ds(pl.multiple_of(step * DEC_SEQ, DEC_SEQ), DEC_SEQ)

        def seq_body(g, carry):
            row = lax.broadcasted_iota(jnp.int32, (q, SSM_STATE), 0)
            in_seq = (row // DEC_SEQ) == step
            rows = pl.ds(pl.multiple_of(g * GROUP_WIDTH, GROUP_WIDTH), GROUP_WIDTH)
            hg = h0_ref[0, rows, :]
            cj = c_scr[g, seq_rows, :].astype(BF16)
            y_off = _dot_nt(cj, hg.astype(BF16))
            y_scr[g, seq_rows, :] = y_scr[g, seq_rows, :] + y_off * eae_scr[g, seq_rows, :]
            bj = jnp.where(in_seq, b_scr[g], 0.0).astype(BF16)
            st = _dot(xwt_scr[g], bj)
            ht_ref[0, rows, :] = head_rows(cdec_scr, g) * hg + st
            return carry

        lax.fori_loop(0, SSM_GROUPS, seq_body, 0)

        @pl.when(step == nstep - 1)
        def _():
            epilogue()


def _ssd(proj, dt_raw, consts, wts, nseq, short, past=None, h0=None):
    m = proj.shape[0]
    q = SSM_CHUNK
    nblk = m // q
    if short:
        steps = q // DEC_SEQ
        grid = (nblk, steps)
        blk = lambda a, b: a
        st_idx = lambda a, b: (a * steps + b, 0, 0)
    else:
        steps = m // nseq // q
        grid = (nseq, steps)
        blk = lambda a, b: a * steps + b
        st_idx = lambda a, b: (a, 0, 0)
    const2 = lambda shape: pl.BlockSpec(shape, lambda a, b: (0, 0))
    const3 = lambda shape: pl.BlockSpec(shape, lambda a, b: (0, 0, 0))
    in_specs = [
        pl.BlockSpec((q, D_INNER), lambda a, b: (blk(a, b), 0)),
        pl.BlockSpec((q, D_INNER), lambda a, b: (blk(a, b), 1)),
        pl.BlockSpec((q, BC_DIM), lambda a, b: (blk(a, b), 2 * D_INNER // BC_DIM)),
        pl.BlockSpec((q, DT_PAD), lambda a, b: (blk(a, b), 0)),
        const2((q, q)), const2((q, q)), const2((q, q)),
        const3((SSM_GROUPS, 128, HEADS_PER_GROUP * 128)),
        const3((SSM_GROUPS, 128, GROUP_WIDTH)),
        const2((SSM_CONV, D_INNER)), const2((1, D_INNER)),
        const2((SSM_CONV, BC_DIM)), const2((1, BC_DIM)),
        const2((1, DT_PAD)), const2((1, DT_PAD)),
        const2((1, D_INNER)), const2((1, D_INNER)),
    ]
    args = [proj, proj, proj, dt_raw, *consts, *wts]
    state_spec = pl.BlockSpec((1, D_INNER, SSM_STATE), st_idx)
    scratch = [
        pltpu.VMEM((SSM_GROUPS, q, GROUP_WIDTH), F32),
        pltpu.VMEM((SSM_GROUPS, q, SSM_STATE), F32),
        pltpu.VMEM((SSM_GROUPS, q, SSM_STATE), F32),
        pltpu.VMEM((SSM_GROUPS, q, GROUP_WIDTH), F32),
        pltpu.VMEM((q, 128), F32), pltpu.VMEM((q, 128), F32), pltpu.VMEM((q, 128), F32),
        pltpu.VMEM((q, 128), F32), pltpu.VMEM((128, q), F32), pltpu.VMEM((128, 128), F32),
    ]
    if short:
        in_specs += [
            pl.BlockSpec((q, D_INNER), lambda a, b: (a, 0)),
            pl.BlockSpec((q, BC_DIM), lambda a, b: (a, 0)),
            state_spec,
        ]
        args += [*past, h0]
        scratch += [pltpu.VMEM((SSM_GROUPS, q, GROUP_WIDTH), F32),
                    pltpu.VMEM((SSM_GROUPS, GROUP_WIDTH, q), BF16)]
    else:
        scratch += [pltpu.VMEM((D_INNER, SSM_STATE), F32),
                    pltpu.VMEM((SUBLANES, D_INNER), F32),
                    pltpu.VMEM((SUBLANES, BC_DIM), F32)]
    return pl.pallas_call(
        functools.partial(_ssd_kernel, short=short, q=q),
        grid=grid,
        in_specs=in_specs,
        out_specs=[pl.BlockSpec((q, D_INNER), lambda a, b: (blk(a, b), 0)), state_spec],
        out_shape=[jax.ShapeDtypeStruct((m, D_INNER), BF16),
                   jax.ShapeDtypeStruct((nseq, D_INNER, SSM_STATE), F32)],
        scratch_shapes=scratch,
        compiler_params=_cparams(("arbitrary", "arbitrary")),
        name="ssd_short" if short else "ssd_long",
    )(*args)


def _mix_kernel(*refs, short, tiles_per_seq):
    yn_ref, scb_ref, scc_ref, sch_ref, ga_ref, gb_ref, cw_ref, wa_ref, wb_ref = refs[:9]
    refs = refs[9:]
    if short:
        past_ref = refs[0]
        refs = refs[1:]
    mixed_ref, utail_ref, s_scr = refs[:3]
    tm = s_scr.shape[0]

    @pl.when(pl.program_id(1) == 0)
    def _():
        u = scc_ref[...].astype(F32) * sch_ref[...].astype(F32)
        if short:
            shift = lambda k: _shift_short(u, past_ref[...], k, SC_CONV)
            utail_ref[...] = u
        else:
            halo_scr = refs[3]
            first = pl.program_id(0) % tiles_per_seq == 0
            halo = jnp.where(first, 0.0, halo_scr[...])
            shift = lambda k: _shift_long(u, halo, k)
            halo_scr[...] = u[tm - SUBLANES:]
            utail_ref[...] = u[tm - SUBLANES:]
        conv = _causal_conv(u, cw_ref[...], shift)
        s_scr[...] = (scb_ref[...].astype(F32) * conv).astype(BF16)

    a = _dot(yn_ref[...], wa_ref[...])
    b = _dot(s_scr[...], wb_ref[...])
    mixed = _sigmoid(ga_ref[...].astype(F32)) * a + _sigmoid(gb_ref[...].astype(F32)) * b
    mixed_ref[...] = mixed.astype(BF16)


def _mix(yn, proj, cw, wa, wb, seq_len, tm, tn, short, past=None):
    m = yn.shape[0]
    col = lambda c: (D_INNER + XBC_DIM) // D_MODEL + c
    in_specs = [
        pl.BlockSpec((tm, D_INNER), lambda i, j: (i, 0)),
        pl.BlockSpec((tm, D_MODEL), lambda i, j: (i, col(0))),
        pl.BlockSpec((tm, D_MODEL), lambda i, j: (i, col(1))),
        pl.BlockSpec((tm, D_MODEL), lambda i, j: (i, col(2))),
        pl.BlockSpec((tm, tn), lambda i, j: (i, col(3) * (D_MODEL // tn) + j)),
        pl.BlockSpec((tm, tn), lambda i, j: (i, col(4) * (D_MODEL // tn) + j)),
        pl.BlockSpec((SC_CONV, SC_DIM), lambda i, j: (0, 0)),
        pl.BlockSpec((D_INNER, tn), lambda i, j: (0, j)),
        pl.BlockSpec((SC_DIM, tn), lambda i, j: (0, j)),
    ]
    args = [yn, proj, proj, proj, proj, proj, cw, wa, wb]
    scratch = [pltpu.VMEM((tm, SC_DIM), BF16)]
    if short:
        in_specs.append(pl.BlockSpec((tm, SC_DIM), lambda i, j: (i, 0)))
        args.append(past)
        tail_rows = tm
    else:
        scratch.append(pltpu.VMEM((SUBLANES, SC_DIM), F32))
        tail_rows = SUBLANES
    return pl.pallas_call(
        functools.partial(_mix_kernel, short=short, tiles_per_seq=max(seq_len // tm, 1)),
        grid=(m // tm, D_MODEL // tn),
        in_specs=in_specs,
        out_specs=[pl.BlockSpec((tm, tn), lambda i, j: (i, j)),
                   pl.BlockSpec((tail_rows, SC_DIM), lambda i, j: (i, 0))],
        out_shape=[jax.ShapeDtypeStruct((m, D_MODEL), BF16),
                   jax.ShapeDtypeStruct((m // tm * tail_rows, SC_DIM), F32)],
        scratch_shapes=scratch,
        compiler_params=_cparams(("arbitrary", "arbitrary")),
        name="mix",
    )(*args)


def _out_proj_kernel(mixed_ref, x_ref, wo_ref, g_ref, x1_ref, n2_ref):
    x1 = x_ref[...] + _dot(mixed_ref[...], wo_ref[...])
    x1_ref[...] = x1
    n2_ref[...] = _rms(x1, g_ref[...]).astype(BF16)


def _out_proj(mixed, x, wo, g, tm):
    m = x.shape[0]
    row = lambda i: (i, 0)
    const = lambda i: (0, 0)
    return pl.pallas_call(
        _out_proj_kernel,
        grid=(m // tm,),
        in_specs=[pl.BlockSpec((tm, D_MODEL), row), pl.BlockSpec((tm, D_MODEL), row),
                  pl.BlockSpec((D_MODEL, D_MODEL), const), pl.BlockSpec((1, D_MODEL), const)],
        out_specs=[pl.BlockSpec((tm, D_MODEL), row), pl.BlockSpec((tm, D_MODEL), row)],
        out_shape=[jax.ShapeDtypeStruct((m, D_MODEL), F32),
                   jax.ShapeDtypeStruct((m, D_MODEL), BF16)],
        compiler_params=_cparams(("parallel",)),
        name="out_proj",
    )(mixed, x, wo, g)


def _ffn_up_kernel(*refs, short, tiles_per_seq):
    n2_ref, wg_ref, wu_ref, cw_ref, cb_ref = refs[:5]
    refs = refs[5:]
    if short:
        past_ref = refs[0]
        refs = refs[1:]
    act_ref, gtail_ref = refs[:2]
    tm = n2_ref.shape[0]
    n2 = n2_ref[...]
    gate = _dot(n2, wg_ref[...])
    up = _dot(n2, wu_ref[...])
    if short:
        shift = lambda k: _shift_short(gate, past_ref[...], k, FFN_CONV)
        gtail_ref[...] = gate
    else:
        halo_scr = refs[2]
        j = pl.program_id(1)
        first = pl.program_id(0) % tiles_per_seq == 0
        halo = jnp.where(first, 0.0, halo_scr[j])
        shift = lambda k: _shift_long(gate, halo, k)
        halo_scr[j] = gate[tm - SUBLANES:]
        gtail_ref[...] = gate[tm - SUBLANES:]
    conv = cb_ref[...] + _causal_conv(gate, cw_ref[...], shift)
    act_ref[...] = (_silu(conv) * up).astype(BF16)


def _ffn_up(n2, wg, wu, cw, cb, seq_len, tm, tn, short, past=None):
    m = n2.shape[0]
    nj = D_FF // tn
    in_specs = [
        pl.BlockSpec((tm, D_MODEL), lambda i, j: (i, 0)),
        pl.BlockSpec((D_MODEL, tn), lambda i, j: (0, j)),
        pl.BlockSpec((D_MODEL, tn), lambda i, j: (0, j)),
        pl.BlockSpec((FFN_CONV, tn), lambda i, j: (0, j)),
        pl.BlockSpec((1, tn), lambda i, j: (0, j)),
    ]
    args = [n2, wg, wu, cw, cb]
    scratch = []
    if short:
        in_specs.append(pl.BlockSpec((tm, tn), lambda i, j: (i, j)))
        args.append(past)
        tail_rows = tm
    else:
        scratch.append(pltpu.VMEM((nj, SUBLANES, tn), F32))
        tail_rows = SUBLANES
    return pl.pallas_call(
        functools.partial(_ffn_up_kernel, short=short, tiles_per_seq=max(seq_len // tm, 1)),
        grid=(m // tm, nj),
        in_specs=in_specs,
        out_specs=[pl.BlockSpec((tm, tn), lambda i, j: (i, j)),
                   pl.BlockSpec((tail_rows, tn), lambda i, j: (i, j))],
        out_shape=[jax.ShapeDtypeStruct((m, D_FF), BF16),
                   jax.ShapeDtypeStruct((m // tm * tail_rows, D_FF), F32)],
        scratch_shapes=scratch,
        compiler_params=_cparams(("arbitrary", "arbitrary")),
        name="ffn_up",
    )(*args)


def _ffn_down_kernel(act_ref, wd_ref, x1_ref, x2_ref):
    @pl.when(pl.program_id(1) == 0)
    def _():
        x2_ref[...] = x1_ref[...]

    x2_ref[...] += _dot(act_ref[...], wd_ref[...])


def _ffn_down(act, wd, x1, tm, tk):
    m = x1.shape[0]
    return pl.pallas_call(
        _ffn_down_kernel,
        grid=(m // tm, D_FF // tk),
        in_specs=[pl.BlockSpec((tm, tk), lambda i, k: (i, k)),
                  pl.BlockSpec((tk, D_MODEL), lambda i, k: (k, 0)),
                  pl.BlockSpec((tm, D_MODEL), lambda i, k: (i, 0))],
        out_specs=pl.BlockSpec((tm, D_MODEL), lambda i, k: (i, 0)),
        out_shape=jax.ShapeDtypeStruct((m, D_MODEL), F32),
        compiler_params=_cparams(("parallel", "arbitrary")),
        name="ffn_down",
    )(act, wd, x1)


def _ple_kernel(x2_ref, pe_ref, wple_ref, wgate_ref, gp_ref, gf_ref, y_ref):
    x2 = x2_ref[...]
    n3 = _rms(x2, gp_ref[...]).astype(BF16)
    gate = _sigmoid(_dot(n3, wgate_ref[...]))
    emb = _dot(pe_ref[...].astype(BF16), wple_ref[...])
    y_ref[...] = _rms(x2 + emb * gate, gf_ref[...])


def _ple(x2, pe, wple, wgate, gp, gf, tm):
    m = x2.shape[0]
    row = lambda i: (i, 0)
    const = lambda i: (0, 0)
    return pl.pallas_call(
        _ple_kernel,
        grid=(m // tm,),
        in_specs=[pl.BlockSpec((tm, D_MODEL), row), pl.BlockSpec((tm, PLE_DIM), row),
                  pl.BlockSpec((PLE_DIM, D_MODEL), const), pl.BlockSpec((D_MODEL, D_MODEL), const),
                  pl.BlockSpec((1, D_MODEL), const), pl.BlockSpec((1, D_MODEL), const)],
        out_specs=pl.BlockSpec((tm, D_MODEL), row),
        out_shape=jax.ShapeDtypeStruct((m, D_MODEL), F32),
        compiler_params=_cparams(("parallel",)),
        name="ple",
    )(x2, pe, wple, wgate, gp, gf)


def _pad_steps(state):
    s, k, c = state.shape
    return jnp.pad(state, ((0, 0), (0, DEC_SEQ - k), (0, 0))).reshape(s * DEC_SEQ, c)


def _tile(m, want):
    return min(m, want)


def _layer(x, pe, w, short, states=None):
    nseq, seq_len, _ = x.shape
    m = nseq * seq_len
    x2d = x.reshape(m, D_MODEL)
    pe2d = pe.reshape(m, PLE_DIM)
    if short:
        ssm0, xbc0, sc0, ffn0 = states
        past_xs = _pad_steps(xbc0[:, :, :D_INNER])
        past_bc = _pad_steps(xbc0[:, :, D_INNER:])
        past_sc = _pad_steps(sc0)
        past_ffn = _pad_steps(ffn0)
        h0 = ssm0.reshape(nseq, D_INNER, SSM_STATE)

    proj, dt_raw = _in_proj(x2d, w["norm_mix"], w["w_in"], w["w_dt"], _tile(m, 1024), 512)

    ssd_w = (w["cw_xs"], w["cb_xs"], w["cw_bc"], w["cb_bc"], w["dt_bias"], w["a_neg"],
             w["d_exp"], w["ssm_norm"])
    consts = _ssd_consts(SSM_CHUNK, short)
    if short:
        yn, ssm_new = _ssd(proj, dt_raw, consts, ssd_w, nseq, True, (past_xs, past_bc), h0)
    else:
        yn, ssm_new = _ssd(proj, dt_raw, consts, ssd_w, nseq, False)

    tm = _tile(m, 512)
    tm_mix = _tile(m, 256 if short else 512)
    mixed, u_tail = _mix(yn, proj, w["sc_conv_w"], w["w_ssm_out"], w["w_sc_out"], seq_len,
                         tm_mix, 512, short, past_sc if short else None)
    x1, n2 = _out_proj(mixed, x2d, w["w_o"], w["norm_ffn"], tm)

    tm_ffn = _tile(m, 1024)
    act, g_tail = _ffn_up(n2, w["w_ffn_gate"], w["w_ffn_up"], w["ffn_conv_w"], w["ffn_conv_b"],
                          seq_len, tm_ffn, 512, short, past_ffn if short else None)
    x2 = _ffn_down(act, w["w_ffn_down"], x1, tm, 512)
    y = _ple(x2, pe2d, w["w_ple"], w["w_ple_gate"], w["norm_ple"], w["norm_final"], tm)

    proj3 = proj.reshape(nseq, seq_len, PROJ_DIM)
    xbc_new = proj3[:, seq_len - (SSM_CONV - 1):, D_INNER:D_INNER + XBC_DIM].astype(F32)

    def last_rows(tail, tile_rows, taps, width):
        if short:
            t = tail.reshape(nseq, seq_len, width)
        else:
            per_seq = seq_len // tile_rows
            t = tail.reshape(nseq, per_seq, SUBLANES, width)[:, per_seq - 1]
        return t[:, t.shape[1] - (taps - 1):]

    sc_new = last_rows(u_tail, tm_mix, SC_CONV, SC_DIM)
    ffn_new = last_rows(g_tail, tm_ffn, FFN_CONV, D_FF)
    ssm_new = ssm_new.reshape(nseq, SSM_HEADS, SSM_HEAD_DIM, SSM_STATE)
    return y.reshape(nseq, seq_len, D_MODEL), ssm_new, xbc_new, sc_new, ffn_new


def _prep_weights(norm_mix, w_in, ssm_conv_w, ssm_conv_b, ssm_dt_bias, ssm_a_log, ssm_d, ssm_norm,
                  w_ssm_out, sc_conv_w, w_sc_out, w_o, norm_ffn, w_ffn_gate, w_ffn_up,
                  ffn_conv_w, ffn_conv_b, w_ffn_down, norm_ple, w_ple, w_ple_gate, norm_final):
    dt_lo = D_INNER + XBC_DIM
    dt_hi = dt_lo + SSM_HEADS
    row = lambda v: v.reshape(1, -1).astype(F32)
    pad_heads = lambda v: jnp.pad(v.astype(F32), (0, DT_PAD - SSM_HEADS)).reshape(1, DT_PAD)
    return {
        "norm_mix": row(norm_mix),
        "w_in": jnp.concatenate([w_in[:, :dt_lo], w_in[:, dt_hi:]], axis=1).astype(BF16),
        "w_dt": jnp.pad(w_in[:, dt_lo:dt_hi], ((0, 0), (0, DT_PAD - SSM_HEADS))).astype(BF16),
        "cw_xs": ssm_conv_w[:, :D_INNER].astype(F32),
        "cb_xs": row(ssm_conv_b[:D_INNER]),
        "cw_bc": ssm_conv_w[:, D_INNER:].astype(F32),
        "cb_bc": row(ssm_conv_b[D_INNER:]),
        "dt_bias": pad_heads(ssm_dt_bias),
        "a_neg": pad_heads(-jnp.exp(ssm_a_log.astype(F32))),
        "d_exp": row(jnp.repeat(ssm_d, SSM_HEAD_DIM)),
        "ssm_norm": row(ssm_norm),
        "w_ssm_out": w_ssm_out.astype(BF16),
        "sc_conv_w": sc_conv_w.astype(F32),
        "w_sc_out": w_sc_out.astype(BF16),
        "w_o": w_o.astype(BF16),
        "norm_ffn": row(norm_ffn),
        "w_ffn_gate": w_ffn_gate.astype(BF16),
        "w_ffn_up": w_ffn_up.astype(BF16),
        "ffn_conv_w": ffn_conv_w.astype(F32),
        "ffn_conv_b": row(ffn_conv_b),
        "w_ffn_down": w_ffn_down.astype(BF16),
        "norm_ple": row(norm_ple),
        "w_ple": w_ple.astype(BF16),
        "w_ple_gate": w_ple_gate.astype(BF16),
        "norm_final": row(norm_final),
    }


def kernel(x_prompt, x_sample, state_ssm, state_ssm_conv, state_short_conv, state_ffn_conv, p_prompt, p_sample, norm_mix, w_in, ssm_conv_w, ssm_conv_b, ssm_dt_bias, ssm_a_log, ssm_d, ssm_norm, w_ssm_out, sc_conv_w, w_sc_out, w_o, norm_ffn, w_ffn_gate, w_ffn_up, ffn_conv_w, ffn_conv_b, w_ffn_down, norm_ple, w_ple, w_ple_gate, norm_final):
    depth = w_in.shape[0]
    assert depth == 1, "the final norm is fused into the layer's last kernel"
    assert x_sample.shape[1] == DEC_SEQ
    layer_w = (norm_mix, w_in, ssm_conv_w, ssm_conv_b, ssm_dt_bias, ssm_a_log, ssm_d, ssm_norm,
               w_ssm_out, sc_conv_w, w_sc_out, w_o, norm_ffn, w_ffn_gate, w_ffn_up,
               ffn_conv_w, ffn_conv_b, w_ffn_down, norm_ple, w_ple, w_ple_gate)
    w = _prep_weights(*[t[0] for t in layer_w], norm_final)
    yp, *prompt_states = _layer(x_prompt, p_prompt[0], w, short=False)
    ys, *sample_states = _layer(
        x_sample, p_sample[0], w, short=True,
        states=(state_ssm[0], state_ssm_conv[0], state_short_conv[0], state_ffn_conv[0]))
    outs = [yp, ys]
    for sp, ss in zip(prompt_states, sample_states):
        outs += [sp[None], ss[None]]
    return tuple(outs)
```

```python
import functools

import numpy as np
import jax
import jax.numpy as jnp
from jax import lax
from jax.experimental import pallas as pl
from jax.experimental.pallas import tpu as pltpu

F32 = jnp.float32
BF16 = jnp.bfloat16

D_MODEL = 2048
D_INNER = 4096
SSM_HEADS = 64
SSM_HEAD_DIM = 64
SSM_GROUPS = 8
HEADS_PER_GROUP = SSM_HEADS // SSM_GROUPS
GROUP_WIDTH = D_INNER // SSM_GROUPS
SSM_STATE = 128
SSM_CHUNK = 128
SSM_CONV = 4
BC_DIM = 2 * SSM_GROUPS * SSM_STATE
XBC_DIM = D_INNER + BC_DIM
SC_DIM = D_MODEL
SC_CONV = 3
D_FF = 5632
FFN_CONV = 3
PLE_DIM = 256
EPS = 1e-6
DEC_SEQ = 8
PROJ_DIM = D_INNER + XBC_DIM + 5 * D_MODEL
DT_PAD = 128

SUBLANES = 8
VMEM_LIMIT = 52 * 1024 * 1024


def _cparams(sem):
    return pltpu.CompilerParams(dimension_semantics=sem, vmem_limit_bytes=VMEM_LIMIT)


def _dot(a, b):
    return jnp.dot(a, b, preferred_element_type=F32)


def _dot_nt(a, b):
    return lax.dot_general(a, b, (((1,), (1,)), ((), ())), preferred_element_type=F32)


def _sigmoid(x):
    return 0.5 * jnp.tanh(0.5 * x) + 0.5


def _silu(x):
    return x * _sigmoid(x)


def _softplus(x):
    return jnp.maximum(x, 0.0) + jnp.log1p(jnp.exp(-jnp.abs(x)))


def _rms(x, g):
    ms = jnp.mean(x * x, axis=-1, keepdims=True)
    return x * lax.rsqrt(ms + EPS) * g


def _split_bf16(a, terms):
    parts = []
    r = a
    for _ in range(terms - 1):
        p = r.astype(BF16)
        parts.append(p)
        r = r - p.astype(F32)
    parts.append(r.astype(BF16))
    return parts


def _dot_sel_right(a, sel, terms):
    out = None
    for p in _split_bf16(a, terms):
        d = _dot(p, sel)
        out = d if out is None else out + d
    return out


def _dot_sel_left(sel, a, terms):
    out = None
    for p in _split_bf16(a, terms):
        d = _dot(sel, p)
        out = d if out is None else out + d
    return out


def _shift_long(x, halo, k):
    r = pltpu.roll(x, k, 0)
    h = pltpu.roll(halo, k, 0)
    row = lax.broadcasted_iota(jnp.int32, halo.shape, 0)
    first = jnp.where(row < k, h, r[:SUBLANES])
    return jnp.concatenate([first, r[SUBLANES:]], axis=0)


def _shift_short(x, past, k, taps):
    n = x.shape[0]
    r = pltpu.roll(x, k, 0)
    s = (n + k - (taps - 1)) % n
    p = past if s == 0 else pltpu.roll(past, s, 0)
    t = lax.broadcasted_iota(jnp.int32, x.shape, 0) & (DEC_SEQ - 1)
    return jnp.where(t < k, p, r)


def _causal_conv(x, w, shift):
    taps = w.shape[0]
    y = w[taps - 1:taps] * x
    for k in range(1, taps):
        y = y + w[taps - 1 - k:taps - k] * shift(k)
    return y


def _in_proj_kernel(x_ref, g_ref, w_ref, wdt_ref, proj_ref, dt_ref, n1_scr):
    @pl.when(pl.program_id(1) == 0)
    def _():
        n1 = _rms(x_ref[...], g_ref[...]).astype(BF16)
        n1_scr[...] = n1
        dt_ref[...] = _dot(n1, wdt_ref[...])

    proj_ref[...] = _dot(n1_scr[...], w_ref[...]).astype(BF16)


def _in_proj(x, g, w, wdt, tm, tn):
    m = x.shape[0]
    return pl.pallas_call(
        _in_proj_kernel,
        grid=(m // tm, PROJ_DIM // tn),
        in_specs=[
            pl.BlockSpec((tm, D_MODEL), lambda i, j: (i, 0)),
            pl.BlockSpec((1, D_MODEL), lambda i, j: (0, 0)),
            pl.BlockSpec((D_MODEL, tn), lambda i, j: (0, j)),
            pl.BlockSpec((D_MODEL, DT_PAD), lambda i, j: (0, 0)),
        ],
        out_specs=[
            pl.BlockSpec((tm, tn), lambda i, j: (i, j)),
            pl.BlockSpec((tm, DT_PAD), lambda i, j: (i, 0)),
        ],
        out_shape=[
            jax.ShapeDtypeStruct((m, PROJ_DIM), BF16),
            jax.ShapeDtypeStruct((m, DT_PAD), F32),
        ],
        scratch_shapes=[pltpu.VMEM((tm, D_MODEL), BF16)],
        compiler_params=_cparams(("parallel", "arbitrary")),
        name="in_proj",
    )(x, g, w, wdt)


def _ssd_consts(q, short):
    l = np.arange(q)[:, None]
    s = np.arange(q)[None, :]
    if short:
        tri = (l // DEC_SEQ == s // DEC_SEQ) & (s <= l)
        end = s == (l // DEC_SEQ) * DEC_SEQ + DEC_SEQ - 1
    else:
        tri = s <= l
        end = np.broadcast_to(s == q - 1, (q, q))
    mask = np.where(tri, 0.0, -np.inf).astype(np.float32)
    h = np.arange(128)[None, :, None]
    g = np.arange(SSM_GROUPS)[:, None, None]
    c = np.arange(GROUP_WIDTH)[None, None, :]
    expand = h == g * HEADS_PER_GROUP + c // SSM_HEAD_DIM
    return (jnp.asarray(tri, BF16), jnp.asarray(end, BF16), jnp.asarray(mask),
            jnp.asarray(expand, BF16))


def _ssd_kernel(*refs, short, q):
    (z_ref, xs_ref, bc_ref, dt_ref, tri_ref, end_ref, mask_ref, exp_ref,
     cwx_ref, cbx_ref, cwb_ref, cbb_ref, dtb_ref, a_ref, dexp_ref, gn_ref) = refs[:16]
    refs = refs[16:]
    if short:
        past_xs_ref, past_bc_ref, h0_ref = refs[:3]
        refs = refs[3:]
    y_ref, ht_ref = refs[:2]
    refs = refs[2:]
    xs_scr, b_scr, c_scr, y_scr, ea_scr, we_scr, aend_scr, acumt_scr, arow_scr = refs[:9]
    refs = refs[9:]
    if short:
        cdec_scr, eae_scr, xwt_scr = refs
    else:
        h_scr, tail_xs, tail_bc = refs

    step = pl.program_id(1)
    nstep = pl.num_programs(1)

    def conv_prelude():
        cwx = cwx_ref[...]
        cwb = cwb_ref[...]
        for g in range(SSM_GROUPS):
            cs = slice(g * GROUP_WIDTH, (g + 1) * GROUP_WIDTH)
            x = xs_ref[:, cs].astype(F32)
            if short:
                past = past_xs_ref[:, cs]
                shift = lambda k, x=x, past=past: _shift_short(x, past, k, SSM_CONV)
            else:
                halo = tail_xs[:, cs]
                shift = lambda k, x=x, halo=halo: _shift_long(x, halo, k)
            xs_scr[g] = _silu(cbx_ref[:, cs] + _causal_conv(x, cwx[:, cs], shift))
            if not short:
                tail_xs[:, cs] = x[q - SUBLANES:]
        for blk in range(BC_DIM // GROUP_WIDTH):
            cs = slice(blk * GROUP_WIDTH, (blk + 1) * GROUP_WIDTH)
            x = bc_ref[:, cs].astype(F32)
            if short:
                past = past_bc_ref[:, cs]
                shift = lambda k, x=x, past=past: _shift_short(x, past, k, SSM_CONV)
            else:
                halo = tail_bc[:, cs]
                shift = lambda k, x=x, halo=halo: _shift_long(x, halo, k)
            v = _silu(cbb_ref[:, cs] + _causal_conv(x, cwb[:, cs], shift))
            if not short:
                tail_bc[:, cs] = x[q - SUBLANES:]
            per_blk = GROUP_WIDTH // SSM_STATE
            dst = b_scr if blk < SSM_GROUPS // per_blk else c_scr
            for r in range(per_blk):
                dst[(blk * per_blk + r) % SSM_GROUPS] = v[:, r * SSM_STATE:(r + 1) * SSM_STATE]

    def dt_prelude():
        dt = _softplus(dt_ref[...] + dtb_ref[...])
        acum = _dot_sel_left(tri_ref[...], dt * a_ref[...], 3)
        aend = _dot_sel_left(end_ref[...], acum, 3)
        acumt = acum.T
        acumt_scr[...] = acumt
        arow_scr[...] = acumt - jnp.log(dt.T)
        aend_scr[...] = aend
        ea_scr[...] = jnp.exp(acum)
        we_scr[...] = jnp.exp(aend - acum) * dt

    def diag_group(g):
        xs = xs_scr[g]
        cb = _dot_nt(c_scr[g].astype(BF16), b_scr[g].astype(BF16))
        mask = mask_ref[...]
        lane = lax.broadcasted_iota(jnp.int32, (q, 128), 1)
        ys = []
        for pair in range(HEADS_PER_GROUP // 2):
            ws = []
            for half in range(2):
                h = g * HEADS_PER_GROUP + 2 * pair + half
                colb = jnp.broadcast_to(acumt_scr[pl.ds(h, 1), :], (q, q)).T
                seg = colb - arow_scr[pl.ds(h, 1), :] + mask
                ws.append((cb * jnp.exp(seg)).astype(BF16))
            xp = xs[:, pair * 128:(pair + 1) * 128]
            top = jnp.where(lane < SSM_HEAD_DIM, xp, 0.0).astype(BF16)
            bot = jnp.where(lane >= SSM_HEAD_DIM, xp, 0.0).astype(BF16)
            ys.append(_dot(jnp.concatenate(ws, axis=1), jnp.concatenate([top, bot], axis=0)))
        return jnp.concatenate(ys, axis=1), xs, exp_ref[g]

    def epilogue():
        for g in range(SSM_GROUPS):
            cs = slice(g * GROUP_WIDTH, (g + 1) * GROUP_WIDTH)
            y = y_scr[g] + dexp_ref[:, cs] * xs_scr[g]
            y = y * _silu(z_ref[:, cs].astype(F32))
            ms = jnp.mean(y * y, axis=-1, keepdims=True)
            y_ref[:, cs] = (y * lax.rsqrt(ms + EPS) * gn_ref[:, cs]).astype(BF16)

    if not short:
        @pl.when(step == 0)
        def _():
            h_scr[...] = jnp.zeros_like(h_scr)
            tail_xs[...] = jnp.zeros_like(tail_xs)
            tail_bc[...] = jnp.zeros_like(tail_bc)

        conv_prelude()
        dt_prelude()

        def group_body(g, carry):
            y_diag, xs, expand = diag_group(g)
            hg = h_scr[g]
            y_off = _dot(c_scr[g].astype(BF16), hg.astype(BF16))
            y_scr[g] = y_diag + y_off * _dot_sel_right(ea_scr[...], expand, 2)
            xw = (xs * _dot_sel_right(we_scr[...], expand, 2)).astype(BF16)
            st = _dot(b_scr[g].T.astype(BF16), xw)
            dec = jnp.exp(_dot_sel_right(aend_scr[:SUBLANES, :], expand, 3))[:1]
            h_scr[g] = dec * hg + st
            return carry

        lax.fori_loop(0, SSM_GROUPS, group_body, 0, unroll=True)
        epilogue()

        @pl.when(step == nstep - 1)
        def _():
            for g in range(SSM_GROUPS):
                ht_ref[0, g * GROUP_WIDTH:(g + 1) * GROUP_WIDTH, :] = h_scr[g].T
    else:
        @pl.when(step == 0)
        def _():
            conv_prelude()
            dt_prelude()

            def shared_body(g, carry):
                y_diag, xs, expand = diag_group(g)
                y_scr[g] = y_diag
                eae_scr[g] = _dot_sel_right(ea_scr[...], expand, 2)
                xwt_scr[g] = (xs * _dot_sel_right(we_scr[...], expand, 2)).T.astype(BF16)
                return carry

            lax.fori_loop(0, SSM_GROUPS, shared_body, 0)

        last_row = step * DEC_SEQ + DEC_SEQ - 1
        row = lax.broadcasted_iota(jnp.int32, (q, 128), 0)
        onehot = jnp.where(row == last_row, 1.0, 0.0).astype(BF16)
        cdec_scr[...] = jnp.exp(_dot_sel_right(acumt_scr[...], onehot, 3))
        seq_rows = pl.ds(pl.multiple_of(step * DEC_SEQ, DEC_SEQ), DEC_SEQ)

        def seq_body(g, carry):
            row = lax.broadcasted_iota(jnp.int32, (q, SSM_STATE), 0)
            in_seq = (row // DEC_SEQ) == step
            rows = pl.ds(pl.multiple_of(g * GROUP_WIDTH, GROUP_WIDTH), GROUP_WIDTH)
            hg = h0_ref[0, rows, :]
            cj = c_scr[g, seq_rows, :].astype(BF16)
            y_off = _dot_nt(cj, hg.astype(BF16))
            y_scr[g, seq_rows, :] = y_scr[g, seq_rows, :] + y_off * eae_scr[g, seq_rows, :]
            bj = jnp.where(in_seq, b_scr[g], 0.0).astype(BF16)
            st = _dot(xwt_scr[g], bj)
            dec = [jnp.broadcast_to(cdec_scr[pl.ds(g * HEADS_PER_GROUP + r, 1), :],
                                    (SSM_HEAD_DIM, 128)) for r in range(HEADS_PER_GROUP)]
            ht_ref[0, rows, :] = jnp.concatenate(dec, axis=0) * hg + st
            return carry

        lax.fori_loop(0, SSM_GROUPS, seq_body, 0)

        @pl.when(step == nstep - 1)
        def _():
            epilogue()


def _ssd(proj, dt_raw, consts, wts, nseq, short, past=None, h0=None):
    m = proj.shape[0]
    q = SSM_CHUNK
    nblk = m // q
    if short:
        steps = q // DEC_SEQ
        grid = (nblk, steps)
        blk = lambda a, b: a
        st_idx = lambda a, b: (a * steps + b, 0, 0)
    else:
        steps = m // nseq // q
        grid = (nseq, steps)
        blk = lambda a, b: a * steps + b
        st_idx = lambda a, b: (a, 0, 0)
    const2 = lambda shape: pl.BlockSpec(shape, lambda a, b: (0, 0))
    const3 = lambda shape: pl.BlockSpec(shape, lambda a, b: (0, 0, 0))
    in_specs = [
        pl.BlockSpec((q, D_INNER), lambda a, b: (blk(a, b), 0)),
        pl.BlockSpec((q, D_INNER), lambda a, b: (blk(a, b), 1)),
        pl.BlockSpec((q, BC_DIM), lambda a, b: (blk(a, b), 2 * D_INNER // BC_DIM)),
        pl.BlockSpec((q, DT_PAD), lambda a, b: (blk(a, b), 0)),
        const2((q, q)), const2((q, q)), const2((q, q)),
        const3((SSM_GROUPS, 128, GROUP_WIDTH)),
        const2((SSM_CONV, D_INNER)), const2((1, D_INNER)),
        const2((SSM_CONV, BC_DIM)), const2((1, BC_DIM)),
        const2((1, DT_PAD)), const2((1, DT_PAD)),
        const2((1, D_INNER)), const2((1, D_INNER)),
    ]
    args = [proj, proj, proj, dt_raw, *consts, *wts]
    state_spec = pl.BlockSpec((1, D_INNER, SSM_STATE), st_idx)
    scratch = [
        pltpu.VMEM((SSM_GROUPS, q, GROUP_WIDTH), F32),
        pltpu.VMEM((SSM_GROUPS, q, SSM_STATE), F32),
        pltpu.VMEM((SSM_GROUPS, q, SSM_STATE), F32),
        pltpu.VMEM((SSM_GROUPS, q, GROUP_WIDTH), F32),
        pltpu.VMEM((q, 128), F32), pltpu.VMEM((q, 128), F32), pltpu.VMEM((q, 128), F32),
        pltpu.VMEM((128, q), F32), pltpu.VMEM((128, q), F32),
    ]
    if short:
        in_specs += [
            pl.BlockSpec((q, D_INNER), lambda a, b: (a, 0)),
            pl.BlockSpec((q, BC_DIM), lambda a, b: (a, 0)),
            state_spec,
        ]
        args += [*past, h0]
        scratch += [pltpu.VMEM((128, 128), F32),
                    pltpu.VMEM((SSM_GROUPS, q, GROUP_WIDTH), F32),
                    pltpu.VMEM((SSM_GROUPS, GROUP_WIDTH, q), BF16)]
    else:
        scratch += [pltpu.VMEM((SSM_GROUPS, SSM_STATE, GROUP_WIDTH), F32),
                    pltpu.VMEM((SUBLANES, D_INNER), F32),
                    pltpu.VMEM((SUBLANES, BC_DIM), F32)]
    return pl.pallas_call(
        functools.partial(_ssd_kernel, short=short, q=q),
        grid=grid,
        in_specs=in_specs,
        out_specs=[pl.BlockSpec((q, D_INNER), lambda a, b: (blk(a, b), 0)), state_spec],
        out_shape=[jax.ShapeDtypeStruct((m, D_INNER), BF16),
                   jax.ShapeDtypeStruct((nseq, D_INNER, SSM_STATE), F32)],
        scratch_shapes=scratch,
        compiler_params=_cparams(("arbitrary", "arbitrary")),
        name="ssd_short" if short else "ssd_long",
    )(*args)


def _mix_kernel(*refs, short, tiles_per_seq):
    yn_ref, scb_ref, scc_ref, sch_ref, ga_ref, gb_ref, cw_ref, wa_ref, wb_ref = refs[:9]
    refs = refs[9:]
    if short:
        past_ref = refs[0]
        refs = refs[1:]
    mixed_ref, utail_ref, s_scr = refs[:3]
    tm = s_scr.shape[0]

    @pl.when(pl.program_id(1) == 0)
    def _():
        u = scc_ref[...].astype(F32) * sch_ref[...].astype(F32)
        if short:
            shift = lambda k: _shift_short(u, past_ref[...], k, SC_CONV)
            utail_ref[...] = u
        else:
            halo_scr = refs[3]
            first = pl.program_id(0) % tiles_per_seq == 0
            halo = jnp.where(first, 0.0, halo_scr[...])
            shift = lambda k: _shift_long(u, halo, k)
            halo_scr[...] = u[tm - SUBLANES:]
            utail_ref[...] = u[tm - SUBLANES:]
        conv = _causal_conv(u, cw_ref[...], shift)
        s_scr[...] = (scb_ref[...].astype(F32) * conv).astype(BF16)

    a = _dot(yn_ref[...], wa_ref[...])
    b = _dot(s_scr[...], wb_ref[...])
    mixed = _sigmoid(ga_ref[...].astype(F32)) * a + _sigmoid(gb_ref[...].astype(F32)) * b
    mixed_ref[...] = mixed.astype(BF16)


def _mix(yn, proj, cw, wa, wb, seq_len, tm, tn, short, past=None):
    m = yn.shape[0]
    col = lambda c: (D_INNER + XBC_DIM) // D_MODEL + c
    in_specs = [
        pl.BlockSpec((tm, D_INNER), lambda i, j: (i, 0)),
        pl.BlockSpec((tm, D_MODEL), lambda i, j: (i, col(0))),
        pl.BlockSpec((tm, D_MODEL), lambda i, j: (i, col(1))),
        pl.BlockSpec((tm, D_MODEL), lambda i, j: (i, col(2))),
        pl.BlockSpec((tm, tn), lambda i, j: (i, col(3) * (D_MODEL // tn) + j)),
        pl.BlockSpec((tm, tn), lambda i, j: (i, col(4) * (D_MODEL // tn) + j)),
        pl.BlockSpec((SC_CONV, SC_DIM), lambda i, j: (0, 0)),
        pl.BlockSpec((D_INNER, tn), lambda i, j: (0, j)),
        pl.BlockSpec((SC_DIM, tn), lambda i, j: (0, j)),
    ]
    args = [yn, proj, proj, proj, proj, proj, cw, wa, wb]
    scratch = [pltpu.VMEM((tm, SC_DIM), BF16)]
    if short:
        in_specs.append(pl.BlockSpec((tm, SC_DIM), lambda i, j: (i, 0)))
        args.append(past)
        tail_rows = tm
    else:
        scratch.append(pltpu.VMEM((SUBLANES, SC_DIM), F32))
        tail_rows = SUBLANES
    return pl.pallas_call(
        functools.partial(_mix_kernel, short=short, tiles_per_seq=max(seq_len // tm, 1)),
        grid=(m // tm, D_MODEL // tn),
        in_specs=in_specs,
        out_specs=[pl.BlockSpec((tm, tn), lambda i, j: (i, j)),
                   pl.BlockSpec((tail_rows, SC_DIM), lambda i, j: (i, 0))],
        out_shape=[jax.ShapeDtypeStruct((m, D_MODEL), BF16),
                   jax.ShapeDtypeStruct((m // tm * tail_rows, SC_DIM), F32)],
        scratch_shapes=scratch,
        compiler_params=_cparams(("arbitrary", "arbitrary")),
        name="mix",
    )(*args)


def _out_proj_kernel(mixed_ref, x_ref, wo_ref, g_ref, x1_ref, n2_ref):
    x1 = x_ref[...] + _dot(mixed_ref[...], wo_ref[...])
    x1_ref[...] = x1
    n2_ref[...] = _rms(x1, g_ref[...]).astype(BF16)


def _out_proj(mixed, x, wo, g, tm):
    m = x.shape[0]
    row = lambda i: (i, 0)
    const = lambda i: (0, 0)
    return pl.pallas_call(
        _out_proj_kernel,
        grid=(m // tm,),
        in_specs=[pl.BlockSpec((tm, D_MODEL), row), pl.BlockSpec((tm, D_MODEL), row),
                  pl.BlockSpec((D_MODEL, D_MODEL), const), pl.BlockSpec((1, D_MODEL), const)],
        out_specs=[pl.BlockSpec((tm, D_MODEL), row), pl.BlockSpec((tm, D_MODEL), row)],
        out_shape=[jax.ShapeDtypeStruct((m, D_MODEL), F32),
                   jax.ShapeDtypeStruct((m, D_MODEL), BF16)],
        compiler_params=_cparams(("parallel",)),
        name="out_proj",
    )(mixed, x, wo, g)


def _ffn_up_kernel(*refs, short, tiles_per_seq):
    n2_ref, wg_ref, wu_ref, cw_ref, cb_ref = refs[:5]
    refs = refs[5:]
    if short:
        past_ref = refs[0]
        refs = refs[1:]
    act_ref, gtail_ref = refs[:2]
    tm = n2_ref.shape[0]
    n2 = n2_ref[...]
    gate = _dot(n2, wg_ref[...])
    up = _dot(n2, wu_ref[...])
    if short:
        shift = lambda k: _shift_short(gate, past_ref[...], k, FFN_CONV)
        gtail_ref[...] = gate
    else:
        halo_scr = refs[2]
        j = pl.program_id(1)
        first = pl.program_id(0) % tiles_per_seq == 0
        halo = jnp.where(first, 0.0, halo_scr[j])
        shift = lambda k: _shift_long(gate, halo, k)
        halo_scr[j] = gate[tm - SUBLANES:]
        gtail_ref[...] = gate[tm - SUBLANES:]
    conv = cb_ref[...] + _causal_conv(gate, cw_ref[...], shift)
    act_ref[...] = (_silu(conv) * up).astype(BF16)


def _ffn_up(n2, wg, wu, cw, cb, seq_len, tm, tn, short, past=None):
    m = n2.shape[0]
    nj = D_FF // tn
    in_specs = [
        pl.BlockSpec((tm, D_MODEL), lambda i, j: (i, 0)),
        pl.BlockSpec((D_MODEL, tn), lambda i, j: (0, j)),
        pl.BlockSpec((D_MODEL, tn), lambda i, j: (0, j)),
        pl.BlockSpec((FFN_CONV, tn), lambda i, j: (0, j)),
        pl.BlockSpec((1, tn), lambda i, j: (0, j)),
    ]
    args = [n2, wg, wu, cw, cb]
    scratch = []
    if short:
        in_specs.append(pl.BlockSpec((tm, tn), lambda i, j: (i, j)))
        args.append(past)
        tail_rows = tm
    else:
        scratch.append(pltpu.VMEM((nj, SUBLANES, tn), F32))
        tail_rows = SUBLANES
    return pl.pallas_call(
        functools.partial(_ffn_up_kernel, short=short, tiles_per_seq=max(seq_len // tm, 1)),
        grid=(m // tm, nj),
        in_specs=in_specs,
        out_specs=[pl.BlockSpec((tm, tn), lambda i, j: (i, j)),
                   pl.BlockSpec((tail_rows, tn), lambda i, j: (i, j))],
        out_shape=[jax.ShapeDtypeStruct((m, D_FF), BF16),
                   jax.ShapeDtypeStruct((m // tm * tail_rows, D_FF), F32)],
        scratch_shapes=scratch,
        compiler_params=_cparams(("arbitrary", "arbitrary")),
        name="ffn_up",
    )(*args)


def _ffn_down_kernel(act_ref, wd_ref, x1_ref, x2_ref):
    @pl.when(pl.program_id(1) == 0)
    def _():
        x2_ref[...] = x1_ref[...]

    x2_ref[...] += _dot(act_ref[...], wd_ref[...])


def _ffn_down(act, wd, x1, tm, tk):
    m = x1.shape[0]
    return pl.pallas_call(
        _ffn_down_kernel,
        grid=(m // tm, D_FF // tk),
        in_specs=[pl.BlockSpec((tm, tk), lambda i, k: (i, k)),
                  pl.BlockSpec((tk, D_MODEL), lambda i, k: (k, 0)),
                  pl.BlockSpec((tm, D_MODEL), lambda i, k: (i, 0))],
        out_specs=pl.BlockSpec((tm, D_MODEL), lambda i, k: (i, 0)),
        out_shape=jax.ShapeDtypeStruct((m, D_MODEL), F32),
        compiler_params=_cparams(("parallel", "arbitrary")),
        name="ffn_down",
    )(act, wd, x1)


def _ple_kernel(x2_ref, pe_ref, wple_ref, wgate_ref, gp_ref, gf_ref, y_ref):
    x2 = x2_ref[...]
    n3 = _rms(x2, gp_ref[...]).astype(BF16)
    gate = _sigmoid(_dot(n3, wgate_ref[...]))
    emb = _dot(pe_ref[...].astype(BF16), wple_ref[...])
    y_ref[...] = _rms(x2 + emb * gate, gf_ref[...])


def _ple(x2, pe, wple, wgate, gp, gf, tm):
    m = x2.shape[0]
    row = lambda i: (i, 0)
    const = lambda i: (0, 0)
    return pl.pallas_call(
        _ple_kernel,
        grid=(m // tm,),
        in_specs=[pl.BlockSpec((tm, D_MODEL), row), pl.BlockSpec((tm, PLE_DIM), row),
                  pl.BlockSpec((PLE_DIM, D_MODEL), const), pl.BlockSpec((D_MODEL, D_MODEL), const),
                  pl.BlockSpec((1, D_MODEL), const), pl.BlockSpec((1, D_MODEL), const)],
        out_specs=pl.BlockSpec((tm, D_MODEL), row),
        out_shape=jax.ShapeDtypeStruct((m, D_MODEL), F32),
        compiler_params=_cparams(("parallel",)),
        name="ple",
    )(x2, pe, wple, wgate, gp, gf)


def _pad_steps(state):
    s, k, c = state.shape
    return jnp.pad(state, ((0, 0), (0, DEC_SEQ - k), (0, 0))).reshape(s * DEC_SEQ, c)


def _tile(m, want):
    return min(m, want)


def _layer(x, pe, w, short, states=None):
    nseq, seq_len, _ = x.shape
    m = nseq * seq_len
    x2d = x.reshape(m, D_MODEL)
    pe2d = pe.reshape(m, PLE_DIM)
    if short:
        ssm0, xbc0, sc0, ffn0 = states
        past_xs = _pad_steps(xbc0[:, :, :D_INNER])
        past_bc = _pad_steps(xbc0[:, :, D_INNER:])
        past_sc = _pad_steps(sc0)
        past_ffn = _pad_steps(ffn0)
        h0 = ssm0.reshape(nseq, D_INNER, SSM_STATE)

    proj, dt_raw = _in_proj(x2d, w["norm_mix"], w["w_in"], w["w_dt"], _tile(m, 1024), 512)

    ssd_w = (w["cw_xs"], w["cb_xs"], w["cw_bc"], w["cb_bc"], w["dt_bias"], w["a_neg"],
             w["d_exp"], w["ssm_norm"])
    consts = _ssd_consts(SSM_CHUNK, short)
    if short:
        yn, ssm_new = _ssd(proj, dt_raw, consts, ssd_w, nseq, True, (past_xs, past_bc), h0)
    else:
        yn, ssm_new = _ssd(proj, dt_raw, consts, ssd_w, nseq, False)

    tm = _tile(m, 512)
    tm_mix = _tile(m, 256 if short else 512)
    mixed, u_tail = _mix(yn, proj, w["sc_conv_w"], w["w_ssm_out"], w["w_sc_out"], seq_len,
                         tm_mix, 512, short, past_sc if short else None)
    x1, n2 = _out_proj(mixed, x2d, w["w_o"], w["norm_ffn"], tm)

    tm_ffn = _tile(m, 1024)
    act, g_tail = _ffn_up(n2, w["w_ffn_gate"], w["w_ffn_up"], w["ffn_conv_w"], w["ffn_conv_b"],
                          seq_len, tm_ffn, 512, short, past_ffn if short else None)
    x2 = _ffn_down(act, w["w_ffn_down"], x1, tm, 512)
    y = _ple(x2, pe2d, w["w_ple"], w["w_ple_gate"], w["norm_ple"], w["norm_final"], tm)

    proj3 = proj.reshape(nseq, seq_len, PROJ_DIM)
    xbc_new = proj3[:, seq_len - (SSM_CONV - 1):, D_INNER:D_INNER + XBC_DIM].astype(F32)

    def last_rows(tail, tile_rows, taps, width):
        if short:
            t = tail.reshape(nseq, seq_len, width)
        else:
            per_seq = seq_len // tile_rows
            t = tail.reshape(nseq, per_seq, SUBLANES, width)[:, per_seq - 1]
        return t[:, t.shape[1] - (taps - 1):]

    sc_new = last_rows(u_tail, tm_mix, SC_CONV, SC_DIM)
    ffn_new = last_rows(g_tail, tm_ffn, FFN_CONV, D_FF)
    ssm_new = ssm_new.reshape(nseq, SSM_HEADS, SSM_HEAD_DIM, SSM_STATE)
    return y.reshape(nseq, seq_len, D_MODEL), ssm_new, xbc_new, sc_new, ffn_new


def _prep_weights(norm_mix, w_in, ssm_conv_w, ssm_conv_b, ssm_dt_bias, ssm_a_log, ssm_d, ssm_norm,
                  w_ssm_out, sc_conv_w, w_sc_out, w_o, norm_ffn, w_ffn_gate, w_ffn_up,
                  ffn_conv_w, ffn_conv_b, w_ffn_down, norm_ple, w_ple, w_ple_gate, norm_final):
    dt_lo = D_INNER + XBC_DIM
    dt_hi = dt_lo + SSM_HEADS
    row = lambda v: v.reshape(1, -1).astype(F32)
    pad_heads = lambda v: jnp.pad(v.astype(F32), (0, DT_PAD - SSM_HEADS)).reshape(1, DT_PAD)
    return {
        "norm_mix": row(norm_mix),
        "w_in": jnp.concatenate([w_in[:, :dt_lo], w_in[:, dt_hi:]], axis=1).astype(BF16),
        "w_dt": jnp.pad(w_in[:, dt_lo:dt_hi], ((0, 0), (0, DT_PAD - SSM_HEADS))).astype(BF16),
        "cw_xs": ssm_conv_w[:, :D_INNER].astype(F32),
        "cb_xs": row(ssm_conv_b[:D_INNER]),
        "cw_bc": ssm_conv_w[:, D_INNER:].astype(F32),
        "cb_bc": row(ssm_conv_b[D_INNER:]),
        "dt_bias": pad_heads(ssm_dt_bias),
        "a_neg": pad_heads(-jnp.exp(ssm_a_log.astype(F32))),
        "d_exp": row(jnp.repeat(ssm_d, SSM_HEAD_DIM)),
        "ssm_norm": row(ssm_norm),
        "w_ssm_out": w_ssm_out.astype(BF16),
        "sc_conv_w": sc_conv_w.astype(F32),
        "w_sc_out": w_sc_out.astype(BF16),
        "w_o": w_o.astype(BF16),
        "norm_ffn": row(norm_ffn),
        "w_ffn_gate": w_ffn_gate.astype(BF16),
        "w_ffn_up": w_ffn_up.astype(BF16),
        "ffn_conv_w": ffn_conv_w.astype(F32),
        "ffn_conv_b": row(ffn_conv_b),
        "w_ffn_down": w_ffn_down.astype(BF16),
        "norm_ple": row(norm_ple),
        "w_ple": w_ple.astype(BF16),
        "w_ple_gate": w_ple_gate.astype(BF16),
        "norm_final": row(norm_final),
    }


def kernel(x_prompt, x_sample, state_ssm, state_ssm_conv, state_short_conv, state_ffn_conv, p_prompt, p_sample, norm_mix, w_in, ssm_conv_w, ssm_conv_b, ssm_dt_bias, ssm_a_log, ssm_d, ssm_norm, w_ssm_out, sc_conv_w, w_sc_out, w_o, norm_ffn, w_ffn_gate, w_ffn_up, ffn_conv_w, ffn_conv_b, w_ffn_down, norm_ple, w_ple, w_ple_gate, norm_final):
    depth = w_in.shape[0]
    assert depth == 1, "the final norm is fused into the layer's last kernel"
    assert x_sample.shape[1] == DEC_SEQ
    layer_w = (norm_mix, w_in, ssm_conv_w, ssm_conv_b, ssm_dt_bias, ssm_a_log, ssm_d, ssm_norm,
               w_ssm_out, sc_conv_w, w_sc_out, w_o, norm_ffn, w_ffn_gate, w_ffn_up,
               ffn_conv_w, ffn_conv_b, w_ffn_down, norm_ple, w_ple, w_ple_gate)
    w = _prep_weights(*[t[0] for t in layer_w], norm_final)
    yp, *prompt_states = _layer(x_prompt, p_prompt[0], w, short=False)
    ys, *sample_states = _layer(
        x_sample, p_sample[0], w, short=True,
        states=(state_ssm[0], state_ssm_conv[0], state_short_conv[0], state_ffn_conv[0]))
    outs = [yp, ys]
    for sp, ss in zip(prompt_states, sample_states):
        outs += [sp[None], ss[None]]
    return tuple(outs)
```

```python
import functools

import numpy as np
import jax
import jax.numpy as jnp
from jax import lax
from jax.experimental import pallas as pl
from jax.experimental.pallas import tpu as pltpu

F32 = jnp.float32
BF16 = jnp.bfloat16

D_MODEL = 2048
D_INNER = 4096
SSM_HEADS = 64
SSM_HEAD_DIM = 64
SSM_GROUPS = 8
HEADS_PER_GROUP = SSM_HEADS // SSM_GROUPS
GROUP_WIDTH = D_INNER // SSM_GROUPS
SSM_STATE = 128
SSM_CHUNK = 128
SSM_CONV = 4
BC_DIM = 2 * SSM_GROUPS * SSM_STATE
XBC_DIM = D_INNER + BC_DIM
SC_DIM = D_MODEL
SC_CONV = 3
D_FF = 5632
FFN_CONV = 3
PLE_DIM = 256
EPS = 1e-6
DEC_SEQ = 8
PROJ_DIM = D_INNER + XBC_DIM + 5 * D_MODEL
DT_PAD = 128

SUBLANES = 8
VMEM_LIMIT = 52 * 1024 * 1024


def _cparams(sem):
    return pltpu.CompilerParams(dimension_semantics=sem, vmem_limit_bytes=VMEM_LIMIT)


def _dot(a, b):
    return jnp.dot(a, b, preferred_element_type=F32)


def _dot_nt(a, b):
    return lax.dot_general(a, b, (((1,), (1,)), ((), ())), preferred_element_type=F32)


def _sigmoid(x):
    return 0.5 * jnp.tanh(0.5 * x) + 0.5


def _silu(x):
    h = 0.5 * x
    return h * jnp.tanh(h) + h


def _softplus(x):
    return jnp.maximum(x, 0.0) + jnp.log1p(jnp.exp(-jnp.abs(x)))


def _rms(x, g):
    ms = jnp.mean(x * x, axis=-1, keepdims=True)
    return x * lax.rsqrt(ms + EPS) * g


def _split_bf16(a, terms):
    parts = []
    r = a
    for _ in range(terms - 1):
        p = r.astype(BF16)
        parts.append(p)
        r = r - p.astype(F32)
    parts.append(r.astype(BF16))
    return parts


def _dot_sel_right(a, sel, terms):
    out = None
    for p in _split_bf16(a, terms):
        d = _dot(p, sel)
        out = d if out is None else out + d
    return out


def _dot_sel_left(sel, a, terms):
    out = None
    for p in _split_bf16(a, terms):
        d = _dot(sel, p)
        out = d if out is None else out + d
    return out


def _shift_long(x, halo, k):
    r = pltpu.roll(x, k, 0)
    h = pltpu.roll(halo, k, 0)
    row = lax.broadcasted_iota(jnp.int32, halo.shape, 0)
    first = jnp.where(row < k, h, r[:SUBLANES])
    return jnp.concatenate([first, r[SUBLANES:]], axis=0)


def _shift_short(x, past, k, taps):
    n = x.shape[0]
    r = pltpu.roll(x, k, 0)
    s = (n + k - (taps - 1)) % n
    p = past if s == 0 else pltpu.roll(past, s, 0)
    t = lax.broadcasted_iota(jnp.int32, x.shape, 0) & (DEC_SEQ - 1)
    return jnp.where(t < k, p, r)


def _causal_conv(x, w, shift):
    taps = w.shape[0]
    y = w[taps - 1:taps] * x
    for k in range(1, taps):
        y = y + w[taps - 1 - k:taps - k] * shift(k)
    return y


def _in_proj_kernel(x_ref, g_ref, wlo_ref, whi_ref, wdt_ref, proj_ref, dt_ref, n1_scr, *, n_lo):
    j = pl.program_id(1)

    @pl.when(j == 0)
    def _():
        n1 = _rms(x_ref[...], g_ref[...]).astype(BF16)
        n1_scr[...] = n1
        dt_ref[...] = _dot(n1, wdt_ref[...])

    @pl.when(j < n_lo)
    def _():
        proj_ref[...] = _dot(n1_scr[...], wlo_ref[...]).astype(BF16)

    @pl.when(j >= n_lo)
    def _():
        proj_ref[...] = _dot(n1_scr[...], whi_ref[...]).astype(BF16)


def _in_proj(x, g, w_lo, w_hi, wdt, tm, tn):
    m = x.shape[0]
    n_lo = w_lo.shape[1] // tn
    return pl.pallas_call(
        functools.partial(_in_proj_kernel, n_lo=n_lo),
        grid=(m // tm, PROJ_DIM // tn),
        in_specs=[
            pl.BlockSpec((tm, D_MODEL), lambda i, j: (i, 0), pipeline_mode=pl.Buffered(1)),
            pl.BlockSpec((1, D_MODEL), lambda i, j: (0, 0)),
            pl.BlockSpec((D_MODEL, tn), lambda i, j: (0, jnp.minimum(j, n_lo - 1))),
            pl.BlockSpec((D_MODEL, tn), lambda i, j: (0, jnp.maximum(j - n_lo, 0))),
            pl.BlockSpec((D_MODEL, DT_PAD), lambda i, j: (0, 0)),
        ],
        out_specs=[
            pl.BlockSpec((tm, tn), lambda i, j: (i, j)),
            pl.BlockSpec((tm, DT_PAD), lambda i, j: (i, 0)),
        ],
        out_shape=[
            jax.ShapeDtypeStruct((m, PROJ_DIM), BF16),
            jax.ShapeDtypeStruct((m, DT_PAD), F32),
        ],
        scratch_shapes=[pltpu.VMEM((tm, D_MODEL), BF16)],
        compiler_params=_cparams(("parallel", "arbitrary")),
        name="in_proj",
    )(x, g, w_lo, w_hi, wdt)


def _ssd_consts(q, short):
    l = np.arange(q)[:, None]
    s = np.arange(q)[None, :]
    if short:
        tri = (l // DEC_SEQ == s // DEC_SEQ) & (s <= l)
        end = s == (l // DEC_SEQ) * DEC_SEQ + DEC_SEQ - 1
    else:
        tri = s <= l
        end = np.broadcast_to(s == q - 1, (q, q))
    mask = np.where(tri, 0.0, -np.inf).astype(np.float32)
    h = np.arange(128)[None, :, None]
    g = np.arange(SSM_GROUPS)[:, None, None]
    c = np.arange(GROUP_WIDTH)[None, None, :]
    expand = h == g * HEADS_PER_GROUP + c // SSM_HEAD_DIM
    return (jnp.asarray(tri, BF16), jnp.asarray(end, BF16), jnp.asarray(mask),
            jnp.asarray(expand, BF16))


def _ssd_kernel(*refs, short, q):
    (z_ref, xs_ref, bc_ref, dt_ref, tri_ref, end_ref, mask_ref, exp_ref,
     cwx_ref, cbx_ref, cwb_ref, cbb_ref, dtb_ref, a_ref, dexp_ref, gn_ref) = refs[:16]
    refs = refs[16:]
    if short:
        past_xs_ref, past_bc_ref, h0_ref = refs[:3]
        refs = refs[3:]
    y_ref, ht_ref = refs[:2]
    refs = refs[2:]
    xs_scr, b_scr, c_scr, y_scr, ea_scr, we_scr, aend_scr, acumt_scr, arow_scr = refs[:9]
    refs = refs[9:]
    if short:
        cdec_scr, eae_scr, xwt_scr = refs
    else:
        h_scr, tail_xs, tail_bc = refs

    step = pl.program_id(1)
    nstep = pl.num_programs(1)

    def conv_prelude():
        cwx = cwx_ref[...]
        cwb = cwb_ref[...]
        for g in range(SSM_GROUPS):
            cs = slice(g * GROUP_WIDTH, (g + 1) * GROUP_WIDTH)
            x = xs_ref[:, cs].astype(F32)
            if short:
                past = past_xs_ref[:, cs]
                shift = lambda k, x=x, past=past: _shift_short(x, past, k, SSM_CONV)
            else:
                halo = tail_xs[:, cs]
                shift = lambda k, x=x, halo=halo: _shift_long(x, halo, k)
            xs_scr[g] = _silu(cbx_ref[:, cs] + _causal_conv(x, cwx[:, cs], shift))
            if not short:
                tail_xs[:, cs] = x[q - SUBLANES:]
        for blk in range(BC_DIM // GROUP_WIDTH):
            cs = slice(blk * GROUP_WIDTH, (blk + 1) * GROUP_WIDTH)
            x = bc_ref[:, cs].astype(F32)
            if short:
                past = past_bc_ref[:, cs]
                shift = lambda k, x=x, past=past: _shift_short(x, past, k, SSM_CONV)
            else:
                halo = tail_bc[:, cs]
                shift = lambda k, x=x, halo=halo: _shift_long(x, halo, k)
            v = _silu(cbb_ref[:, cs] + _causal_conv(x, cwb[:, cs], shift))
            if not short:
                tail_bc[:, cs] = x[q - SUBLANES:]
            per_blk = GROUP_WIDTH // SSM_STATE
            dst = b_scr if blk < SSM_GROUPS // per_blk else c_scr
            for r in range(per_blk):
                dst[(blk * per_blk + r) % SSM_GROUPS] = v[:, r * SSM_STATE:(r + 1) * SSM_STATE]

    def dt_prelude():
        dt = _softplus(dt_ref[...] + dtb_ref[...])
        acum = _dot_sel_left(tri_ref[...], dt * a_ref[...], 3)
        aend = _dot_sel_left(end_ref[...], acum, 3)
        acumt = acum.T
        acumt_scr[...] = acumt
        arow_scr[...] = acumt - jnp.log(dt.T)
        aend_scr[...] = aend
        ea_scr[...] = jnp.exp(acum)
        we_scr[...] = jnp.exp(aend - acum) * dt

    def diag_group(g):
        xs = xs_scr[g]
        cb = _dot_nt(c_scr[g].astype(BF16), b_scr[g].astype(BF16))
        mask = mask_ref[...]
        lane = lax.broadcasted_iota(jnp.int32, (q, 128), 1)
        ys = []
        for pair in range(HEADS_PER_GROUP // 2):
            ws = []
            for half in range(2):
                h = g * HEADS_PER_GROUP + 2 * pair + half
                colb = jnp.broadcast_to(acumt_scr[pl.ds(h, 1), :], (q, q)).T
                seg = colb - arow_scr[pl.ds(h, 1), :] + mask
                ws.append((cb * jnp.exp(seg)).astype(BF16))
            xp = xs[:, pair * 128:(pair + 1) * 128]
            top = jnp.where(lane < SSM_HEAD_DIM, xp, 0.0).astype(BF16)
            bot = jnp.where(lane >= SSM_HEAD_DIM, xp, 0.0).astype(BF16)
            ys.append(_dot(jnp.concatenate(ws, axis=1), jnp.concatenate([top, bot], axis=0)))
        return jnp.concatenate(ys, axis=1), xs, exp_ref[g]

    def epilogue():
        for g in range(SSM_GROUPS):
            cs = slice(g * GROUP_WIDTH, (g + 1) * GROUP_WIDTH)
            y = y_scr[g] + dexp_ref[:, cs] * xs_scr[g]
            y = y * _silu(z_ref[:, cs].astype(F32))
            ms = jnp.mean(y * y, axis=-1, keepdims=True)
            y_ref[:, cs] = (y * lax.rsqrt(ms + EPS) * gn_ref[:, cs]).astype(BF16)

    if not short:
        @pl.when(step == 0)
        def _():
            h_scr[...] = jnp.zeros_like(h_scr)
            tail_xs[...] = jnp.zeros_like(tail_xs)
            tail_bc[...] = jnp.zeros_like(tail_bc)

        conv_prelude()
        dt_prelude()

        def group_body(g, carry):
            y_diag, xs, expand = diag_group(g)
            hg = h_scr[g]
            y_off = _dot(c_scr[g].astype(BF16), hg.astype(BF16))
            y_scr[g] = y_diag + y_off * _dot_sel_right(ea_scr[...], expand, 2)
            xw = (xs * _dot_sel_right(we_scr[...], expand, 2)).astype(BF16)
            st = _dot(b_scr[g].T.astype(BF16), xw)
            dec = jnp.exp(_dot_sel_right(aend_scr[:SUBLANES, :], expand, 3))[:1]
            h_scr[g] = dec * hg + st
            return carry

        lax.fori_loop(0, SSM_GROUPS, group_body, 0, unroll=True)
        epilogue()

        @pl.when(step == nstep - 1)
        def _():
            for g in range(SSM_GROUPS):
                ht_ref[0, g * GROUP_WIDTH:(g + 1) * GROUP_WIDTH, :] = h_scr[g].T
    else:
        @pl.when(step == 0)
        def _():
            conv_prelude()
            dt_prelude()

            def shared_body(g, carry):
                y_diag, xs, expand = diag_group(g)
                y_scr[g] = y_diag
                eae_scr[g] = _dot_sel_right(ea_scr[...], expand, 2)
                xwt_scr[g] = (xs * _dot_sel_right(we_scr[...], expand, 2)).T.astype(BF16)
                return carry

            lax.fori_loop(0, SSM_GROUPS, shared_body, 0, unroll=True)

        last_row = step * DEC_SEQ + DEC_SEQ - 1
        row = lax.broadcasted_iota(jnp.int32, (q, 128), 0)
        onehot = jnp.where(row == last_row, 1.0, 0.0).astype(BF16)
        cdec_scr[...] = jnp.exp(_dot_sel_right(acumt_scr[...], onehot, 3))
        seq_rows = pl.ds(pl.multiple_of(step * DEC_SEQ, DEC_SEQ), DEC_SEQ)

        def seq_body(g, carry):
            row = lax.broadcasted_iota(jnp.int32, (q, SSM_STATE), 0)
            in_seq = (row // DEC_SEQ) == step
            rows = pl.ds(pl.multiple_of(g * GROUP_WIDTH, GROUP_WIDTH), GROUP_WIDTH)
            hg = h0_ref[0, rows, :]
            cj = c_scr[g, seq_rows, :].astype(BF16)
            y_off = _dot_nt(cj, hg.astype(BF16))
            y_scr[g, seq_rows, :] = y_scr[g, seq_rows, :] + y_off * eae_scr[g, seq_rows, :]
            bj = jnp.where(in_seq, b_scr[g], 0.0).astype(BF16)
            st = _dot(xwt_scr[g], bj)
            dec = [jnp.broadcast_to(cdec_scr[pl.ds(g * HEADS_PER_GROUP + r, 1), :],
                                    (SSM_HEAD_DIM, 128)) for r in range(HEADS_PER_GROUP)]
            ht_ref[0, rows, :] = jnp.concatenate(dec, axis=0) * hg + st
            return carry

        lax.fori_loop(0, SSM_GROUPS, seq_body, 0, unroll=True)

        @pl.when(step == nstep - 1)
        def _():
            epilogue()


def _ssd(proj, dt_raw, consts, wts, nseq, short, past=None, h0=None):
    m = proj.shape[0]
    q = SSM_CHUNK
    nblk = m // q
    if short:
        steps = q // DEC_SEQ
        grid = (nblk, steps)
        blk = lambda a, b: a
        st_idx = lambda a, b: (a * steps + b, 0, 0)
    else:
        steps = m // nseq // q
        grid = (nseq, steps)
        blk = lambda a, b: a * steps + b
        st_idx = lambda a, b: (a, 0, 0)
    const2 = lambda shape: pl.BlockSpec(shape, lambda a, b: (0, 0))
    const3 = lambda shape: pl.BlockSpec(shape, lambda a, b: (0, 0, 0))
    in_specs = [
        pl.BlockSpec((q, D_INNER), lambda a, b: (blk(a, b), 0)),
        pl.BlockSpec((q, D_INNER), lambda a, b: (blk(a, b), 1)),
        pl.BlockSpec((q, BC_DIM), lambda a, b: (blk(a, b), 2 * D_INNER // BC_DIM)),
        pl.BlockSpec((q, DT_PAD), lambda a, b: (blk(a, b), 0)),
        const2((q, q)), const2((q, q)), const2((q, q)),
        const3((SSM_GROUPS, 128, GROUP_WIDTH)),
        const2((SSM_CONV, D_INNER)), const2((1, D_INNER)),
        const2((SSM_CONV, BC_DIM)), const2((1, BC_DIM)),
        const2((1, DT_PAD)), const2((1, DT_PAD)),
        const2((1, D_INNER)), const2((1, D_INNER)),
    ]
    args = [proj, proj, proj, dt_raw, *consts, *wts]
    state_spec = pl.BlockSpec((1, D_INNER, SSM_STATE), st_idx)
    scratch = [
        pltpu.VMEM((SSM_GROUPS, q, GROUP_WIDTH), F32),
        pltpu.VMEM((SSM_GROUPS, q, SSM_STATE), F32),
        pltpu.VMEM((SSM_GROUPS, q, SSM_STATE), F32),
        pltpu.VMEM((SSM_GROUPS, q, GROUP_WIDTH), F32),
        pltpu.VMEM((q, 128), F32), pltpu.VMEM((q, 128), F32), pltpu.VMEM((q, 128), F32),
        pltpu.VMEM((128, q), F32), pltpu.VMEM((128, q), F32),
    ]
    if short:
        in_specs += [
            pl.BlockSpec((q, D_INNER), lambda a, b: (a, 0)),
            pl.BlockSpec((q, BC_DIM), lambda a, b: (a, 0)),
            state_spec,
        ]
        args += [*past, h0]
        scratch += [pltpu.VMEM((128, 128), F32),
                    pltpu.VMEM((SSM_GROUPS, q, GROUP_WIDTH), F32),
                    pltpu.VMEM((SSM_GROUPS, GROUP_WIDTH, q), BF16)]
    else:
        scratch += [pltpu.VMEM((SSM_GROUPS, SSM_STATE, GROUP_WIDTH), F32),
                    pltpu.VMEM((SUBLANES, D_INNER), F32),
                    pltpu.VMEM((SUBLANES, BC_DIM), F32)]
    return pl.pallas_call(
        functools.partial(_ssd_kernel, short=short, q=q),
        grid=grid,
        in_specs=in_specs,
        out_specs=[pl.BlockSpec((q, D_INNER), lambda a, b: (blk(a, b), 0)), state_spec],
        out_shape=[jax.ShapeDtypeStruct((m, D_INNER), BF16),
                   jax.ShapeDtypeStruct((nseq, D_INNER, SSM_STATE), F32)],
        scratch_shapes=scratch,
        compiler_params=_cparams(("arbitrary", "arbitrary")),
        name="ssd_short" if short else "ssd_long",
    )(*args)


def _mix_kernel(*refs, short, tiles_per_seq):
    yn_ref, scb_ref, scc_ref, sch_ref, ga_ref, gb_ref, cw_ref, wa_ref, wb_ref = refs[:9]
    refs = refs[9:]
    if short:
        past_ref = refs[0]
        refs = refs[1:]
    mixed_ref, utail_ref = refs[:2]
    tm = yn_ref.shape[0]

    u = scc_ref[...].astype(F32) * sch_ref[...].astype(F32)
    if short:
        shift = lambda k: _shift_short(u, past_ref[...], k, SC_CONV)
        utail_ref[...] = u
    else:
        halo_scr = refs[2]
        first = pl.program_id(0) % tiles_per_seq == 0
        halo = jnp.where(first, 0.0, halo_scr[...])
        shift = lambda k: _shift_long(u, halo, k)
        halo_scr[...] = u[tm - SUBLANES:]
        utail_ref[...] = u[tm - SUBLANES:]
    s = (scb_ref[...].astype(F32) * _causal_conv(u, cw_ref[...], shift)).astype(BF16)

    a = _dot(yn_ref[...], wa_ref[...])
    b = _dot(s, wb_ref[...])
    mixed = _sigmoid(ga_ref[...].astype(F32)) * a + _sigmoid(gb_ref[...].astype(F32)) * b
    mixed_ref[...] = mixed.astype(BF16)


def _mix(yn, proj, cw, wa, wb, seq_len, tm, short, past=None):
    m = yn.shape[0]
    col = lambda c: (D_INNER + XBC_DIM) // D_MODEL + c
    rows = lambda c: pl.BlockSpec((tm, D_MODEL), lambda i: (i, col(c)))
    resident = lambda shape: pl.BlockSpec(shape, lambda i: (0, 0), pipeline_mode=pl.Buffered(1))
    in_specs = [
        pl.BlockSpec((tm, D_INNER), lambda i: (i, 0)),
        rows(0), rows(1), rows(2), rows(3), rows(4),
        resident((SC_CONV, SC_DIM)),
        resident((D_INNER, D_MODEL)),
        resident((SC_DIM, D_MODEL)),
    ]
    args = [yn, proj, proj, proj, proj, proj, cw, wa, wb]
    scratch = []
    if short:
        in_specs.append(pl.BlockSpec((tm, SC_DIM), lambda i: (i, 0)))
        args.append(past)
        tail_rows = tm
    else:
        scratch.append(pltpu.VMEM((SUBLANES, SC_DIM), F32))
        tail_rows = SUBLANES
    return pl.pallas_call(
        functools.partial(_mix_kernel, short=short, tiles_per_seq=max(seq_len // tm, 1)),
        grid=(m // tm,),
        in_specs=in_specs,
        out_specs=[pl.BlockSpec((tm, D_MODEL), lambda i: (i, 0)),
                   pl.BlockSpec((tail_rows, SC_DIM), lambda i: (i, 0))],
        out_shape=[jax.ShapeDtypeStruct((m, D_MODEL), BF16),
                   jax.ShapeDtypeStruct((m // tm * tail_rows, SC_DIM), F32)],
        scratch_shapes=scratch,
        compiler_params=_cparams(("arbitrary",)),
        name="mix",
    )(*args)


def _out_proj_kernel(mixed_ref, x_ref, wo_ref, g_ref, x1_ref, n2_ref):
    x1 = x_ref[...] + _dot(mixed_ref[...], wo_ref[...])
    x1_ref[...] = x1
    n2_ref[...] = _rms(x1, g_ref[...]).astype(BF16)


def _out_proj(mixed, x, wo, g, tm):
    m = x.shape[0]
    row = lambda i: (i, 0)
    const = lambda i: (0, 0)
    return pl.pallas_call(
        _out_proj_kernel,
        grid=(m // tm,),
        in_specs=[pl.BlockSpec((tm, D_MODEL), row), pl.BlockSpec((tm, D_MODEL), row),
                  pl.BlockSpec((D_MODEL, D_MODEL), const), pl.BlockSpec((1, D_MODEL), const)],
        out_specs=[pl.BlockSpec((tm, D_MODEL), row), pl.BlockSpec((tm, D_MODEL), row)],
        out_shape=[jax.ShapeDtypeStruct((m, D_MODEL), F32),
                   jax.ShapeDtypeStruct((m, D_MODEL), BF16)],
        compiler_params=_cparams(("parallel",)),
        name="out_proj",
    )(mixed, x, wo, g)


def _ffn_up_kernel(*refs, short, tiles_per_seq, splits):
    n2_ref, wg_ref, wu_ref, cw_ref, cb_ref = refs[:5]
    refs = refs[5:]
    if short:
        past_ref = refs[0]
        refs = refs[1:]
    act_ref, gtail_ref = refs[:2]
    tm = n2_ref.shape[0]
    n2 = n2_ref[...]
    j = pl.program_id(1)
    first = pl.program_id(0) % tiles_per_seq == 0
    lo = 0
    for width in splits:
        cs = slice(lo, lo + width)
        lo += width
        gate = _dot(n2, wg_ref[:, cs])
        up = _dot(n2, wu_ref[:, cs])
        if short:
            past = past_ref[:, cs]
            shift = lambda k, gate=gate, past=past: _shift_short(gate, past, k, FFN_CONV)
            gtail_ref[:, cs] = gate
        else:
            halo_scr = refs[2]
            halo = jnp.where(first, 0.0, halo_scr[j, :, cs])
            shift = lambda k, gate=gate, halo=halo: _shift_long(gate, halo, k)
            halo_scr[j, :, cs] = gate[tm - SUBLANES:]
            gtail_ref[:, cs] = gate[tm - SUBLANES:]
        conv = cb_ref[:, cs] + _causal_conv(gate, cw_ref[:, cs], shift)
        act_ref[:, cs] = (_silu(conv) * up).astype(BF16)


def _ffn_up(n2, wg, wu, cw, cb, seq_len, tm, tn, splits, short, past=None):
    m = n2.shape[0]
    nj = D_FF // tn
    assert sum(splits) == tn
    in_specs = [
        pl.BlockSpec((tm, D_MODEL), lambda i, j: (i, 0)),
        pl.BlockSpec((D_MODEL, tn), lambda i, j: (0, j)),
        pl.BlockSpec((D_MODEL, tn), lambda i, j: (0, j)),
        pl.BlockSpec((FFN_CONV, tn), lambda i, j: (0, j)),
        pl.BlockSpec((1, tn), lambda i, j: (0, j)),
    ]
    args = [n2, wg, wu, cw, cb]
    scratch = []
    if short:
        in_specs.append(pl.BlockSpec((tm, tn), lambda i, j: (i, j)))
        args.append(past)
        tail_rows = tm
    else:
        scratch.append(pltpu.VMEM((nj, SUBLANES, tn), F32))
        tail_rows = SUBLANES
    return pl.pallas_call(
        functools.partial(_ffn_up_kernel, short=short, tiles_per_seq=max(seq_len // tm, 1),
                          splits=splits),
        grid=(m // tm, nj),
        in_specs=in_specs,
        out_specs=[pl.BlockSpec((tm, tn), lambda i, j: (i, j)),
                   pl.BlockSpec((tail_rows, tn), lambda i, j: (i, j))],
        out_shape=[jax.ShapeDtypeStruct((m, D_FF), BF16),
                   jax.ShapeDtypeStruct((m // tm * tail_rows, D_FF), F32)],
        scratch_shapes=scratch,
        compiler_params=_cparams(("arbitrary", "arbitrary")),
        name="ffn_up",
    )(*args)


def _ffn_down_kernel(act_ref, wd_ref, x1_ref, x2_ref):
    @pl.when(pl.program_id(1) == 0)
    def _():
        x2_ref[...] = x1_ref[...]

    x2_ref[...] += _dot(act_ref[...], wd_ref[...])


def _ffn_down(act, wd, x1, tm, tk):
    m = x1.shape[0]
    return pl.pallas_call(
        _ffn_down_kernel,
        grid=(m // tm, D_FF // tk),
        in_specs=[pl.BlockSpec((tm, tk), lambda i, k: (i, k)),
                  pl.BlockSpec((tk, D_MODEL), lambda i, k: (k, 0)),
                  pl.BlockSpec((tm, D_MODEL), lambda i, k: (i, 0))],
        out_specs=pl.BlockSpec((tm, D_MODEL), lambda i, k: (i, 0)),
        out_shape=jax.ShapeDtypeStruct((m, D_MODEL), F32),
        compiler_params=_cparams(("parallel", "arbitrary")),
        name="ffn_down",
    )(act, wd, x1)


def _ple_kernel(x2_ref, pe_ref, wple_ref, wgate_ref, gp_ref, gf_ref, y_ref):
    x2 = x2_ref[...]
    n3 = _rms(x2, gp_ref[...]).astype(BF16)
    gate = _sigmoid(_dot(n3, wgate_ref[...]))
    emb = _dot(pe_ref[...].astype(BF16), wple_ref[...])
    y_ref[...] = _rms(x2 + emb * gate, gf_ref[...])


def _ple(x2, pe, wple, wgate, gp, gf, tm):
    m = x2.shape[0]
    row = lambda i: (i, 0)
    const = lambda i: (0, 0)
    return pl.pallas_call(
        _ple_kernel,
        grid=(m // tm,),
        in_specs=[pl.BlockSpec((tm, D_MODEL), row), pl.BlockSpec((tm, PLE_DIM), row),
                  pl.BlockSpec((PLE_DIM, D_MODEL), const), pl.BlockSpec((D_MODEL, D_MODEL), const),
                  pl.BlockSpec((1, D_MODEL), const), pl.BlockSpec((1, D_MODEL), const)],
        out_specs=pl.BlockSpec((tm, D_MODEL), row),
        out_shape=jax.ShapeDtypeStruct((m, D_MODEL), F32),
        compiler_params=_cparams(("parallel",)),
        name="ple",
    )(x2, pe, wple, wgate, gp, gf)


def _pad_steps(state):
    s, k, c = state.shape
    return jnp.pad(state, ((0, 0), (0, DEC_SEQ - k), (0, 0))).reshape(s * DEC_SEQ, c)


def _tile(m, want):
    return min(m, want)


def _layer(x, pe, w, short, states=None):
    nseq, seq_len, _ = x.shape
    m = nseq * seq_len
    x2d = x.reshape(m, D_MODEL)
    pe2d = pe.reshape(m, PLE_DIM)
    if short:
        ssm0, xbc0, sc0, ffn0 = states
        past_xs = _pad_steps(xbc0[:, :, :D_INNER])
        past_bc = _pad_steps(xbc0[:, :, D_INNER:])
        past_sc = _pad_steps(sc0)
        past_ffn = _pad_steps(ffn0)
        h0 = ssm0.reshape(nseq, D_INNER, SSM_STATE)

    proj, dt_raw = _in_proj(x2d, w["norm_mix"], w["w_in_lo"], w["w_in_hi"], w["w_dt"],
                            _tile(m, 1024), 1024)

    ssd_w = (w["cw_xs"], w["cb_xs"], w["cw_bc"], w["cb_bc"], w["dt_bias"], w["a_neg"],
             w["d_exp"], w["ssm_norm"])
    consts = _ssd_consts(SSM_CHUNK, short)
    if short:
        yn, ssm_new = _ssd(proj, dt_raw, consts, ssd_w, nseq, True, (past_xs, past_bc), h0)
    else:
        yn, ssm_new = _ssd(proj, dt_raw, consts, ssd_w, nseq, False)

    tm = _tile(m, 512)
    tm_mix = _tile(m, 128 if short else 256)
    mixed, u_tail = _mix(yn, proj, w["sc_conv_w"], w["w_ssm_out"], w["w_sc_out"], seq_len,
                         tm_mix, short, past_sc if short else None)
    x1, n2 = _out_proj(mixed, x2d, w["w_o"], w["norm_ffn"], tm)

    tm_ffn = _tile(m, 1024)
    tn_ffn, splits = (512, (256, 256)) if short else (1408, (512, 512, 384))
    act, g_tail = _ffn_up(n2, w["w_ffn_gate"], w["w_ffn_up"], w["ffn_conv_w"], w["ffn_conv_b"],
                          seq_len, tm_ffn, tn_ffn, splits, short, past_ffn if short else None)
    x2 = _ffn_down(act, w["w_ffn_down"], x1, tm, 1408)
    y = _ple(x2, pe2d, w["w_ple"], w["w_ple_gate"], w["norm_ple"], w["norm_final"], tm)

    proj3 = proj.reshape(nseq, seq_len, PROJ_DIM)
    xbc_new = proj3[:, seq_len - (SSM_CONV - 1):, D_INNER:D_INNER + XBC_DIM].astype(F32)

    def last_rows(tail, tile_rows, taps, width):
        if short:
            t = tail.reshape(nseq, seq_len, width)
        else:
            per_seq = seq_len // tile_rows
            t = tail.reshape(nseq, per_seq, SUBLANES, width)[:, per_seq - 1]
        return t[:, t.shape[1] - (taps - 1):]

    sc_new = last_rows(u_tail, tm_mix, SC_CONV, SC_DIM)
    ffn_new = last_rows(g_tail, tm_ffn, FFN_CONV, D_FF)
    ssm_new = ssm_new.reshape(nseq, SSM_HEADS, SSM_HEAD_DIM, SSM_STATE)
    return y.reshape(nseq, seq_len, D_MODEL), ssm_new, xbc_new, sc_new, ffn_new


def _prep_weights(norm_mix, w_in, ssm_conv_w, ssm_conv_b, ssm_dt_bias, ssm_a_log, ssm_d, ssm_norm,
                  w_ssm_out, sc_conv_w, w_sc_out, w_o, norm_ffn, w_ffn_gate, w_ffn_up,
                  ffn_conv_w, ffn_conv_b, w_ffn_down, norm_ple, w_ple, w_ple_gate, norm_final):
    dt_lo = D_INNER + XBC_DIM
    dt_hi = dt_lo + SSM_HEADS
    row = lambda v: v.reshape(1, -1).astype(F32)
    pad_heads = lambda v: jnp.pad(v.astype(F32), (0, DT_PAD - SSM_HEADS)).reshape(1, DT_PAD)
    return {
        "norm_mix": row(norm_mix),
        "w_in_lo": w_in[:, :dt_lo].astype(BF16),
        "w_in_hi": w_in[:, dt_hi:].astype(BF16),
        "w_dt": jnp.pad(w_in[:, dt_lo:dt_hi], ((0, 0), (0, DT_PAD - SSM_HEADS))).astype(BF16),
        "cw_xs": ssm_conv_w[:, :D_INNER].astype(F32),
        "cb_xs": row(ssm_conv_b[:D_INNER]),
        "cw_bc": ssm_conv_w[:, D_INNER:].astype(F32),
        "cb_bc": row(ssm_conv_b[D_INNER:]),
        "dt_bias": pad_heads(ssm_dt_bias),
        "a_neg": pad_heads(-jnp.exp(ssm_a_log.astype(F32))),
        "d_exp": row(jnp.repeat(ssm_d, SSM_HEAD_DIM)),
        "ssm_norm": row(ssm_norm),
        "w_ssm_out": w_ssm_out.astype(BF16),
        "sc_conv_w": sc_conv_w.astype(F32),
        "w_sc_out": w_sc_out.astype(BF16),
        "w_o": w_o.astype(BF16),
        "norm_ffn": row(norm_ffn),
        "w_ffn_gate": w_ffn_gate.astype(BF16),
        "w_ffn_up": w_ffn_up.astype(BF16),
        "ffn_conv_w": ffn_conv_w.astype(F32),
        "ffn_conv_b": row(ffn_conv_b),
        "w_ffn_down": w_ffn_down.astype(BF16),
        "norm_ple": row(norm_ple),
        "w_ple": w_ple.astype(BF16),
        "w_ple_gate": w_ple_gate.astype(BF16),
        "norm_final": row(norm_final),
    }


def kernel(x_prompt, x_sample, state_ssm, state_ssm_conv, state_short_conv, state_ffn_conv, p_prompt, p_sample, norm_mix, w_in, ssm_conv_w, ssm_conv_b, ssm_dt_bias, ssm_a_log, ssm_d, ssm_norm, w_ssm_out, sc_conv_w, w_sc_out, w_o, norm_ffn, w_ffn_gate, w_ffn_up, ffn_conv_w, ffn_conv_b, w_ffn_down, norm_ple, w_ple, w_ple_gate, norm_final):
    depth = w_in.shape[0]
    assert depth == 1, "the final norm is fused into the layer's last kernel"
    assert x_sample.shape[1] == DEC_SEQ
    layer_w = (norm_mix, w_in, ssm_conv_w, ssm_conv_b, ssm_dt_bias, ssm_a_log, ssm_d, ssm_norm,
               w_ssm_out, sc_conv_w, w_sc_out, w_o, norm_ffn, w_ffn_gate, w_ffn_up,
               ffn_conv_w, ffn_conv_b, w_ffn_down, norm_ple, w_ple, w_ple_gate)
    w = _prep_weights(*[t[0] for t in layer_w], norm_final)
    yp, *prompt_states = _layer(x_prompt, p_prompt[0], w, short=False)
    ys, *sample_states = _layer(
        x_sample, p_sample[0], w, short=True,
        states=(state_ssm[0], state_ssm_conv[0], state_short_conv[0], state_ffn_conv[0]))
    outs = [yp, ys]
    for sp, ss in zip(prompt_states, sample_states):
        outs += [sp[None], ss[None]]
    return tuple(outs)
```

```python
import functools

import numpy as np
import jax
import jax.numpy as jnp
from jax import lax
from jax.experimental import pallas as pl
from jax.experimental.pallas import tpu as pltpu

F32 = jnp.float32
BF16 = jnp.bfloat16

D_MODEL = 2048
D_INNER = 4096
SSM_HEADS = 64
SSM_HEAD_DIM = 64
SSM_GROUPS = 8
HEADS_PER_GROUP = SSM_HEADS // SSM_GROUPS
GROUP_WIDTH = D_INNER // SSM_GROUPS
SSM_STATE = 128
SSM_CHUNK = 128
SSM_CONV = 4
BC_DIM = 2 * SSM_GROUPS * SSM_STATE
XBC_DIM = D_INNER + BC_DIM
SC_DIM = D_MODEL
SC_CONV = 3
D_FF = 5632
FFN_CONV = 3
PLE_DIM = 256
EPS = 1e-6
DEC_SEQ = 8
PROJ_DIM = D_INNER + XBC_DIM + 5 * D_MODEL
DT_PAD = 128

SUBLANES = 8
BF16_ROWS = 16
LOG2E = 1.4426950408889634
VMEM_LIMIT = 52 * 1024 * 1024


def _cparams(sem):
    return pltpu.CompilerParams(dimension_semantics=sem, vmem_limit_bytes=VMEM_LIMIT)


def _dot(a, b):
    return jnp.dot(a, b, preferred_element_type=F32)


def _dot_nt(a, b):
    return lax.dot_general(a, b, (((1,), (1,)), ((), ())), preferred_element_type=F32)


def _sigmoid(x):
    return 0.5 * jnp.tanh(0.5 * x) + 0.5


def _silu(x):
    h = 0.5 * x
    return h * jnp.tanh(h) + h


def _softplus(x):
    return jnp.maximum(x, 0.0) + jnp.log(1.0 + jnp.exp(-jnp.abs(x)))


def _rms(x, g):
    ms = jnp.mean(x * x, axis=-1, keepdims=True)
    return x * lax.rsqrt(ms + EPS) * g


def _split_bf16(a, terms):
    parts = []
    r = a
    for _ in range(terms - 1):
        p = r.astype(BF16)
        parts.append(p)
        r = r - p.astype(F32)
    parts.append(r.astype(BF16))
    return parts


def _dot_sel_right(a, sel, terms):
    out = None
    for p in _split_bf16(a, terms):
        d = _dot(p, sel)
        out = d if out is None else out + d
    return out


def _dot_sel_left(sel, a, terms):
    out = None
    for p in _split_bf16(a, terms):
        d = _dot(sel, p)
        out = d if out is None else out + d
    return out


def _shift_long(x, halo, k):
    r = pltpu.roll(x, k, 0)
    h = pltpu.roll(halo, k, 0)
    row = lax.broadcasted_iota(jnp.int32, halo.shape, 0)
    first = jnp.where(row < k, h, r[:SUBLANES])
    return jnp.concatenate([first, r[SUBLANES:]], axis=0)


def _shift_short(x, past, k, taps):
    n = x.shape[0]
    r = pltpu.roll(x, k, 0)
    s = (n + k - (taps - 1)) % n
    p = past if s == 0 else pltpu.roll(past, s, 0)
    t = lax.broadcasted_iota(jnp.int32, x.shape, 0) & (DEC_SEQ - 1)
    return jnp.where(t < k, p, r)


def _causal_conv(x, w, shift):
    taps = w.shape[0]
    y = w[taps - 1:taps] * x
    for k in range(1, taps):
        y = y + w[taps - 1 - k:taps - k] * shift(k)
    return y


def _in_proj_kernel(x_ref, g_ref, wlo_ref, whi_ref, wdt_ref, proj_ref, dt_ref, n1_scr, *, n_lo,
                    sub):
    j = pl.program_id(1)

    @pl.when(j == 0)
    def _():
        n1 = _rms(x_ref[...], g_ref[...]).astype(BF16)
        n1_scr[...] = n1
        dt_ref[...] = _dot(n1, wdt_ref[...])

    @pl.when(j < n_lo)
    def _():
        n1 = n1_scr[...]
        for c in range(wlo_ref.shape[1] // sub):
            cs = slice(c * sub, (c + 1) * sub)
            proj_ref[:, cs] = _dot(n1, wlo_ref[:, cs].astype(BF16)).astype(BF16)

    @pl.when(j >= n_lo)
    def _():
        proj_ref[...] = _dot(n1_scr[...], whi_ref[...]).astype(BF16)


def _in_proj(x, g, w_in, w_hi, wdt, tm, tn):
    m = x.shape[0]
    n_lo = (D_INNER + XBC_DIM) // tn
    return pl.pallas_call(
        functools.partial(_in_proj_kernel, n_lo=n_lo, sub=256),
        grid=(m // tm, PROJ_DIM // tn),
        in_specs=[
            pl.BlockSpec((tm, D_MODEL), lambda i, j: (i, 0), pipeline_mode=pl.Buffered(1)),
            pl.BlockSpec((1, D_MODEL), lambda i, j: (0, 0)),
            pl.BlockSpec((D_MODEL, tn), lambda i, j: (0, jnp.minimum(j, n_lo - 1))),
            pl.BlockSpec((D_MODEL, tn), lambda i, j: (0, jnp.maximum(j - n_lo, 0))),
            pl.BlockSpec((D_MODEL, DT_PAD), lambda i, j: (0, 0)),
        ],
        out_specs=[
            pl.BlockSpec((tm, tn), lambda i, j: (i, j)),
            pl.BlockSpec((tm, DT_PAD), lambda i, j: (i, 0)),
        ],
        out_shape=[
            jax.ShapeDtypeStruct((m, PROJ_DIM), BF16),
            jax.ShapeDtypeStruct((m, DT_PAD), F32),
        ],
        scratch_shapes=[pltpu.VMEM((tm, D_MODEL), BF16)],
        compiler_params=_cparams(("parallel", "arbitrary")),
        name="in_proj",
    )(x, g, w_in, w_hi, wdt)


def _ssd_consts(q, short):
    l = np.arange(q)[:, None]
    s = np.arange(q)[None, :]
    if short:
        tri = (l // DEC_SEQ == s // DEC_SEQ) & (s <= l)
        end = s == (l // DEC_SEQ) * DEC_SEQ + DEC_SEQ - 1
    else:
        tri = s <= l
        end = np.broadcast_to(s == q - 1, (q, q))
    mask = np.where(tri, 0.0, -np.inf).astype(np.float32)
    h = np.arange(128)[None, :, None]
    g = np.arange(SSM_GROUPS)[:, None, None]
    c = np.arange(GROUP_WIDTH)[None, None, :]
    expand = h == g * HEADS_PER_GROUP + c // SSM_HEAD_DIM
    consts = [jnp.asarray(tri, BF16), jnp.asarray(end, BF16), jnp.asarray(mask),
              jnp.asarray(expand, BF16)]
    if not short:
        k = np.arange(1, SSM_CONV)[:, None, None]
        t = np.arange(q)[None, :, None]
        src = np.arange(q + BF16_ROWS)[None, None, :]
        shift = (src == BF16_ROWS + t - k).reshape((SSM_CONV - 1) * q, q + BF16_ROWS)
        consts.append(jnp.asarray(shift, BF16))
    return tuple(consts)


def _ssd_kernel(*refs, short, q, seqs):
    z_ref, xs_ref, bc_ref, dt_ref, tri_ref, end_ref, mask_ref, exp_ref = refs[:8]
    refs = refs[8:]
    if not short:
        shift_ref = refs[0]
        refs = refs[1:]
    cwx_ref, cbx_ref, cwb_ref, cbb_ref, dtb_ref, a_ref, dexp_ref, gn_ref = refs[:8]
    refs = refs[8:]
    past_xs_ref = past_bc_ref = tail_xs = tail_bc = None
    if short:
        past_xs_ref, past_bc_ref, h0_ref = refs[:3]
        refs = refs[3:]
    y_ref, ht_ref = refs[:2]
    refs = refs[2:]
    xs_scr, b_scr, c_scr, y_scr, ea_scr, we_scr, aend_scr, acumt_scr, arow_scr = refs[:9]
    refs = refs[9:]
    if short:
        eae_scr, xwt_scr = refs
    else:
        h_scr, tail_xs, tail_bc = refs

    step = pl.program_id(1)
    nstep = pl.num_programs(1)

    def conv_block(src_ref, past_ref, tail, cw, cb_ref, cs):
        xb = src_ref[:, cs]
        x = xb.astype(F32)
        if short:
            past = past_ref[:, cs]
            shift = lambda k: _shift_short(x, past, k, SSM_CONV)
        else:
            ext = jnp.concatenate([tail[:, cs], xb], axis=0)
            sh = _dot(shift_ref[...], ext)
            shift = lambda k: sh[(k - 1) * q:k * q]
            tail[:, cs] = xb[q - BF16_ROWS:]
        return _silu(cb_ref[:, cs] + _causal_conv(x, cw[:, cs], shift))

    def conv_prelude():
        cwx = cwx_ref[...]
        cwb = cwb_ref[...]
        for g in range(SSM_GROUPS):
            cs = slice(g * GROUP_WIDTH, (g + 1) * GROUP_WIDTH)
            xs_scr[g] = conv_block(xs_ref, past_xs_ref, tail_xs, cwx, cbx_ref, cs)
        per_blk = GROUP_WIDTH // SSM_STATE
        for blk in range(BC_DIM // GROUP_WIDTH):
            cs = slice(blk * GROUP_WIDTH, (blk + 1) * GROUP_WIDTH)
            v = conv_block(bc_ref, past_bc_ref, tail_bc, cwb, cbb_ref, cs)
            dst = b_scr if blk < SSM_GROUPS // per_blk else c_scr
            for r in range(per_blk):
                dst[(blk * per_blk + r) % SSM_GROUPS] = v[:, r * SSM_STATE:(r + 1) * SSM_STATE]

    def dt_prelude():
        dt = _softplus(dt_ref[...] + dtb_ref[...])
        acum = _dot_sel_left(tri_ref[...], dt * a_ref[...], 3)
        aend = _dot_sel_left(end_ref[...], acum, 3)
        acumt = acum.T
        acumt_scr[...] = acumt * LOG2E
        arow_scr[...] = (acumt - jnp.log(dt.T)) * LOG2E
        aend_scr[...] = aend
        ea_scr[...] = jnp.exp(acum)
        we_scr[...] = jnp.exp(aend - acum) * dt

    def diag_group(g):
        xs = xs_scr[g]
        cb = _dot_nt(c_scr[g].astype(BF16), b_scr[g].astype(BF16))
        mask = mask_ref[...]
        lane = lax.broadcasted_iota(jnp.int32, (q, 128), 1)
        ys = []
        for pair in range(HEADS_PER_GROUP // 2):
            ws = []
            for half in range(2):
                h = g * HEADS_PER_GROUP + 2 * pair + half
                colb = jnp.broadcast_to(acumt_scr[pl.ds(h, 1), :], (q, q)).T
                seg = colb - arow_scr[pl.ds(h, 1), :] + mask
                ws.append((cb * jnp.exp2(seg)).astype(BF16))
            xp = xs[:, pair * 128:(pair + 1) * 128]
            top = jnp.where(lane < SSM_HEAD_DIM, xp, 0.0).astype(BF16)
            bot = jnp.where(lane >= SSM_HEAD_DIM, xp, 0.0).astype(BF16)
            ys.append(_dot(jnp.concatenate(ws, axis=1), jnp.concatenate([top, bot], axis=0)))
        return jnp.concatenate(ys, axis=1), xs, exp_ref[g]

    def epilogue():
        for g in range(SSM_GROUPS):
            cs = slice(g * GROUP_WIDTH, (g + 1) * GROUP_WIDTH)
            y = y_scr[g] + dexp_ref[:, cs] * xs_scr[g]
            y = y * _silu(z_ref[:, cs].astype(F32))
            ms = jnp.mean(y * y, axis=-1, keepdims=True)
            y_ref[:, cs] = (y * lax.rsqrt(ms + EPS) * gn_ref[:, cs]).astype(BF16)

    if not short:
        @pl.when(step == 0)
        def _():
            h_scr[...] = jnp.zeros_like(h_scr)
            tail_xs[...] = jnp.zeros_like(tail_xs)
            tail_bc[...] = jnp.zeros_like(tail_bc)

        conv_prelude()
        dt_prelude()

        for g in range(SSM_GROUPS):
            y_diag, xs, expand = diag_group(g)
            hg = h_scr[g]
            y_off = _dot(c_scr[g].astype(BF16), hg.astype(BF16))
            y_scr[g] = y_diag + y_off * _dot_sel_right(ea_scr[...], expand, 2)
            xw = (xs * _dot_sel_right(we_scr[...], expand, 2)).astype(BF16)
            st = _dot(b_scr[g].T.astype(BF16), xw)
            dec = jnp.exp(_dot_sel_right(aend_scr[:SUBLANES, :], expand, 3))[:1]
            h_scr[g] = dec * hg + st

        epilogue()

        @pl.when(step == nstep - 1)
        def _():
            for g in range(SSM_GROUPS):
                ht_ref[0, g * GROUP_WIDTH:(g + 1) * GROUP_WIDTH, :] = h_scr[g].T
    else:
        @pl.when(step == 0)
        def _():
            conv_prelude()
            dt_prelude()
            for g in range(SSM_GROUPS):
                y_diag, xs, expand = diag_group(g)
                y_scr[g] = y_diag
                eae_scr[g] = _dot_sel_right(ea_scr[...], expand, 2)
                xwt_scr[g] = (xs * _dot_sel_right(we_scr[...], expand, 2)).T.astype(BF16)

        for u in range(seqs):
            sidx = step * seqs + u
            row = lax.broadcasted_iota(jnp.int32, (q, 128), 0)
            onehot = jnp.where(row == sidx * DEC_SEQ + DEC_SEQ - 1, 1.0, 0.0).astype(BF16)
            cdec = jnp.exp2(_dot_sel_right(acumt_scr[...], onehot, 3))
            in_seq = (row // DEC_SEQ) == sidx
            seq_rows = pl.ds(pl.multiple_of(sidx * DEC_SEQ, DEC_SEQ), DEC_SEQ)
            for g in range(SSM_GROUPS):
                rows = slice(g * GROUP_WIDTH, (g + 1) * GROUP_WIDTH)
                hg = h0_ref[u, rows, :]
                cj = c_scr[g, seq_rows, :].astype(BF16)
                y_off = _dot_nt(cj, hg.astype(BF16))
                y_scr[g, seq_rows, :] = y_scr[g, seq_rows, :] + y_off * eae_scr[g, seq_rows, :]
                bj = jnp.where(in_seq, b_scr[g], 0.0).astype(BF16)
                st = _dot(xwt_scr[g], bj)
                dec = [jnp.broadcast_to(cdec[g * HEADS_PER_GROUP + r:g * HEADS_PER_GROUP + r + 1],
                                        (SSM_HEAD_DIM, 128)) for r in range(HEADS_PER_GROUP)]
                ht_ref[u, rows, :] = jnp.concatenate(dec, axis=0) * hg + st

        @pl.when(step == nstep - 1)
        def _():
            epilogue()


def _ssd(proj, dt_raw, consts, wts, nseq, short, past=None, h0=None):
    m = proj.shape[0]
    q = SSM_CHUNK
    nblk = m // q
    if short:
        seqs = 2
        steps = q // DEC_SEQ // seqs
        grid = (nblk, steps)
        blk = lambda a, b: a
        st_idx = lambda a, b: (a * steps + b, 0, 0)
    else:
        seqs = 1
        steps = m // nseq // q
        grid = (nseq, steps)
        blk = lambda a, b: a * steps + b
        st_idx = lambda a, b: (a, 0, 0)
    const2 = lambda shape: pl.BlockSpec(shape, lambda a, b: (0, 0))
    const3 = lambda shape: pl.BlockSpec(shape, lambda a, b: (0, 0, 0))
    in_specs = [
        pl.BlockSpec((q, D_INNER), lambda a, b: (blk(a, b), 0)),
        pl.BlockSpec((q, D_INNER), lambda a, b: (blk(a, b), 1)),
        pl.BlockSpec((q, BC_DIM), lambda a, b: (blk(a, b), 2 * D_INNER // BC_DIM)),
        pl.BlockSpec((q, DT_PAD), lambda a, b: (blk(a, b), 0)),
        const2((q, q)), const2((q, q)), const2((q, q)),
        const3((SSM_GROUPS, 128, GROUP_WIDTH)),
    ]
    if not short:
        in_specs.append(const2(((SSM_CONV - 1) * q, q + BF16_ROWS)))
    in_specs += [
        const2((SSM_CONV, D_INNER)), const2((1, D_INNER)),
        const2((SSM_CONV, BC_DIM)), const2((1, BC_DIM)),
        const2((1, DT_PAD)), const2((1, DT_PAD)),
        const2((1, D_INNER)), const2((1, D_INNER)),
    ]
    args = [proj, proj, proj, dt_raw, *consts, *wts]
    state_spec = pl.BlockSpec((seqs, D_INNER, SSM_STATE), st_idx)
    scratch = [
        pltpu.VMEM((SSM_GROUPS, q, GROUP_WIDTH), F32),
        pltpu.VMEM((SSM_GROUPS, q, SSM_STATE), F32),
        pltpu.VMEM((SSM_GROUPS, q, SSM_STATE), F32),
        pltpu.VMEM((SSM_GROUPS, q, GROUP_WIDTH), F32),
        pltpu.VMEM((q, 128), F32), pltpu.VMEM((q, 128), F32), pltpu.VMEM((q, 128), F32),
        pltpu.VMEM((128, q), F32), pltpu.VMEM((128, q), F32),
    ]
    if short:
        in_specs += [
            pl.BlockSpec((q, D_INNER), lambda a, b: (a, 0)),
            pl.BlockSpec((q, BC_DIM), lambda a, b: (a, 0)),
            state_spec,
        ]
        args += [*past, h0]
        scratch += [pltpu.VMEM((SSM_GROUPS, q, GROUP_WIDTH), F32),
                    pltpu.VMEM((SSM_GROUPS, GROUP_WIDTH, q), BF16)]
    else:
        scratch += [pltpu.VMEM((SSM_GROUPS, SSM_STATE, GROUP_WIDTH), F32),
                    pltpu.VMEM((BF16_ROWS, D_INNER), BF16),
                    pltpu.VMEM((BF16_ROWS, BC_DIM), BF16)]
    return pl.pallas_call(
        functools.partial(_ssd_kernel, short=short, q=q, seqs=seqs),
        grid=grid,
        in_specs=in_specs,
        out_specs=[pl.BlockSpec((q, D_INNER), lambda a, b: (blk(a, b), 0)), state_spec],
        out_shape=[jax.ShapeDtypeStruct((m, D_INNER), BF16),
                   jax.ShapeDtypeStruct((nseq, D_INNER, SSM_STATE), F32)],
        scratch_shapes=scratch,
        compiler_params=_cparams(("arbitrary", "arbitrary")),
        name="ssd_short" if short else "ssd_long",
    )(*args)


def _mix_kernel(*refs, short, tiles_per_seq):
    yn_ref, scb_ref, scc_ref, sch_ref, ga_ref, gb_ref, cw_ref, wa_ref, wb_ref = refs[:9]
    refs = refs[9:]
    if short:
        past_ref = refs[0]
        refs = refs[1:]
    mixed_ref, utail_ref = refs[:2]
    tm = yn_ref.shape[0]

    u = scc_ref[...].astype(F32) * sch_ref[...].astype(F32)
    if short:
        shift = lambda k: _shift_short(u, past_ref[...], k, SC_CONV)
        utail_ref[...] = u
    else:
        halo_scr = refs[2]
        first = pl.program_id(0) % tiles_per_seq == 0
        halo = jnp.where(first, 0.0, halo_scr[...])
        shift = lambda k: _shift_long(u, halo, k)
        halo_scr[...] = u[tm - SUBLANES:]
        utail_ref[...] = u[tm - SUBLANES:]
    s = (scb_ref[...].astype(F32) * _causal_conv(u, cw_ref[...], shift)).astype(BF16)

    a = _dot(yn_ref[...], wa_ref[...])
    b = _dot(s, wb_ref[...])
    mixed = _sigmoid(ga_ref[...].astype(F32)) * a + _sigmoid(gb_ref[...].astype(F32)) * b
    mixed_ref[...] = mixed.astype(BF16)


def _mix(yn, proj, cw, wa, wb, seq_len, tm, short, past=None):
    m = yn.shape[0]
    col = lambda c: (D_INNER + XBC_DIM) // D_MODEL + c
    rows = lambda c: pl.BlockSpec((tm, D_MODEL), lambda i: (i, col(c)))
    resident = lambda shape: pl.BlockSpec(shape, lambda i: (0, 0), pipeline_mode=pl.Buffered(1))
    in_specs = [
        pl.BlockSpec((tm, D_INNER), lambda i: (i, 0)),
        rows(0), rows(1), rows(2), rows(3), rows(4),
        resident((SC_CONV, SC_DIM)),
        resident((D_INNER, D_MODEL)),
        resident((SC_DIM, D_MODEL)),
    ]
    args = [yn, proj, proj, proj, proj, proj, cw, wa, wb]
    scratch = []
    if short:
        in_specs.append(pl.BlockSpec((tm, SC_DIM), lambda i: (i, 0)))
        args.append(past)
        tail_rows = tm
    else:
        scratch.append(pltpu.VMEM((SUBLANES, SC_DIM), F32))
        tail_rows = SUBLANES
    return pl.pallas_call(
        functools.partial(_mix_kernel, short=short, tiles_per_seq=max(seq_len // tm, 1)),
        grid=(m // tm,),
        in_specs=in_specs,
        out_specs=[pl.BlockSpec((tm, D_MODEL), lambda i: (i, 0)),
                   pl.BlockSpec((tail_rows, SC_DIM), lambda i: (i, 0))],
        out_shape=[jax.ShapeDtypeStruct((m, D_MODEL), BF16),
                   jax.ShapeDtypeStruct((m // tm * tail_rows, SC_DIM), F32)],
        scratch_shapes=scratch,
        compiler_params=_cparams(("arbitrary",)),
        name="mix",
    )(*args)


def _row_halves(ref):
    half = ref.shape[0] // 2
    return [slice(0, half), slice(half, 2 * half)]


def _out_proj_kernel(mixed_ref, x_ref, wo_ref, g_ref, x1_ref, n2_ref):
    for rs in _row_halves(x_ref):
        x1 = x_ref[rs, :] + _dot(mixed_ref[rs, :], wo_ref[...])
        x1_ref[rs, :] = x1
        n2_ref[rs, :] = _rms(x1, g_ref[...]).astype(BF16)


def _out_proj(mixed, x, wo, g, tm):
    m = x.shape[0]
    row = lambda i: (i, 0)
    const = lambda i: (0, 0)
    return pl.pallas_call(
        _out_proj_kernel,
        grid=(m // tm,),
        in_specs=[pl.BlockSpec((tm, D_MODEL), row), pl.BlockSpec((tm, D_MODEL), row),
                  pl.BlockSpec((D_MODEL, D_MODEL), const, pipeline_mode=pl.Buffered(1)),
                  pl.BlockSpec((1, D_MODEL), const)],
        out_specs=[pl.BlockSpec((tm, D_MODEL), row), pl.BlockSpec((tm, D_MODEL), row)],
        out_shape=[jax.ShapeDtypeStruct((m, D_MODEL), F32),
                   jax.ShapeDtypeStruct((m, D_MODEL), BF16)],
        compiler_params=_cparams(("parallel",)),
        name="out_proj",
    )(mixed, x, wo, g)


def _ffn_up_kernel(*refs, short, tiles_per_seq, splits):
    n2_ref, wg_ref, wu_ref, cw_ref, cb_ref = refs[:5]
    refs = refs[5:]
    if short:
        past_ref = refs[0]
        refs = refs[1:]
    act_ref, gtail_ref = refs[:2]
    tm = n2_ref.shape[0]
    n2 = n2_ref[...]
    j = pl.program_id(1)
    first = pl.program_id(0) % tiles_per_seq == 0
    lo = 0
    for width in splits:
        cs = slice(lo, lo + width)
        lo += width
        gate = _dot(n2, wg_ref[:, cs])
        up = _dot(n2, wu_ref[:, cs])
        if short:
            past = past_ref[:, cs]
            shift = lambda k, gate=gate, past=past: _shift_short(gate, past, k, FFN_CONV)
            gtail_ref[:, cs] = gate
        else:
            halo_scr = refs[2]
            halo = jnp.where(first, 0.0, halo_scr[j, :, cs])
            shift = lambda k, gate=gate, halo=halo: _shift_long(gate, halo, k)
            halo_scr[j, :, cs] = gate[tm - SUBLANES:]
            gtail_ref[:, cs] = gate[tm - SUBLANES:]
        conv = cb_ref[:, cs] + _causal_conv(gate, cw_ref[:, cs], shift)
        act_ref[:, cs] = (_silu(conv) * up).astype(BF16)


def _ffn_up(n2, wg, wu, cw, cb, seq_len, tm, tn, splits, short, past=None):
    m = n2.shape[0]
    nj = D_FF // tn
    assert sum(splits) == tn
    in_specs = [
        pl.BlockSpec((tm, D_MODEL), lambda i, j: (i, 0)),
        pl.BlockSpec((D_MODEL, tn), lambda i, j: (0, j)),
        pl.BlockSpec((D_MODEL, tn), lambda i, j: (0, j)),
        pl.BlockSpec((FFN_CONV, tn), lambda i, j: (0, j)),
        pl.BlockSpec((1, tn), lambda i, j: (0, j)),
    ]
    args = [n2, wg, wu, cw, cb]
    scratch = []
    if short:
        in_specs.append(pl.BlockSpec((tm, tn), lambda i, j: (i, j)))
        args.append(past)
        tail_rows = tm
    else:
        scratch.append(pltpu.VMEM((nj, SUBLANES, tn), F32))
        tail_rows = SUBLANES
    return pl.pallas_call(
        functools.partial(_ffn_up_kernel, short=short, tiles_per_seq=max(seq_len // tm, 1),
                          splits=splits),
        grid=(m // tm, nj),
        in_specs=in_specs,
        out_specs=[pl.BlockSpec((tm, tn), lambda i, j: (i, j)),
                   pl.BlockSpec((tail_rows, tn), lambda i, j: (i, j))],
        out_shape=[jax.ShapeDtypeStruct((m, D_FF), BF16),
                   jax.ShapeDtypeStruct((m // tm * tail_rows, D_FF), F32)],
        scratch_shapes=scratch,
        compiler_params=_cparams(("arbitrary", "arbitrary")),
        name="ffn_up",
    )(*args)


def _ffn_down_kernel(act_ref, wd_ref, x1_ref, x2_ref):
    x2_ref[...] = x1_ref[...] + _dot(act_ref[...], wd_ref[...])


def _ffn_down(act, wd, x1, tm):
    m = x1.shape[0]
    return pl.pallas_call(
        _ffn_down_kernel,
        grid=(m // tm,),
        in_specs=[pl.BlockSpec((tm, D_FF), lambda i: (i, 0)),
                  pl.BlockSpec((D_FF, D_MODEL), lambda i: (0, 0), pipeline_mode=pl.Buffered(1)),
                  pl.BlockSpec((tm, D_MODEL), lambda i: (i, 0))],
        out_specs=pl.BlockSpec((tm, D_MODEL), lambda i: (i, 0)),
        out_shape=jax.ShapeDtypeStruct((m, D_MODEL), F32),
        compiler_params=_cparams(("parallel",)),
        name="ffn_down",
    )(act, wd, x1)


def _ple_kernel(x2_ref, pe_ref, wple_ref, wgate_ref, gp_ref, gf_ref, y_ref):
    for rs in _row_halves(x2_ref):
        x2 = x2_ref[rs, :]
        n3 = _rms(x2, gp_ref[...]).astype(BF16)
        gate = _sigmoid(_dot(n3, wgate_ref[...]))
        emb = _dot(pe_ref[rs, :].astype(BF16), wple_ref[...])
        y_ref[rs, :] = _rms(x2 + emb * gate, gf_ref[...])


def _ple(x2, pe, wple, wgate, gp, gf, tm):
    m = x2.shape[0]
    row = lambda i: (i, 0)
    const = lambda i: (0, 0)
    return pl.pallas_call(
        _ple_kernel,
        grid=(m // tm,),
        in_specs=[pl.BlockSpec((tm, D_MODEL), row), pl.BlockSpec((tm, PLE_DIM), row),
                  pl.BlockSpec((PLE_DIM, D_MODEL), const),
                  pl.BlockSpec((D_MODEL, D_MODEL), const, pipeline_mode=pl.Buffered(1)),
                  pl.BlockSpec((1, D_MODEL), const), pl.BlockSpec((1, D_MODEL), const)],
        out_specs=pl.BlockSpec((tm, D_MODEL), row),
        out_shape=jax.ShapeDtypeStruct((m, D_MODEL), F32),
        compiler_params=_cparams(("parallel",)),
        name="ple",
    )(x2, pe, wple, wgate, gp, gf)


def _pad_steps(state):
    s, k, c = state.shape
    return jnp.pad(state, ((0, 0), (0, DEC_SEQ - k), (0, 0))).reshape(s * DEC_SEQ, c)


def _tile(m, want):
    return min(m, want)


def _layer(x, pe, w, short, states=None):
    nseq, seq_len, _ = x.shape
    m = nseq * seq_len
    x2d = x.reshape(m, D_MODEL)
    pe2d = pe.reshape(m, PLE_DIM)
    if short:
        ssm0, xbc0, sc0, ffn0 = states
        past_xs = _pad_steps(xbc0[:, :, :D_INNER])
        past_bc = _pad_steps(xbc0[:, :, D_INNER:])
        past_sc = _pad_steps(sc0)
        past_ffn = _pad_steps(ffn0)
        h0 = ssm0.reshape(nseq, D_INNER, SSM_STATE)

    proj, dt_raw = _in_proj(x2d, w["norm_mix"], w["w_in"], w["w_in_hi"], w["w_dt"],
                            _tile(m, 1024), 1024)

    ssd_w = (w["cw_xs"], w["cb_xs"], w["cw_bc"], w["cb_bc"], w["dt_bias"], w["a_neg"],
             w["d_exp"], w["ssm_norm"])
    consts = _ssd_consts(SSM_CHUNK, short)
    if short:
        yn, ssm_new = _ssd(proj, dt_raw, consts, ssd_w, nseq, True, (past_xs, past_bc), h0)
    else:
        yn, ssm_new = _ssd(proj, dt_raw, consts, ssd_w, nseq, False)

    tm = _tile(m, 512)
    tm_mix = _tile(m, 128 if short else 256)
    mixed, u_tail = _mix(yn, proj, w["sc_conv_w"], w["w_ssm_out"], w["w_sc_out"], seq_len,
                         tm_mix, short, past_sc if short else None)
    x1, n2 = _out_proj(mixed, x2d, w["w_o"], w["norm_ffn"], tm)

    tm_ffn = _tile(m, 1024)
    tn_ffn, splits = (512, (256, 256)) if short else (1408, (512, 512, 384))
    act, g_tail = _ffn_up(n2, w["w_ffn_gate"], w["w_ffn_up"], w["ffn_conv_w"], w["ffn_conv_b"],
                          seq_len, tm_ffn, tn_ffn, splits, short, past_ffn if short else None)
    x2 = _ffn_down(act, w["w_ffn_down"], x1, _tile(m, 256))
    y = _ple(x2, pe2d, w["w_ple"], w["w_ple_gate"], w["norm_ple"], w["norm_final"], tm)

    proj3 = proj.reshape(nseq, seq_len, PROJ_DIM)
    xbc_new = proj3[:, seq_len - (SSM_CONV - 1):, D_INNER:D_INNER + XBC_DIM].astype(F32)

    def last_rows(tail, tile_rows, taps, width):
        if short:
            t = tail.reshape(nseq, seq_len, width)
        else:
            per_seq = seq_len // tile_rows
            t = tail.reshape(nseq, per_seq, SUBLANES, width)[:, per_seq - 1]
        return t[:, t.shape[1] - (taps - 1):]

    sc_new = last_rows(u_tail, tm_mix, SC_CONV, SC_DIM)
    ffn_new = last_rows(g_tail, tm_ffn, FFN_CONV, D_FF)
    ssm_new = ssm_new.reshape(nseq, SSM_HEADS, SSM_HEAD_DIM, SSM_STATE)
    return y.reshape(nseq, seq_len, D_MODEL), ssm_new, xbc_new, sc_new, ffn_new


def _prep_weights(norm_mix, w_in, ssm_conv_w, ssm_conv_b, ssm_dt_bias, ssm_a_log, ssm_d, ssm_norm,
                  w_ssm_out, sc_conv_w, w_sc_out, w_o, norm_ffn, w_ffn_gate, w_ffn_up,
                  ffn_conv_w, ffn_conv_b, w_ffn_down, norm_ple, w_ple, w_ple_gate, norm_final):
    dt_lo = D_INNER + XBC_DIM
    dt_hi = dt_lo + SSM_HEADS
    row = lambda v: v.reshape(1, -1).astype(F32)
    pad_heads = lambda v: jnp.pad(v.astype(F32), (0, DT_PAD - SSM_HEADS)).reshape(1, DT_PAD)
    return {
        "norm_mix": row(norm_mix),
        "w_in": w_in,
        "w_in_hi": w_in[:, dt_hi:].astype(BF16),
        "w_dt": jnp.pad(w_in[:, dt_lo:dt_hi], ((0, 0), (0, DT_PAD - SSM_HEADS))).astype(BF16),
        "cw_xs": ssm_conv_w[:, :D_INNER].astype(F32),
        "cb_xs": row(ssm_conv_b[:D_INNER]),
        "cw_bc": ssm_conv_w[:, D_INNER:].astype(F32),
        "cb_bc": row(ssm_conv_b[D_INNER:]),
        "dt_bias": pad_heads(ssm_dt_bias),
        "a_neg": pad_heads(-jnp.exp(ssm_a_log.astype(F32))),
        "d_exp": row(jnp.repeat(ssm_d, SSM_HEAD_DIM)),
        "ssm_norm": row(ssm_norm),
        "w_ssm_out": w_ssm_out.astype(BF16),
        "sc_conv_w": sc_conv_w.astype(F32),
        "w_sc_out": w_sc_out.astype(BF16),
        "w_o": w_o.astype(BF16),
        "norm_ffn": row(norm_ffn),
        "w_ffn_gate": w_ffn_gate.astype(BF16),
        "w_ffn_up": w_ffn_up.astype(BF16),
        "ffn_conv_w": ffn_conv_w.astype(F32),
        "ffn_conv_b": row(ffn_conv_b),
        "w_ffn_down": w_ffn_down.astype(BF16),
        "norm_ple": row(norm_ple),
        "w_ple": w_ple.astype(BF16),
        "w_ple_gate": w_ple_gate.astype(BF16),
        "norm_final": row(norm_final),
    }


def kernel(x_prompt, x_sample, state_ssm, state_ssm_conv, state_short_conv, state_ffn_conv, p_prompt, p_sample, norm_mix, w_in, ssm_conv_w, ssm_conv_b, ssm_dt_bias, ssm_a_log, ssm_d, ssm_norm, w_ssm_out, sc_conv_w, w_sc_out, w_o, norm_ffn, w_ffn_gate, w_ffn_up, ffn_conv_w, ffn_conv_b, w_ffn_down, norm_ple, w_ple, w_ple_gate, norm_final):
    depth = w_in.shape[0]
    assert depth == 1, "the final norm is fused into the layer's last kernel"
    assert x_sample.shape[1] == DEC_SEQ
    layer_w = (norm_mix, w_in, ssm_conv_w, ssm_conv_b, ssm_dt_bias, ssm_a_log, ssm_d, ssm_norm,
               w_ssm_out, sc_conv_w, w_sc_out, w_o, norm_ffn, w_ffn_gate, w_ffn_up,
               ffn_conv_w, ffn_conv_b, w_ffn_down, norm_ple, w_ple, w_ple_gate)
    w = _prep_weights(*[t[0] for t in layer_w], norm_final)
    yp, *prompt_states = _layer(x_prompt, p_prompt[0], w, short=False)
    ys, *sample_states = _layer(
        x_sample, p_sample[0], w, short=True,
        states=(state_ssm[0], state_ssm_conv[0], state_short_conv[0], state_ffn_conv[0]))
    outs = [yp, ys]
    for sp, ss in zip(prompt_states, sample_states):
        outs += [sp[None], ss[None]]
    return tuple(outs)
```

```python
import functools

import numpy as np
import jax
import jax.numpy as jnp
from jax import lax
from jax.experimental import pallas as pl
from jax.experimental.pallas import tpu as pltpu

F32 = jnp.float32
BF16 = jnp.bfloat16

D_MODEL = 2048
D_INNER = 4096
SSM_HEADS = 64
SSM_HEAD_DIM = 64
SSM_GROUPS = 8
HEADS_PER_GROUP = SSM_HEADS // SSM_GROUPS
GROUP_WIDTH = D_INNER // SSM_GROUPS
SSM_STATE = 128
SSM_CHUNK = 128
SSM_CONV = 4
BC_DIM = 2 * SSM_GROUPS * SSM_STATE
XBC_DIM = D_INNER + BC_DIM
SC_DIM = D_MODEL
SC_CONV = 3
D_FF = 5632
FFN_CONV = 3
PLE_DIM = 256
EPS = 1e-6
DEC_SEQ = 8
PROJ_DIM = D_INNER + XBC_DIM + 5 * D_MODEL
DT_PAD = 128

SUBLANES = 8
LOG2E = 1.4426950408889634
VMEM_LIMIT = 52 * 1024 * 1024


def _cparams(sem):
    return pltpu.CompilerParams(dimension_semantics=sem, vmem_limit_bytes=VMEM_LIMIT)


def _dot(a, b):
    return jnp.dot(a, b, preferred_element_type=F32)


def _dot_nt(a, b):
    return lax.dot_general(a, b, (((1,), (1,)), ((), ())), preferred_element_type=F32)


def _sigmoid(x):
    return 0.5 * jnp.tanh(0.5 * x) + 0.5


def _silu(x):
    h = 0.5 * x
    return h * jnp.tanh(h) + h


def _softplus(x):
    return jnp.maximum(x, 0.0) + jnp.log(1.0 + jnp.exp(-jnp.abs(x)))


def _rms(x, g):
    ms = jnp.mean(x * x, axis=-1, keepdims=True)
    return x * lax.rsqrt(ms + EPS) * g


def _split_bf16(a, terms):
    parts = []
    r = a
    for _ in range(terms - 1):
        p = r.astype(BF16)
        parts.append(p)
        r = r - p.astype(F32)
    parts.append(r.astype(BF16))
    return parts


def _dot_sel_right(a, sel, terms):
    out = None
    for p in _split_bf16(a, terms):
        d = _dot(p, sel)
        out = d if out is None else out + d
    return out


def _dot_sel_left(sel, a, terms):
    out = None
    for p in _split_bf16(a, terms):
        d = _dot(sel, p)
        out = d if out is None else out + d
    return out


def _shift_long(x, halo, k):
    r = pltpu.roll(x, k, 0)
    h = pltpu.roll(halo, k, 0)
    row = lax.broadcasted_iota(jnp.int32, halo.shape, 0)
    first = jnp.where(row < k, h, r[:SUBLANES])
    return jnp.concatenate([first, r[SUBLANES:]], axis=0)


def _shift_short(x, past, k, taps):
    n = x.shape[0]
    r = pltpu.roll(x, k, 0)
    s = (n + k - (taps - 1)) % n
    p = past if s == 0 else pltpu.roll(past, s, 0)
    t = lax.broadcasted_iota(jnp.int32, x.shape, 0) & (DEC_SEQ - 1)
    return jnp.where(t < k, p, r)


def _causal_conv(x, w, shift):
    taps = w.shape[0]
    y = w[taps - 1:taps] * x
    for k in range(1, taps):
        y = y + w[taps - 1 - k:taps - k] * shift(k)
    return y


def _repack_kernel(main_ref, next_ref, out_ref):
    w = jnp.concatenate([main_ref[...], next_ref[...]], axis=1)
    width = w.shape[1]
    out_ref[...] = pltpu.roll(w, width - SSM_HEADS, 1)[:, :out_ref.shape[1]].astype(BF16)


def _repack_hi(w_in, tn):
    lo = D_INNER + XBC_DIM
    n_hi = PROJ_DIM - lo
    return pl.pallas_call(
        _repack_kernel,
        grid=(n_hi // tn,),
        in_specs=[pl.BlockSpec((D_MODEL, tn), lambda j: (0, lo // tn + j)),
                  pl.BlockSpec((D_MODEL, 128), lambda j: (0, (lo + (j + 1) * tn) // 128))],
        out_specs=pl.BlockSpec((D_MODEL, tn), lambda j: (0, j)),
        out_shape=jax.ShapeDtypeStruct((D_MODEL, n_hi), BF16),
        compiler_params=_cparams(("parallel",)),
        name="repack_w_in",
    )(w_in, w_in)


def _in_proj_kernel(x_ref, g_ref, wlo_ref, whi_ref, wdt_ref, proj_ref, dt_ref, n1_scr, *, n_lo,
                    sub):
    j = pl.program_id(1)

    @pl.when(j == 0)
    def _():
        n1 = _rms(x_ref[...], g_ref[...]).astype(BF16)
        n1_scr[...] = n1
        dt_ref[...] = _dot(n1, wdt_ref[...].astype(BF16))

    @pl.when(j < n_lo)
    def _():
        n1 = n1_scr[...]
        for c in range(wlo_ref.shape[1] // sub):
            cs = slice(c * sub, (c + 1) * sub)
            proj_ref[:, cs] = _dot(n1, wlo_ref[:, cs].astype(BF16)).astype(BF16)

    @pl.when(j >= n_lo)
    def _():
        proj_ref[...] = _dot(n1_scr[...], whi_ref[...]).astype(BF16)


def _in_proj(x, g, w_in, w_hi, tm, tn):
    m = x.shape[0]
    n_lo = (D_INNER + XBC_DIM) // tn
    return pl.pallas_call(
        functools.partial(_in_proj_kernel, n_lo=n_lo, sub=256),
        grid=(m // tm, PROJ_DIM // tn),
        in_specs=[
            pl.BlockSpec((tm, D_MODEL), lambda i, j: (i, 0), pipeline_mode=pl.Buffered(1)),
            pl.BlockSpec((1, D_MODEL), lambda i, j: (0, 0)),
            pl.BlockSpec((D_MODEL, tn), lambda i, j: (0, jnp.minimum(j, n_lo - 1))),
            pl.BlockSpec((D_MODEL, tn), lambda i, j: (0, jnp.maximum(j - n_lo, 0))),
            pl.BlockSpec((D_MODEL, DT_PAD), lambda i, j: (0, (D_INNER + XBC_DIM) // DT_PAD)),
        ],
        out_specs=[
            pl.BlockSpec((tm, tn), lambda i, j: (i, j)),
            pl.BlockSpec((tm, DT_PAD), lambda i, j: (i, 0)),
        ],
        out_shape=[
            jax.ShapeDtypeStruct((m, PROJ_DIM), BF16),
            jax.ShapeDtypeStruct((m, DT_PAD), F32),
        ],
        scratch_shapes=[pltpu.VMEM((tm, D_MODEL), BF16)],
        compiler_params=_cparams(("parallel", "arbitrary")),
        name="in_proj",
    )(x, g, w_in, w_hi, w_in)


def _ssd_consts(q, short):
    l = np.arange(q)[:, None]
    s = np.arange(q)[None, :]
    if short:
        tri = (l // DEC_SEQ == s // DEC_SEQ) & (s <= l)
        end = s == (l // DEC_SEQ) * DEC_SEQ + DEC_SEQ - 1
    else:
        tri = s <= l
        end = np.broadcast_to(s == q - 1, (q, q))
    mask = np.where(tri, 0.0, -np.inf).astype(np.float32)
    h = np.arange(128)[None, :, None]
    g = np.arange(SSM_GROUPS)[:, None, None]
    c = np.arange(GROUP_WIDTH)[None, None, :]
    expand = h == g * HEADS_PER_GROUP + c // SSM_HEAD_DIM
    return (jnp.asarray(tri, BF16), jnp.asarray(end, BF16), jnp.asarray(mask),
            jnp.asarray(expand, BF16))


def _ssd_kernel(*refs, short, q, seqs):
    z_ref, xs_ref, bc_ref, dt_ref, tri_ref, end_ref, mask_ref, exp_ref = refs[:8]
    refs = refs[8:]
    cwx_ref, cbx_ref, cwb_ref, cbb_ref, dtb_ref, a_ref, dexp_ref, gn_ref = refs[:8]
    refs = refs[8:]
    past_xs_ref = past_bc_ref = tail_xs = tail_bc = None
    if short:
        past_xs_ref, past_bc_ref, h0_ref = refs[:3]
        refs = refs[3:]
    y_ref, ht_ref = refs[:2]
    refs = refs[2:]
    xs_scr, b_scr, c_scr, y_scr, ea_scr, we_scr, aend_scr, acumt_scr, arow_scr = refs[:9]
    refs = refs[9:]
    if short:
        eae_scr, xwt_scr = refs
    else:
        h_scr, tail_xs, tail_bc = refs

    step = pl.program_id(1)
    nstep = pl.num_programs(1)

    def conv_block(src_ref, past_ref, tail, cw, cb_ref, cs):
        xb = src_ref[:, cs]
        x = xb.astype(F32)
        if short:
            past = past_ref[:, cs]
            shift = lambda k: _shift_short(x, past, k, SSM_CONV)
        else:
            halo = tail[:, cs]
            shift = lambda k: _shift_long(x, halo, k)
            tail[:, cs] = x[q - SUBLANES:]
        return _silu(cb_ref[:, cs] + _causal_conv(x, cw[:, cs], shift))

    def conv_group(g):
        cs = slice(g * GROUP_WIDTH, (g + 1) * GROUP_WIDTH)
        bs = slice(g * SSM_STATE, (g + 1) * SSM_STATE)
        cc = slice((SSM_GROUPS + g) * SSM_STATE, (SSM_GROUPS + g + 1) * SSM_STATE)
        cwx = cwx_ref[...]
        cwb = cwb_ref[...]
        return (conv_block(xs_ref, past_xs_ref, tail_xs, cwx, cbx_ref, cs),
                conv_block(bc_ref, past_bc_ref, tail_bc, cwb, cbb_ref, bs),
                conv_block(bc_ref, past_bc_ref, tail_bc, cwb, cbb_ref, cc))

    def dt_prelude():
        dt = _softplus(dt_ref[...] + dtb_ref[...])
        acum = _dot_sel_left(tri_ref[...], dt * a_ref[...], 3)
        aend = _dot_sel_left(end_ref[...], acum, 3)
        acumt = acum.T
        acumt_scr[...] = acumt * LOG2E
        arow_scr[...] = (acumt - jnp.log(dt.T)) * LOG2E
        aend_scr[...] = aend
        ea_scr[...] = jnp.exp(acum)
        we_scr[...] = jnp.exp(aend - acum) * dt

    def diag_group(g, xs, bg, cg):
        cb = _dot_nt(cg.astype(BF16), bg.astype(BF16))
        mask = mask_ref[...]
        lane = lax.broadcasted_iota(jnp.int32, (q, 128), 1)
        ys = []
        for pair in range(HEADS_PER_GROUP // 2):
            ws = []
            for half in range(2):
                h = g * HEADS_PER_GROUP + 2 * pair + half
                colb = jnp.broadcast_to(acumt_scr[pl.ds(h, 1), :], (q, q)).T
                seg = colb - arow_scr[pl.ds(h, 1), :] + mask
                ws.append((cb * jnp.exp2(seg)).astype(BF16))
            xp = xs[:, pair * 128:(pair + 1) * 128]
            top = jnp.where(lane < SSM_HEAD_DIM, xp, 0.0).astype(BF16)
            bot = jnp.where(lane >= SSM_HEAD_DIM, xp, 0.0).astype(BF16)
            ys.append(_dot(jnp.concatenate(ws, axis=1), jnp.concatenate([top, bot], axis=0)))
        return jnp.concatenate(ys, axis=1)

    def finish_group(g, y, xs):
        cs = slice(g * GROUP_WIDTH, (g + 1) * GROUP_WIDTH)
        y = y + dexp_ref[:, cs] * xs
        y = y * _silu(z_ref[:, cs].astype(F32))
        ms = jnp.mean(y * y, axis=-1, keepdims=True)
        y_ref[:, cs] = (y * lax.rsqrt(ms + EPS) * gn_ref[:, cs]).astype(BF16)

    if not short:
        @pl.when(step == 0)
        def _():
            h_scr[...] = jnp.zeros_like(h_scr)
            tail_xs[...] = jnp.zeros_like(tail_xs)
            tail_bc[...] = jnp.zeros_like(tail_bc)

        dt_prelude()
        for g in range(SSM_GROUPS):
            xs_scr[g], b_scr[g], c_scr[g] = conv_group(g)
        for g in range(SSM_GROUPS):
            xs, bg, cg = xs_scr[g], b_scr[g], c_scr[g]
            expand = exp_ref[g]
            y_diag = diag_group(g, xs, bg, cg)
            hg = h_scr[g]
            y_off = _dot(cg.astype(BF16), hg.astype(BF16))
            y_scr[g] = y_diag + y_off * _dot_sel_right(ea_scr[...], expand, 2)
            xw = (xs * _dot_sel_right(we_scr[...], expand, 2)).astype(BF16)
            st = _dot(bg.T.astype(BF16), xw)
            dec = jnp.exp(_dot_sel_right(aend_scr[:SUBLANES, :], expand, 3))[:1]
            h_scr[g] = dec * hg + st
        for g in range(SSM_GROUPS):
            finish_group(g, y_scr[g], xs_scr[g])

        @pl.when(step == nstep - 1)
        def _():
            for g in range(SSM_GROUPS):
                ht_ref[0, g * GROUP_WIDTH:(g + 1) * GROUP_WIDTH, :] = h_scr[g].T
    else:
        @pl.when(step == 0)
        def _():
            dt_prelude()
            for g in range(SSM_GROUPS):
                xs, bg, cg = conv_group(g)
                expand = exp_ref[g]
                xs_scr[g] = xs
                b_scr[g] = bg
                c_scr[g] = cg
                y_scr[g] = diag_group(g, xs, bg, cg)
                eae_scr[g] = _dot_sel_right(ea_scr[...], expand, 2)
                xwt_scr[g] = (xs * _dot_sel_right(we_scr[...], expand, 2)).T.astype(BF16)

        for u in range(seqs):
            sidx = step * seqs + u
            row = lax.broadcasted_iota(jnp.int32, (q, 128), 0)
            onehot = jnp.where(row == sidx * DEC_SEQ + DEC_SEQ - 1, 1.0, 0.0).astype(BF16)
            cdec = jnp.exp2(_dot_sel_right(acumt_scr[...], onehot, 3))
            in_seq = (row // DEC_SEQ) == sidx
            seq_rows = pl.ds(pl.multiple_of(sidx * DEC_SEQ, DEC_SEQ), DEC_SEQ)
            for g in range(SSM_GROUPS):
                rows = slice(g * GROUP_WIDTH, (g + 1) * GROUP_WIDTH)
                hg = h0_ref[u, rows, :]
                cj = c_scr[g, seq_rows, :].astype(BF16)
                y_off = _dot_nt(cj, hg.astype(BF16))
                y_scr[g, seq_rows, :] = y_scr[g, seq_rows, :] + y_off * eae_scr[g, seq_rows, :]
                bj = jnp.where(in_seq, b_scr[g], 0.0).astype(BF16)
                st = _dot(xwt_scr[g], bj)
                dec = [jnp.broadcast_to(cdec[g * HEADS_PER_GROUP + r:g * HEADS_PER_GROUP + r + 1],
                                        (SSM_HEAD_DIM, 128)) for r in range(HEADS_PER_GROUP)]
                ht_ref[u, rows, :] = jnp.concatenate(dec, axis=0) * hg + st

        @pl.when(step == nstep - 1)
        def _():
            for g in range(SSM_GROUPS):
                finish_group(g, y_scr[g], xs_scr[g])


def _ssd(proj, dt_raw, consts, wts, nseq, short, past=None, h0=None):
    m = proj.shape[0]
    q = SSM_CHUNK
    nblk = m // q
    if short:
        seqs = 2
        steps = q // DEC_SEQ // seqs
        grid = (nblk, steps)
        blk = lambda a, b: a
        st_idx = lambda a, b: (a * steps + b, 0, 0)
    else:
        seqs = 1
        steps = m // nseq // q
        grid = (nseq, steps)
        blk = lambda a, b: a * steps + b
        st_idx = lambda a, b: (a, 0, 0)
    const2 = lambda shape: pl.BlockSpec(shape, lambda a, b: (0, 0))
    const3 = lambda shape: pl.BlockSpec(shape, lambda a, b: (0, 0, 0))
    in_specs = [
        pl.BlockSpec((q, D_INNER), lambda a, b: (blk(a, b), 0)),
        pl.BlockSpec((q, D_INNER), lambda a, b: (blk(a, b), 1)),
        pl.BlockSpec((q, BC_DIM), lambda a, b: (blk(a, b), 2 * D_INNER // BC_DIM)),
        pl.BlockSpec((q, DT_PAD), lambda a, b: (blk(a, b), 0)),
        const2((q, q)), const2((q, q)), const2((q, q)),
        const3((SSM_GROUPS, 128, GROUP_WIDTH)),
        const2((SSM_CONV, D_INNER)), const2((1, D_INNER)),
        const2((SSM_CONV, BC_DIM)), const2((1, BC_DIM)),
        const2((1, DT_PAD)), const2((1, DT_PAD)),
        const2((1, D_INNER)), const2((1, D_INNER)),
    ]
    args = [proj, proj, proj, dt_raw, *consts, *wts]
    state_spec = pl.BlockSpec((seqs, D_INNER, SSM_STATE), st_idx)
    scratch = [
        pltpu.VMEM((SSM_GROUPS, q, GROUP_WIDTH), F32),
        pltpu.VMEM((SSM_GROUPS, q, SSM_STATE), F32),
        pltpu.VMEM((SSM_GROUPS, q, SSM_STATE), F32),
        pltpu.VMEM((SSM_GROUPS, q, GROUP_WIDTH), F32),
        pltpu.VMEM((q, 128), F32), pltpu.VMEM((q, 128), F32), pltpu.VMEM((q, 128), F32),
        pltpu.VMEM((128, q), F32), pltpu.VMEM((128, q), F32),
    ]
    if short:
        in_specs += [
            pl.BlockSpec((q, D_INNER), lambda a, b: (a, 0)),
            pl.BlockSpec((q, BC_DIM), lambda a, b: (a, 0)),
            state_spec,
        ]
        args += [*past, h0]
        scratch += [pltpu.VMEM((SSM_GROUPS, q, GROUP_WIDTH), F32),
                    pltpu.VMEM((SSM_GROUPS, GROUP_WIDTH, q), BF16)]
    else:
        scratch += [pltpu.VMEM((SSM_GROUPS, SSM_STATE, GROUP_WIDTH), F32),
                    pltpu.VMEM((SUBLANES, D_INNER), F32),
                    pltpu.VMEM((SUBLANES, BC_DIM), F32)]
    return pl.pallas_call(
        functools.partial(_ssd_kernel, short=short, q=q, seqs=seqs),
        grid=grid,
        in_specs=in_specs,
        out_specs=[pl.BlockSpec((q, D_INNER), lambda a, b: (blk(a, b), 0)), state_spec],
        out_shape=[jax.ShapeDtypeStruct((m, D_INNER), BF16),
                   jax.ShapeDtypeStruct((nseq, D_INNER, SSM_STATE), F32)],
        scratch_shapes=scratch,
        compiler_params=_cparams(("arbitrary", "arbitrary")),
        name="ssd_short" if short else "ssd_long",
    )(*args)


def _mix_kernel(*refs, short, tiles_per_seq):
    yn_ref, scb_ref, scc_ref, sch_ref, ga_ref, gb_ref, cw_ref, wa_ref, wb_ref = refs[:9]
    refs = refs[9:]
    if short:
        past_ref = refs[0]
        refs = refs[1:]
    mixed_ref, utail_ref = refs[:2]
    tm = yn_ref.shape[0]

    u = scc_ref[...].astype(F32) * sch_ref[...].astype(F32)
    if short:
        shift = lambda k: _shift_short(u, past_ref[...], k, SC_CONV)
        utail_ref[...] = u
    else:
        halo_scr = refs[2]
        first = pl.program_id(0) % tiles_per_seq == 0
        halo = jnp.where(first, 0.0, halo_scr[...])
        shift = lambda k: _shift_long(u, halo, k)
        halo_scr[...] = u[tm - SUBLANES:]
        utail_ref[...] = u[tm - SUBLANES:]
    s = (scb_ref[...].astype(F32) * _causal_conv(u, cw_ref[...], shift)).astype(BF16)

    a = _dot(yn_ref[...], wa_ref[...])
    b = _dot(s, wb_ref[...])
    mixed = _sigmoid(ga_ref[...].astype(F32)) * a + _sigmoid(gb_ref[...].astype(F32)) * b
    mixed_ref[...] = mixed.astype(BF16)


def _mix(yn, proj, cw, wa, wb, seq_len, tm, short, past=None):
    m = yn.shape[0]
    col = lambda c: (D_INNER + XBC_DIM) // D_MODEL + c
    rows = lambda c: pl.BlockSpec((tm, D_MODEL), lambda i: (i, col(c)))
    resident = lambda shape: pl.BlockSpec(shape, lambda i: (0, 0), pipeline_mode=pl.Buffered(1))
    in_specs = [
        pl.BlockSpec((tm, D_INNER), lambda i: (i, 0)),
        rows(0), rows(1), rows(2), rows(3), rows(4),
        resident((SC_CONV, SC_DIM)),
        resident((D_INNER, D_MODEL)),
        resident((SC_DIM, D_MODEL)),
    ]
    args = [yn, proj, proj, proj, proj, proj, cw, wa, wb]
    scratch = []
    if short:
        in_specs.append(pl.BlockSpec((tm, SC_DIM), lambda i: (i, 0)))
        args.append(past)
        tail_rows = tm
    else:
        scratch.append(pltpu.VMEM((SUBLANES, SC_DIM), F32))
        tail_rows = SUBLANES
    return pl.pallas_call(
        functools.partial(_mix_kernel, short=short, tiles_per_seq=max(seq_len // tm, 1)),
        grid=(m // tm,),
        in_specs=in_specs,
        out_specs=[pl.BlockSpec((tm, D_MODEL), lambda i: (i, 0)),
                   pl.BlockSpec((tail_rows, SC_DIM), lambda i: (i, 0))],
        out_shape=[jax.ShapeDtypeStruct((m, D_MODEL), BF16),
                   jax.ShapeDtypeStruct((m // tm * tail_rows, SC_DIM), F32)],
        scratch_shapes=scratch,
        compiler_params=_cparams(("arbitrary",)),
        name="mix",
    )(*args)


def _row_halves(ref):
    half = ref.shape[0] // 2
    return [slice(0, half), slice(half, 2 * half)]


def _out_proj_kernel(mixed_ref, x_ref, wo_ref, g_ref, x1_ref, n2_ref):
    for rs in _row_halves(x_ref):
        x1 = x_ref[rs, :] + _dot(mixed_ref[rs, :], wo_ref[...])
        x1_ref[rs, :] = x1
        n2_ref[rs, :] = _rms(x1, g_ref[...]).astype(BF16)


def _out_proj(mixed, x, wo, g, tm):
    m = x.shape[0]
    row = lambda i: (i, 0)
    const = lambda i: (0, 0)
    return pl.pallas_call(
        _out_proj_kernel,
        grid=(m // tm,),
        in_specs=[pl.BlockSpec((tm, D_MODEL), row), pl.BlockSpec((tm, D_MODEL), row),
                  pl.BlockSpec((D_MODEL, D_MODEL), const, pipeline_mode=pl.Buffered(1)),
                  pl.BlockSpec((1, D_MODEL), const)],
        out_specs=[pl.BlockSpec((tm, D_MODEL), row), pl.BlockSpec((tm, D_MODEL), row)],
        out_shape=[jax.ShapeDtypeStruct((m, D_MODEL), F32),
                   jax.ShapeDtypeStruct((m, D_MODEL), BF16)],
        compiler_params=_cparams(("parallel",)),
        name="out_proj",
    )(mixed, x, wo, g)


def _ffn_up_kernel(*refs, short, tiles_per_seq, splits):
    n2_ref, wg_ref, wu_ref, cw_ref, cb_ref = refs[:5]
    refs = refs[5:]
    if short:
        past_ref = refs[0]
        refs = refs[1:]
    act_ref, gtail_ref = refs[:2]
    tm = n2_ref.shape[0]
    n2 = n2_ref[...]
    j = pl.program_id(1)
    first = pl.program_id(0) % tiles_per_seq == 0
    lo = 0
    for width in splits:
        cs = slice(lo, lo + width)
        lo += width
        gate = _dot(n2, wg_ref[:, cs])
        up = _dot(n2, wu_ref[:, cs])
        if short:
            past = past_ref[:, cs]
            shift = lambda k, gate=gate, past=past: _shift_short(gate, past, k, FFN_CONV)
            gtail_ref[:, cs] = gate
        else:
            halo_scr = refs[2]
            halo = jnp.where(first, 0.0, halo_scr[j, :, cs])
            shift = lambda k, gate=gate, halo=halo: _shift_long(gate, halo, k)
            halo_scr[j, :, cs] = gate[tm - SUBLANES:]
            gtail_ref[:, cs] = gate[tm - SUBLANES:]
        conv = cb_ref[:, cs] + _causal_conv(gate, cw_ref[:, cs], shift)
        act_ref[:, cs] = (_silu(conv) * up).astype(BF16)


def _ffn_up(n2, wg, wu, cw, cb, seq_len, tm, tn, splits, short, past=None):
    m = n2.shape[0]
    nj = D_FF // tn
    assert sum(splits) == tn
    in_specs = [
        pl.BlockSpec((tm, D_MODEL), lambda i, j: (i, 0)),
        pl.BlockSpec((D_MODEL, tn), lambda i, j: (0, j)),
        pl.BlockSpec((D_MODEL, tn), lambda i, j: (0, j)),
        pl.BlockSpec((FFN_CONV, tn), lambda i, j: (0, j)),
        pl.BlockSpec((1, tn), lambda i, j: (0, j)),
    ]
    args = [n2, wg, wu, cw, cb]
    scratch = []
    if short:
        in_specs.append(pl.BlockSpec((tm, tn), lambda i, j: (i, j)))
        args.append(past)
        tail_rows = tm
    else:
        scratch.append(pltpu.VMEM((nj, SUBLANES, tn), F32))
        tail_rows = SUBLANES
    return pl.pallas_call(
        functools.partial(_ffn_up_kernel, short=short, tiles_per_seq=max(seq_len // tm, 1),
                          splits=splits),
        grid=(m // tm, nj),
        in_specs=in_specs,
        out_specs=[pl.BlockSpec((tm, tn), lambda i, j: (i, j)),
                   pl.BlockSpec((tail_rows, tn), lambda i, j: (i, j))],
        out_shape=[jax.ShapeDtypeStruct((m, D_FF), BF16),
                   jax.ShapeDtypeStruct((m // tm * tail_rows, D_FF), F32)],
        scratch_shapes=scratch,
        compiler_params=_cparams(("arbitrary", "arbitrary")),
        name="ffn_up",
    )(*args)


def _ffn_down_kernel(act_ref, wd_ref, x1_ref, x2_ref):
    x2_ref[...] = x1_ref[...] + _dot(act_ref[...], wd_ref[...])


def _ffn_down(act, wd, x1, tm):
    m = x1.shape[0]
    return pl.pallas_call(
        _ffn_down_kernel,
        grid=(m // tm,),
        in_specs=[pl.BlockSpec((tm, D_FF), lambda i: (i, 0)),
                  pl.BlockSpec((D_FF, D_MODEL), lambda i: (0, 0), pipeline_mode=pl.Buffered(1)),
                  pl.BlockSpec((tm, D_MODEL), lambda i: (i, 0))],
        out_specs=pl.BlockSpec((tm, D_MODEL), lambda i: (i, 0)),
        out_shape=jax.ShapeDtypeStruct((m, D_MODEL), F32),
        compiler_params=_cparams(("parallel",)),
        name="ffn_down",
    )(act, wd, x1)


def _ple_kernel(x2_ref, pe_ref, wple_ref, wgate_ref, gp_ref, gf_ref, y_ref):
    for rs in _row_halves(x2_ref):
        x2 = x2_ref[rs, :]
        n3 = _rms(x2, gp_ref[...]).astype(BF16)
        gate = _sigmoid(_dot(n3, wgate_ref[...]))
        emb = _dot(pe_ref[rs, :].astype(BF16), wple_ref[...])
        y_ref[rs, :] = _rms(x2 + emb * gate, gf_ref[...])


def _ple(x2, pe, wple, wgate, gp, gf, tm):
    m = x2.shape[0]
    row = lambda i: (i, 0)
    const = lambda i: (0, 0)
    return pl.pallas_call(
        _ple_kernel,
        grid=(m // tm,),
        in_specs=[pl.BlockSpec((tm, D_MODEL), row), pl.BlockSpec((tm, PLE_DIM), row),
                  pl.BlockSpec((PLE_DIM, D_MODEL), const),
                  pl.BlockSpec((D_MODEL, D_MODEL), const, pipeline_mode=pl.Buffered(1)),
                  pl.BlockSpec((1, D_MODEL), const), pl.BlockSpec((1, D_MODEL), const)],
        out_specs=pl.BlockSpec((tm, D_MODEL), row),
        out_shape=jax.ShapeDtypeStruct((m, D_MODEL), F32),
        compiler_params=_cparams(("parallel",)),
        name="ple",
    )(x2, pe, wple, wgate, gp, gf)


def _pad_steps(state):
    s, k, c = state.shape
    return jnp.pad(state, ((0, 0), (0, DEC_SEQ - k), (0, 0))).reshape(s * DEC_SEQ, c)


def _tile(m, want):
    return min(m, want)


def _layer(x, pe, w, short, states=None):
    nseq, seq_len, _ = x.shape
    m = nseq * seq_len
    x2d = x.reshape(m, D_MODEL)
    pe2d = pe.reshape(m, PLE_DIM)
    if short:
        ssm0, xbc0, sc0, ffn0 = states
        past_xs = _pad_steps(xbc0[:, :, :D_INNER])
        past_bc = _pad_steps(xbc0[:, :, D_INNER:])
        past_sc = _pad_steps(sc0)
        past_ffn = _pad_steps(ffn0)
        h0 = ssm0.reshape(nseq, D_INNER, SSM_STATE)

    proj, dt_raw = _in_proj(x2d, w["norm_mix"], w["w_in"], w["w_in_hi"], _tile(m, 1024), 1024)

    ssd_w = (w["cw_xs"], w["cb_xs"], w["cw_bc"], w["cb_bc"], w["dt_bias"], w["a_neg"],
             w["d_exp"], w["ssm_norm"])
    consts = _ssd_consts(SSM_CHUNK, short)
    if short:
        yn, ssm_new = _ssd(proj, dt_raw, consts, ssd_w, nseq, True, (past_xs, past_bc), h0)
    else:
        yn, ssm_new = _ssd(proj, dt_raw, consts, ssd_w, nseq, False)

    tm = _tile(m, 512)
    tm_mix = _tile(m, 128 if short else 256)
    mixed, u_tail = _mix(yn, proj, w["sc_conv_w"], w["w_ssm_out"], w["w_sc_out"], seq_len,
                         tm_mix, short, past_sc if short else None)
    x1, n2 = _out_proj(mixed, x2d, w["w_o"], w["norm_ffn"], tm)

    tm_ffn = _tile(m, 1024)
    tn_ffn, splits = (512, (256, 256)) if short else (1408, (512, 512, 384))
    act, g_tail = _ffn_up(n2, w["w_ffn_gate"], w["w_ffn_up"], w["ffn_conv_w"], w["ffn_conv_b"],
                          seq_len, tm_ffn, tn_ffn, splits, short, past_ffn if short else None)
    x2 = _ffn_down(act, w["w_ffn_down"], x1, _tile(m, 256))
    y = _ple(x2, pe2d, w["w_ple"], w["w_ple_gate"], w["norm_ple"], w["norm_final"], tm)

    proj3 = proj.reshape(nseq, seq_len, PROJ_DIM)
    xbc_new = proj3[:, seq_len - (SSM_CONV - 1):, D_INNER:D_INNER + XBC_DIM].astype(F32)

    def last_rows(tail, tile_rows, taps, width):
        if short:
            t = tail.reshape(nseq, seq_len, width)
        else:
            per_seq = seq_len // tile_rows
            t = tail.reshape(nseq, per_seq, SUBLANES, width)[:, per_seq - 1]
        return t[:, t.shape[1] - (taps - 1):]

    sc_new = last_rows(u_tail, tm_mix, SC_CONV, SC_DIM)
    ffn_new = last_rows(g_tail, tm_ffn, FFN_CONV, D_FF)
    ssm_new = ssm_new.reshape(nseq, SSM_HEADS, SSM_HEAD_DIM, SSM_STATE)
    return y.reshape(nseq, seq_len, D_MODEL), ssm_new, xbc_new, sc_new, ffn_new


def _prep_weights(norm_mix, w_in, ssm_conv_w, ssm_conv_b, ssm_dt_bias, ssm_a_log, ssm_d, ssm_norm,
                  w_ssm_out, sc_conv_w, w_sc_out, w_o, norm_ffn, w_ffn_gate, w_ffn_up,
                  ffn_conv_w, ffn_conv_b, w_ffn_down, norm_ple, w_ple, w_ple_gate, norm_final):
    row = lambda v: v.reshape(1, -1).astype(F32)
    pad_heads = lambda v: jnp.pad(v.astype(F32), (0, DT_PAD - SSM_HEADS)).reshape(1, DT_PAD)
    return {
        "norm_mix": row(norm_mix),
        "w_in": w_in,
        "w_in_hi": _repack_hi(w_in, 512),
        "cw_xs": ssm_conv_w[:, :D_INNER].astype(F32),
        "cb_xs": row(ssm_conv_b[:D_INNER]),
        "cw_bc": ssm_conv_w[:, D_INNER:].astype(F32),
        "cb_bc": row(ssm_conv_b[D_INNER:]),
        "dt_bias": pad_heads(ssm_dt_bias),
        "a_neg": pad_heads(-jnp.exp(ssm_a_log.astype(F32))),
        "d_exp": row(jnp.repeat(ssm_d, SSM_HEAD_DIM)),
        "ssm_norm": row(ssm_norm),
        "w_ssm_out": w_ssm_out.astype(BF16),
        "sc_conv_w": sc_conv_w.astype(F32),
        "w_sc_out": w_sc_out.astype(BF16),
        "w_o": w_o.astype(BF16),
        "norm_ffn": row(norm_ffn),
        "w_ffn_gate": w_ffn_gate.astype(BF16),
        "w_ffn_up": w_ffn_up.astype(BF16),
        "ffn_conv_w": ffn_conv_w.astype(F32),
        "ffn_conv_b": row(ffn_conv_b),
        "w_ffn_down": w_ffn_down.astype(BF16),
        "norm_ple": row(norm_ple),
        "w_ple": w_ple.astype(BF16),
        "w_ple_gate": w_ple_gate.astype(BF16),
        "norm_final": row(norm_final),
    }


def kernel(x_prompt, x_sample, state_ssm, state_ssm_conv, state_short_conv, state_ffn_conv, p_prompt, p_sample, norm_mix, w_in, ssm_conv_w, ssm_conv_b, ssm_dt_bias, ssm_a_log, ssm_d, ssm_norm, w_ssm_out, sc_conv_w, w_sc_out, w_o, norm_ffn, w_ffn_gate, w_ffn_up, ffn_conv_w, ffn_conv_b, w_ffn_down, norm_ple, w_ple, w_ple_gate, norm_final):
    depth = w_in.shape[0]
    assert depth == 1, "the final norm is fused into the layer's last kernel"
    assert x_sample.shape[1] == DEC_SEQ
    layer_w = (norm_mix, w_in, ssm_conv_w, ssm_conv_b, ssm_dt_bias, ssm_a_log, ssm_d, ssm_norm,
               w_ssm_out, sc_conv_w, w_sc_out, w_o, norm_ffn, w_ffn_gate, w_ffn_up,
               ffn_conv_w, ffn_conv_b, w_ffn_down, norm_ple, w_ple, w_ple_gate)
    w = _prep_weights(*[t[0] for t in layer_w], norm_final)
    yp, *prompt_states = _layer(x_prompt, p_prompt[0], w, short=False)
    ys, *sample_states = _layer(
        x_sample, p_sample[0], w, short=True,
        states=(state_ssm[0], state_ssm_conv[0], state_short_conv[0], state_ffn_conv[0]))
    outs = [yp, ys]
    for sp, ss in zip(prompt_states, sample_states):
        outs += [sp[None], ss[None]]
    return tuple(outs)
```

```python
import functools

import numpy as np
import jax
import jax.numpy as jnp
from jax import lax
from jax.experimental import pallas as pl
from jax.experimental.pallas import tpu as pltpu

F32 = jnp.float32
BF16 = jnp.bfloat16

D_MODEL = 2048
D_INNER = 4096
SSM_HEADS = 64
SSM_HEAD_DIM = 64
SSM_GROUPS = 8
HEADS_PER_GROUP = SSM_HEADS // SSM_GROUPS
GROUP_WIDTH = D_INNER // SSM_GROUPS
SSM_STATE = 128
SSM_CHUNK = 128
SSM_CONV = 4
BC_DIM = 2 * SSM_GROUPS * SSM_STATE
XBC_DIM = D_INNER + BC_DIM
SC_DIM = D_MODEL
SC_CONV = 3
D_FF = 5632
FFN_CONV = 3
PLE_DIM = 256
EPS = 1e-6
DEC_SEQ = 8
PROJ_DIM = D_INNER + XBC_DIM + 5 * D_MODEL
DT_PAD = 128

SUBLANES = 8
LOG2E = 1.4426950408889634
VMEM_LIMIT = 52 * 1024 * 1024


def _cparams(sem):
    return pltpu.CompilerParams(dimension_semantics=sem, vmem_limit_bytes=VMEM_LIMIT)


def _dot(a, b):
    return jnp.dot(a, b, preferred_element_type=F32)


def _dot_nt(a, b):
    return lax.dot_general(a, b, (((1,), (1,)), ((), ())), preferred_element_type=F32)


def _sigmoid(x):
    return 0.5 * jnp.tanh(0.5 * x) + 0.5


def _silu(x):
    h = 0.5 * x
    return h * jnp.tanh(h) + h


def _softplus(x):
    return jnp.maximum(x, 0.0) + jnp.log(1.0 + jnp.exp(-jnp.abs(x)))


def _rms(x, g):
    ms = jnp.mean(x * x, axis=-1, keepdims=True)
    return x * lax.rsqrt(ms + EPS) * g


def _split_bf16(a, terms):
    parts = []
    r = a
    for _ in range(terms - 1):
        p = r.astype(BF16)
        parts.append(p)
        r = r - p.astype(F32)
    parts.append(r.astype(BF16))
    return parts


def _dot_sel_right(a, sel, terms):
    out = None
    for p in _split_bf16(a, terms):
        d = _dot(p, sel)
        out = d if out is None else out + d
    return out


def _dot_sel_left(sel, a, terms):
    out = None
    for p in _split_bf16(a, terms):
        d = _dot(sel, p)
        out = d if out is None else out + d
    return out


def _shift_long(x, halo, k):
    r = pltpu.roll(x, k, 0)
    h = pltpu.roll(halo, k, 0)
    row = lax.broadcasted_iota(jnp.int32, halo.shape, 0)
    first = jnp.where(row < k, h, r[:SUBLANES])
    return jnp.concatenate([first, r[SUBLANES:]], axis=0)


def _shift_short(x, past, k, taps):
    n = x.shape[0]
    r = pltpu.roll(x, k, 0)
    s = (n + k - (taps - 1)) % n
    p = past if s == 0 else pltpu.roll(past, s, 0)
    t = lax.broadcasted_iota(jnp.int32, x.shape, 0) & (DEC_SEQ - 1)
    return jnp.where(t < k, p, r)


def _causal_conv(x, w, shift):
    taps = w.shape[0]
    y = w[taps - 1:taps] * x
    for k in range(1, taps):
        y = y + w[taps - 1 - k:taps - k] * shift(k)
    return y


def _in_proj_kernel(x_ref, g_ref, w_ref, wdt_ref, proj_ref, dt_ref, n1_scr, *, sub):
    @pl.when(pl.program_id(1) == 0)
    def _():
        n1 = _rms(x_ref[...], g_ref[...]).astype(BF16)
        n1_scr[...] = n1
        dt_ref[...] = _dot_nt(n1, wdt_ref[...].astype(BF16))

    n1 = n1_scr[...]
    for c in range(w_ref.shape[0] // sub):
        rs = slice(c * sub, (c + 1) * sub)
        proj_ref[:, rs] = _dot_nt(n1, w_ref[rs, :].astype(BF16)).astype(BF16)


def _in_proj(x, g, w_t, tm, tn):
    m = x.shape[0]
    dt_lo = D_INNER + XBC_DIM
    dt_hi = dt_lo + SSM_HEADS
    n_lo = dt_lo // tn
    w_row = lambda i, j: (
        pl.multiple_of(jnp.where(j < n_lo, j * tn, dt_hi + (j - n_lo) * tn), SSM_HEADS), 0)
    return pl.pallas_call(
        functools.partial(_in_proj_kernel, sub=256),
        grid=(m // tm, PROJ_DIM // tn),
        in_specs=[
            pl.BlockSpec((tm, D_MODEL), lambda i, j: (i, 0), pipeline_mode=pl.Buffered(1)),
            pl.BlockSpec((1, D_MODEL), lambda i, j: (0, 0)),
            pl.BlockSpec((pl.Element(tn), pl.Element(D_MODEL)), w_row),
            pl.BlockSpec((DT_PAD, D_MODEL), lambda i, j: (dt_lo // DT_PAD, 0)),
        ],
        out_specs=[
            pl.BlockSpec((tm, tn), lambda i, j: (i, j)),
            pl.BlockSpec((tm, DT_PAD), lambda i, j: (i, 0)),
        ],
        out_shape=[
            jax.ShapeDtypeStruct((m, PROJ_DIM), BF16),
            jax.ShapeDtypeStruct((m, DT_PAD), F32),
        ],
        scratch_shapes=[pltpu.VMEM((tm, D_MODEL), BF16)],
        compiler_params=_cparams(("parallel", "arbitrary")),
        name="in_proj",
    )(x, g, w_t, w_t)


def _ssd_consts(q, short):
    l = np.arange(q)[:, None]
    s = np.arange(q)[None, :]
    if short:
        tri = (l // DEC_SEQ == s // DEC_SEQ) & (s <= l)
        end = s == (l // DEC_SEQ) * DEC_SEQ + DEC_SEQ - 1
    else:
        tri = s <= l
        end = np.broadcast_to(s == q - 1, (q, q))
    mask = np.where(tri, 0.0, -np.inf).astype(np.float32)
    h = np.arange(128)[None, :, None]
    g = np.arange(SSM_GROUPS)[:, None, None]
    c = np.arange(GROUP_WIDTH)[None, None, :]
    expand = h == g * HEADS_PER_GROUP + c // SSM_HEAD_DIM
    return (jnp.asarray(tri, BF16), jnp.asarray(end, BF16), jnp.asarray(mask),
            jnp.asarray(expand, BF16))


def _ssd_kernel(*refs, short, q, seqs):
    z_ref, xs_ref, bc_ref, dt_ref, tri_ref, end_ref, mask_ref, exp_ref = refs[:8]
    refs = refs[8:]
    cwx_ref, cbx_ref, cwb_ref, cbb_ref, dtb_ref, a_ref, dexp_ref, gn_ref = refs[:8]
    refs = refs[8:]
    past_xs_ref = past_bc_ref = tail_xs = tail_bc = None
    if short:
        past_xs_ref, past_bc_ref, h0_ref = refs[:3]
        refs = refs[3:]
    y_ref, ht_ref = refs[:2]
    refs = refs[2:]
    xs_scr, b_scr, c_scr, y_scr, ea_scr, we_scr, aend_scr, acumt_scr, arow_scr = refs[:9]
    refs = refs[9:]
    if short:
        eae_scr, xwt_scr = refs
    else:
        h_scr, tail_xs, tail_bc = refs

    step = pl.program_id(1)
    nstep = pl.num_programs(1)

    def conv_block(src_ref, past_ref, tail, cw, cb_ref, cs):
        xb = src_ref[:, cs]
        x = xb.astype(F32)
        if short:
            past = past_ref[:, cs]
            shift = lambda k: _shift_short(x, past, k, SSM_CONV)
        else:
            halo = tail[:, cs]
            shift = lambda k: _shift_long(x, halo, k)
            tail[:, cs] = x[q - SUBLANES:]
        return _silu(cb_ref[:, cs] + _causal_conv(x, cw[:, cs], shift))

    def conv_group(g):
        cs = slice(g * GROUP_WIDTH, (g + 1) * GROUP_WIDTH)
        bs = slice(g * SSM_STATE, (g + 1) * SSM_STATE)
        cc = slice((SSM_GROUPS + g) * SSM_STATE, (SSM_GROUPS + g + 1) * SSM_STATE)
        cwx = cwx_ref[...]
        cwb = cwb_ref[...]
        return (conv_block(xs_ref, past_xs_ref, tail_xs, cwx, cbx_ref, cs),
                conv_block(bc_ref, past_bc_ref, tail_bc, cwb, cbb_ref, bs),
                conv_block(bc_ref, past_bc_ref, tail_bc, cwb, cbb_ref, cc))

    def dt_prelude():
        dt = _softplus(dt_ref[...] + dtb_ref[...])
        acum = _dot_sel_left(tri_ref[...], dt * a_ref[...], 3)
        aend = _dot_sel_left(end_ref[...], acum, 3)
        acumt = acum.T
        acumt_scr[...] = acumt * LOG2E
        arow_scr[...] = (acumt - jnp.log(dt.T)) * LOG2E
        aend_scr[...] = aend
        ea_scr[...] = jnp.exp(acum)
        we_scr[...] = jnp.exp(aend - acum) * dt

    def diag_group(g, xs, bg, cg):
        cb = _dot_nt(cg.astype(BF16), bg.astype(BF16))
        mask = mask_ref[...]
        lane = lax.broadcasted_iota(jnp.int32, (q, 128), 1)
        ys = []
        for pair in range(HEADS_PER_GROUP // 2):
            ws = []
            for half in range(2):
                h = g * HEADS_PER_GROUP + 2 * pair + half
                colb = jnp.broadcast_to(acumt_scr[pl.ds(h, 1), :], (q, q)).T
                seg = colb - arow_scr[pl.ds(h, 1), :] + mask
                ws.append((cb * jnp.exp2(seg)).astype(BF16))
            xp = xs[:, pair * 128:(pair + 1) * 128]
            top = jnp.where(lane < SSM_HEAD_DIM, xp, 0.0).astype(BF16)
            bot = jnp.where(lane >= SSM_HEAD_DIM, xp, 0.0).astype(BF16)
            ys.append(_dot(jnp.concatenate(ws, axis=1), jnp.concatenate([top, bot], axis=0)))
        return jnp.concatenate(ys, axis=1)

    def finish_group(g, y, xs):
        cs = slice(g * GROUP_WIDTH, (g + 1) * GROUP_WIDTH)
        y = y + dexp_ref[:, cs] * xs
        y = y * _silu(z_ref[:, cs].astype(F32))
        ms = jnp.mean(y * y, axis=-1, keepdims=True)
        y_ref[:, cs] = (y * lax.rsqrt(ms + EPS) * gn_ref[:, cs]).astype(BF16)

    if not short:
        @pl.when(step == 0)
        def _():
            h_scr[...] = jnp.zeros_like(h_scr)
            tail_xs[...] = jnp.zeros_like(tail_xs)
            tail_bc[...] = jnp.zeros_like(tail_bc)

        dt_prelude()
        for g in range(SSM_GROUPS):
            xs_scr[g], b_scr[g], c_scr[g] = conv_group(g)
        for g in range(SSM_GROUPS):
            xs, bg, cg = xs_scr[g], b_scr[g], c_scr[g]
            expand = exp_ref[g]
            y_diag = diag_group(g, xs, bg, cg)
            hg = h_scr[g]
            y_off = _dot(cg.astype(BF16), hg.astype(BF16))
            y_scr[g] = y_diag + y_off * _dot_sel_right(ea_scr[...], expand, 2)
            xw = (xs * _dot_sel_right(we_scr[...], expand, 2)).astype(BF16)
            st = _dot(bg.T.astype(BF16), xw)
            dec = jnp.exp(_dot_sel_right(aend_scr[:SUBLANES, :], expand, 3))[:1]
            h_scr[g] = dec * hg + st
        for g in range(SSM_GROUPS):
            finish_group(g, y_scr[g], xs_scr[g])

        @pl.when(step == nstep - 1)
        def _():
            for g in range(SSM_GROUPS):
                ht_ref[0, g * GROUP_WIDTH:(g + 1) * GROUP_WIDTH, :] = h_scr[g].T
    else:
        @pl.when(step == 0)
        def _():
            dt_prelude()
            for g in range(SSM_GROUPS):
                xs, bg, cg = conv_group(g)
                expand = exp_ref[g]
                xs_scr[g] = xs
                b_scr[g] = bg
                c_scr[g] = cg
                y_scr[g] = diag_group(g, xs, bg, cg)
                eae_scr[g] = _dot_sel_right(ea_scr[...], expand, 2)
                xwt_scr[g] = (xs * _dot_sel_right(we_scr[...], expand, 2)).T.astype(BF16)

        for u in range(seqs):
            sidx = step * seqs + u
            row = lax.broadcasted_iota(jnp.int32, (q, 128), 0)
            onehot = jnp.where(row == sidx * DEC_SEQ + DEC_SEQ - 1, 1.0, 0.0).astype(BF16)
            cdec = jnp.exp2(_dot_sel_right(acumt_scr[...], onehot, 3))
            in_seq = (row // DEC_SEQ) == sidx
            seq_rows = pl.ds(pl.multiple_of(sidx * DEC_SEQ, DEC_SEQ), DEC_SEQ)
            for g in range(SSM_GROUPS):
                rows = slice(g * GROUP_WIDTH, (g + 1) * GROUP_WIDTH)
                hg = h0_ref[u, rows, :]
                cj = c_scr[g, seq_rows, :].astype(BF16)
                y_off = _dot_nt(cj, hg.astype(BF16))
                y_scr[g, seq_rows, :] = y_scr[g, seq_rows, :] + y_off * eae_scr[g, seq_rows, :]
                bj = jnp.where(in_seq, b_scr[g], 0.0).astype(BF16)
                st = _dot(xwt_scr[g], bj)
                dec = [jnp.broadcast_to(cdec[g * HEADS_PER_GROUP + r:g * HEADS_PER_GROUP + r + 1],
                                        (SSM_HEAD_DIM, 128)) for r in range(HEADS_PER_GROUP)]
                ht_ref[u, rows, :] = jnp.concatenate(dec, axis=0) * hg + st

        @pl.when(step == nstep - 1)
        def _():
            for g in range(SSM_GROUPS):
                finish_group(g, y_scr[g], xs_scr[g])


def _ssd(proj, dt_raw, consts, wts, nseq, short, past=None, h0=None):
    m = proj.shape[0]
    q = SSM_CHUNK
    nblk = m // q
    if short:
        seqs = 2
        steps = q // DEC_SEQ // seqs
        grid = (nblk, steps)
        blk = lambda a, b: a
        st_idx = lambda a, b: (a * steps + b, 0, 0)
    else:
        seqs = 1
        steps = m // nseq // q
        grid = (nseq, steps)
        blk = lambda a, b: a * steps + b
        st_idx = lambda a, b: (a, 0, 0)
    const2 = lambda shape: pl.BlockSpec(shape, lambda a, b: (0, 0))
    const3 = lambda shape: pl.BlockSpec(shape, lambda a, b: (0, 0, 0))
    in_specs = [
        pl.BlockSpec((q, D_INNER), lambda a, b: (blk(a, b), 0)),
        pl.BlockSpec((q, D_INNER), lambda a, b: (blk(a, b), 1)),
        pl.BlockSpec((q, BC_DIM), lambda a, b: (blk(a, b), 2 * D_INNER // BC_DIM)),
        pl.BlockSpec((q, DT_PAD), lambda a, b: (blk(a, b), 0)),
        const2((q, q)), const2((q, q)), const2((q, q)),
        const3((SSM_GROUPS, 128, GROUP_WIDTH)),
        const2((SSM_CONV, D_INNER)), const2((1, D_INNER)),
        const2((SSM_CONV, BC_DIM)), const2((1, BC_DIM)),
        const2((1, DT_PAD)), const2((1, DT_PAD)),
        const2((1, D_INNER)), const2((1, D_INNER)),
    ]
    args = [proj, proj, proj, dt_raw, *consts, *wts]
    state_spec = pl.BlockSpec((seqs, D_INNER, SSM_STATE), st_idx)
    scratch = [
        pltpu.VMEM((SSM_GROUPS, q, GROUP_WIDTH), F32),
        pltpu.VMEM((SSM_GROUPS, q, SSM_STATE), F32),
        pltpu.VMEM((SSM_GROUPS, q, SSM_STATE), F32),
        pltpu.VMEM((SSM_GROUPS, q, GROUP_WIDTH), F32),
        pltpu.VMEM((q, 128), F32), pltpu.VMEM((q, 128), F32), pltpu.VMEM((q, 128), F32),
        pltpu.VMEM((128, q), F32), pltpu.VMEM((128, q), F32),
    ]
    if short:
        in_specs += [
            pl.BlockSpec((q, D_INNER), lambda a, b: (a, 0)),
            pl.BlockSpec((q, BC_DIM), lambda a, b: (a, 0)),
            state_spec,
        ]
        args += [*past, h0]
        scratch += [pltpu.VMEM((SSM_GROUPS, q, GROUP_WIDTH), F32),
                    pltpu.VMEM((SSM_GROUPS, GROUP_WIDTH, q), BF16)]
    else:
        scratch += [pltpu.VMEM((SSM_GROUPS, SSM_STATE, GROUP_WIDTH), F32),
                    pltpu.VMEM((SUBLANES, D_INNER), F32),
                    pltpu.VMEM((SUBLANES, BC_DIM), F32)]
    return pl.pallas_call(
        functools.partial(_ssd_kernel, short=short, q=q, seqs=seqs),
        grid=grid,
        in_specs=in_specs,
        out_specs=[pl.BlockSpec((q, D_INNER), lambda a, b: (blk(a, b), 0)), state_spec],
        out_shape=[jax.ShapeDtypeStruct((m, D_INNER), BF16),
                   jax.ShapeDtypeStruct((nseq, D_INNER, SSM_STATE), F32)],
        scratch_shapes=scratch,
        compiler_params=_cparams(("arbitrary", "arbitrary")),
        name="ssd_short" if short else "ssd_long",
    )(*args)


def _mix_kernel(*refs, short, tiles_per_seq):
    yn_ref, scb_ref, scc_ref, sch_ref, ga_ref, gb_ref, cw_ref, wa_ref, wb_ref = refs[:9]
    refs = refs[9:]
    if short:
        past_ref = refs[0]
        refs = refs[1:]
    mixed_ref, utail_ref = refs[:2]
    tm = yn_ref.shape[0]

    u = scc_ref[...].astype(F32) * sch_ref[...].astype(F32)
    if short:
        shift = lambda k: _shift_short(u, past_ref[...], k, SC_CONV)
        utail_ref[...] = u
    else:
        halo_scr = refs[2]
        first = pl.program_id(0) % tiles_per_seq == 0
        halo = jnp.where(first, 0.0, halo_scr[...])
        shift = lambda k: _shift_long(u, halo, k)
        halo_scr[...] = u[tm - SUBLANES:]
        utail_ref[...] = u[tm - SUBLANES:]
    s = (scb_ref[...].astype(F32) * _causal_conv(u, cw_ref[...], shift)).astype(BF16)

    a = _dot(yn_ref[...], wa_ref[...])
    b = _dot(s, wb_ref[...])
    mixed = _sigmoid(ga_ref[...].astype(F32)) * a + _sigmoid(gb_ref[...].astype(F32)) * b
    mixed_ref[...] = mixed.astype(BF16)


def _mix(yn, proj, cw, wa, wb, seq_len, tm, short, past=None):
    m = yn.shape[0]
    col = lambda c: (D_INNER + XBC_DIM) // D_MODEL + c
    rows = lambda c: pl.BlockSpec((tm, D_MODEL), lambda i: (i, col(c)))
    resident = lambda shape: pl.BlockSpec(shape, lambda i: (0, 0), pipeline_mode=pl.Buffered(1))
    in_specs = [
        pl.BlockSpec((tm, D_INNER), lambda i: (i, 0)),
        rows(0), rows(1), rows(2), rows(3), rows(4),
        resident((SC_CONV, SC_DIM)),
        resident((D_INNER, D_MODEL)),
        resident((SC_DIM, D_MODEL)),
    ]
    args = [yn, proj, proj, proj, proj, proj, cw, wa, wb]
    scratch = []
    if short:
        in_specs.append(pl.BlockSpec((tm, SC_DIM), lambda i: (i, 0)))
        args.append(past)
        tail_rows = tm
    else:
        scratch.append(pltpu.VMEM((SUBLANES, SC_DIM), F32))
        tail_rows = SUBLANES
    return pl.pallas_call(
        functools.partial(_mix_kernel, short=short, tiles_per_seq=max(seq_len // tm, 1)),
        grid=(m // tm,),
        in_specs=in_specs,
        out_specs=[pl.BlockSpec((tm, D_MODEL), lambda i: (i, 0)),
                   pl.BlockSpec((tail_rows, SC_DIM), lambda i: (i, 0))],
        out_shape=[jax.ShapeDtypeStruct((m, D_MODEL), BF16),
                   jax.ShapeDtypeStruct((m // tm * tail_rows, SC_DIM), F32)],
        scratch_shapes=scratch,
        compiler_params=_cparams(("arbitrary",)),
        name="mix",
    )(*args)


def _row_halves(ref):
    half = ref.shape[0] // 2
    return [slice(0, half), slice(half, 2 * half)]


def _out_proj_kernel(mixed_ref, x_ref, wo_ref, g_ref, x1_ref, n2_ref):
    for rs in _row_halves(x_ref):
        x1 = x_ref[rs, :] + _dot(mixed_ref[rs, :], wo_ref[...])
        x1_ref[rs, :] = x1
        n2_ref[rs, :] = _rms(x1, g_ref[...]).astype(BF16)


def _out_proj(mixed, x, wo, g, tm):
    m = x.shape[0]
    row = lambda i: (i, 0)
    const = lambda i: (0, 0)
    return pl.pallas_call(
        _out_proj_kernel,
        grid=(m // tm,),
        in_specs=[pl.BlockSpec((tm, D_MODEL), row), pl.BlockSpec((tm, D_MODEL), row),
                  pl.BlockSpec((D_MODEL, D_MODEL), const, pipeline_mode=pl.Buffered(1)),
                  pl.BlockSpec((1, D_MODEL), const)],
        out_specs=[pl.BlockSpec((tm, D_MODEL), row), pl.BlockSpec((tm, D_MODEL), row)],
        out_shape=[jax.ShapeDtypeStruct((m, D_MODEL), F32),
                   jax.ShapeDtypeStruct((m, D_MODEL), BF16)],
        compiler_params=_cparams(("parallel",)),
        name="out_proj",
    )(mixed, x, wo, g)


def _ffn_up_kernel(*refs, short, tiles_per_seq, splits):
    n2_ref, wg_ref, wu_ref, cw_ref, cb_ref = refs[:5]
    refs = refs[5:]
    if short:
        past_ref = refs[0]
        refs = refs[1:]
    act_ref, gtail_ref = refs[:2]
    tm = n2_ref.shape[0]
    n2 = n2_ref[...]
    j = pl.program_id(1)
    first = pl.program_id(0) % tiles_per_seq == 0
    lo = 0
    for width in splits:
        cs = slice(lo, lo + width)
        lo += width
        gate = _dot(n2, wg_ref[:, cs])
        up = _dot(n2, wu_ref[:, cs])
        if short:
            past = past_ref[:, cs]
            shift = lambda k, gate=gate, past=past: _shift_short(gate, past, k, FFN_CONV)
            gtail_ref[:, cs] = gate
        else:
            halo_scr = refs[2]
            halo = jnp.where(first, 0.0, halo_scr[j, :, cs])
            shift = lambda k, gate=gate, halo=halo: _shift_long(gate, halo, k)
            halo_scr[j, :, cs] = gate[tm - SUBLANES:]
            gtail_ref[:, cs] = gate[tm - SUBLANES:]
        conv = cb_ref[:, cs] + _causal_conv(gate, cw_ref[:, cs], shift)
        act_ref[:, cs] = (_silu(conv) * up).astype(BF16)


def _ffn_up(n2, wg, wu, cw, cb, seq_len, tm, tn, splits, short, past=None):
    m = n2.shape[0]
    nj = D_FF // tn
    assert sum(splits) == tn
    in_specs = [
        pl.BlockSpec((tm, D_MODEL), lambda i, j: (i, 0)),
        pl.BlockSpec((D_MODEL, tn), lambda i, j: (0, j)),
        pl.BlockSpec((D_MODEL, tn), lambda i, j: (0, j)),
        pl.BlockSpec((FFN_CONV, tn), lambda i, j: (0, j)),
        pl.BlockSpec((1, tn), lambda i, j: (0, j)),
    ]
    args = [n2, wg, wu, cw, cb]
    scratch = []
    if short:
        in_specs.append(pl.BlockSpec((tm, tn), lambda i, j: (i, j)))
        args.append(past)
        tail_rows = tm
    else:
        scratch.append(pltpu.VMEM((nj, SUBLANES, tn), F32))
        tail_rows = SUBLANES
    return pl.pallas_call(
        functools.partial(_ffn_up_kernel, short=short, tiles_per_seq=max(seq_len // tm, 1),
                          splits=splits),
        grid=(m // tm, nj),
        in_specs=in_specs,
        out_specs=[pl.BlockSpec((tm, tn), lambda i, j: (i, j)),
                   pl.BlockSpec((tail_rows, tn), lambda i, j: (i, j))],
        out_shape=[jax.ShapeDtypeStruct((m, D_FF), BF16),
                   jax.ShapeDtypeStruct((m // tm * tail_rows, D_FF), F32)],
        scratch_shapes=scratch,
        compiler_params=_cparams(("arbitrary", "arbitrary")),
        name="ffn_up",
    )(*args)


def _ffn_down_kernel(act_ref, wd_ref, x1_ref, x2_ref):
    x2_ref[...] = x1_ref[...] + _dot(act_ref[...], wd_ref[...])


def _ffn_down(act, wd, x1, tm):
    m = x1.shape[0]
    return pl.pallas_call(
        _ffn_down_kernel,
        grid=(m // tm,),
        in_specs=[pl.BlockSpec((tm, D_FF), lambda i: (i, 0)),
                  pl.BlockSpec((D_FF, D_MODEL), lambda i: (0, 0), pipeline_mode=pl.Buffered(1)),
                  pl.BlockSpec((tm, D_MODEL), lambda i: (i, 0))],
        out_specs=pl.BlockSpec((tm, D_MODEL), lambda i: (i, 0)),
        out_shape=jax.ShapeDtypeStruct((m, D_MODEL), F32),
        compiler_params=_cparams(("parallel",)),
        name="ffn_down",
    )(act, wd, x1)


def _ple_kernel(x2_ref, pe_ref, wple_ref, wgate_ref, gp_ref, gf_ref, y_ref):
    for rs in _row_halves(x2_ref):
        x2 = x2_ref[rs, :]
        n3 = _rms(x2, gp_ref[...]).astype(BF16)
        gate = _sigmoid(_dot(n3, wgate_ref[...]))
        emb = _dot(pe_ref[rs, :].astype(BF16), wple_ref[...])
        y_ref[rs, :] = _rms(x2 + emb * gate, gf_ref[...])


def _ple(x2, pe, wple, wgate, gp, gf, tm):
    m = x2.shape[0]
    row = lambda i: (i, 0)
    const = lambda i: (0, 0)
    return pl.pallas_call(
        _ple_kernel,
        grid=(m // tm,),
        in_specs=[pl.BlockSpec((tm, D_MODEL), row), pl.BlockSpec((tm, PLE_DIM), row),
                  pl.BlockSpec((PLE_DIM, D_MODEL), const),
                  pl.BlockSpec((D_MODEL, D_MODEL), const, pipeline_mode=pl.Buffered(1)),
                  pl.BlockSpec((1, D_MODEL), const), pl.BlockSpec((1, D_MODEL), const)],
        out_specs=pl.BlockSpec((tm, D_MODEL), row),
        out_shape=jax.ShapeDtypeStruct((m, D_MODEL), F32),
        compiler_params=_cparams(("parallel",)),
        name="ple",
    )(x2, pe, wple, wgate, gp, gf)


def _pad_steps(state):
    s, k, c = state.shape
    return jnp.pad(state, ((0, 0), (0, DEC_SEQ - k), (0, 0))).reshape(s * DEC_SEQ, c)


def _tile(m, want):
    return min(m, want)


def _layer(x, pe, w, short, states=None):
    nseq, seq_len, _ = x.shape
    m = nseq * seq_len
    x2d = x.reshape(m, D_MODEL)
    pe2d = pe.reshape(m, PLE_DIM)
    if short:
        ssm0, xbc0, sc0, ffn0 = states
        past_xs = _pad_steps(xbc0[:, :, :D_INNER])
        past_bc = _pad_steps(xbc0[:, :, D_INNER:])
        past_sc = _pad_steps(sc0)
        past_ffn = _pad_steps(ffn0)
        h0 = ssm0.reshape(nseq, D_INNER, SSM_STATE)

    proj, dt_raw = _in_proj(x2d, w["norm_mix"], w["w_in_t"], _tile(m, 1024), 1024)

    ssd_w = (w["cw_xs"], w["cb_xs"], w["cw_bc"], w["cb_bc"], w["dt_bias"], w["a_neg"],
             w["d_exp"], w["ssm_norm"])
    consts = _ssd_consts(SSM_CHUNK, short)
    if short:
        yn, ssm_new = _ssd(proj, dt_raw, consts, ssd_w, nseq, True, (past_xs, past_bc), h0)
    else:
        yn, ssm_new = _ssd(proj, dt_raw, consts, ssd_w, nseq, False)

    tm = _tile(m, 512)
    tm_mix = _tile(m, 128 if short else 256)
    mixed, u_tail = _mix(yn, proj, w["sc_conv_w"], w["w_ssm_out"], w["w_sc_out"], seq_len,
                         tm_mix, short, past_sc if short else None)
    x1, n2 = _out_proj(mixed, x2d, w["w_o"], w["norm_ffn"], tm)

    tm_ffn = _tile(m, 1024)
    tn_ffn, splits = (512, (256, 256)) if short else (1408, (512, 512, 384))
    act, g_tail = _ffn_up(n2, w["w_ffn_gate"], w["w_ffn_up"], w["ffn_conv_w"], w["ffn_conv_b"],
                          seq_len, tm_ffn, tn_ffn, splits, short, past_ffn if short else None)
    x2 = _ffn_down(act, w["w_ffn_down"], x1, _tile(m, 256))
    y = _ple(x2, pe2d, w["w_ple"], w["w_ple_gate"], w["norm_ple"], w["norm_final"], tm)

    proj3 = proj.reshape(nseq, seq_len, PROJ_DIM)
    xbc_new = proj3[:, seq_len - (SSM_CONV - 1):, D_INNER:D_INNER + XBC_DIM].astype(F32)

    def last_rows(tail, tile_rows, taps, width):
        if short:
            t = tail.reshape(nseq, seq_len, width)
        else:
            per_seq = seq_len // tile_rows
            t = tail.reshape(nseq, per_seq, SUBLANES, width)[:, per_seq - 1]
        return t[:, t.shape[1] - (taps - 1):]

    sc_new = last_rows(u_tail, tm_mix, SC_CONV, SC_DIM)
    ffn_new = last_rows(g_tail, tm_ffn, FFN_CONV, D_FF)
    ssm_new = ssm_new.reshape(nseq, SSM_HEADS, SSM_HEAD_DIM, SSM_STATE)
    return y.reshape(nseq, seq_len, D_MODEL), ssm_new, xbc_new, sc_new, ffn_new


def _prep_weights(norm_mix, w_in, ssm_conv_w, ssm_conv_b, ssm_dt_bias, ssm_a_log, ssm_d, ssm_norm,
                  w_ssm_out, sc_conv_w, w_sc_out, w_o, norm_ffn, w_ffn_gate, w_ffn_up,
                  ffn_conv_w, ffn_conv_b, w_ffn_down, norm_ple, w_ple, w_ple_gate, norm_final):
    row = lambda v: v.reshape(1, -1).astype(F32)
    pad_heads = lambda v: jnp.pad(v.astype(F32), (0, DT_PAD - SSM_HEADS)).reshape(1, DT_PAD)
    return {
        "norm_mix": row(norm_mix),
        "w_in_t": w_in.T,
        "cw_xs": ssm_conv_w[:, :D_INNER].astype(F32),
        "cb_xs": row(ssm_conv_b[:D_INNER]),
        "cw_bc": ssm_conv_w[:, D_INNER:].astype(F32),
        "cb_bc": row(ssm_conv_b[D_INNER:]),
        "dt_bias": pad_heads(ssm_dt_bias),
        "a_neg": pad_heads(-jnp.exp(ssm_a_log.astype(F32))),
        "d_exp": row(jnp.repeat(ssm_d, SSM_HEAD_DIM)),
        "ssm_norm": row(ssm_norm),
        "w_ssm_out": w_ssm_out.astype(BF16),
        "sc_conv_w": sc_conv_w.astype(F32),
        "w_sc_out": w_sc_out.astype(BF16),
        "w_o": w_o.astype(BF16),
        "norm_ffn": row(norm_ffn),
        "w_ffn_gate": w_ffn_gate.astype(BF16),
        "w_ffn_up": w_ffn_up.astype(BF16),
        "ffn_conv_w": ffn_conv_w.astype(F32),
        "ffn_conv_b": row(ffn_conv_b),
        "w_ffn_down": w_ffn_down.astype(BF16),
        "norm_ple": row(norm_ple),
        "w_ple": w_ple.astype(BF16),
        "w_ple_gate": w_ple_gate.astype(BF16),
        "norm_final": row(norm_final),
    }


def kernel(x_prompt, x_sample, state_ssm, state_ssm_conv, state_short_conv, state_ffn_conv, p_prompt, p_sample, norm_mix, w_in, ssm_conv_w, ssm_conv_b, ssm_dt_bias, ssm_a_log, ssm_d, ssm_norm, w_ssm_out, sc_conv_w, w_sc_out, w_o, norm_ffn, w_ffn_gate, w_ffn_up, ffn_conv_w, ffn_conv_b, w_ffn_down, norm_ple, w_ple, w_ple_gate, norm_final):
    depth = w_in.shape[0]
    assert depth == 1, "the final norm is fused into the layer's last kernel"
    assert x_sample.shape[1] == DEC_SEQ
    layer_w = (norm_mix, w_in, ssm_conv_w, ssm_conv_b, ssm_dt_bias, ssm_a_log, ssm_d, ssm_norm,
               w_ssm_out, sc_conv_w, w_sc_out, w_o, norm_ffn, w_ffn_gate, w_ffn_up,
               ffn_conv_w, ffn_conv_b, w_ffn_down, norm_ple, w_ple, w_ple_gate)
    w = _prep_weights(*[t[0] for t in layer_w], norm_final)
    yp, *prompt_states = _layer(x_prompt, p_prompt[0], w, short=False)
    ys, *sample_states = _layer(
        x_sample, p_sample[0], w, short=True,
        states=(state_ssm[0], state_ssm_conv[0], state_short_conv[0], state_ffn_conv[0]))
    outs = [yp, ys]
    for sp, ss in zip(prompt_states, sample_states):
        outs += [sp[None], ss[None]]
    return tuple(outs)
```

```python
import functools

import numpy as np
import jax
import jax.numpy as jnp
from jax import lax
from jax.experimental import pallas as pl
from jax.experimental.pallas import tpu as pltpu

F32 = jnp.float32
BF16 = jnp.bfloat16

D_MODEL = 2048
D_INNER = 4096
SSM_HEADS = 64
SSM_HEAD_DIM = 64
SSM_GROUPS = 8
HEADS_PER_GROUP = SSM_HEADS // SSM_GROUPS
GROUP_WIDTH = D_INNER // SSM_GROUPS
SSM_STATE = 128
SSM_CHUNK = 128
SSM_CONV = 4
BC_DIM = 2 * SSM_GROUPS * SSM_STATE
XBC_DIM = D_INNER + BC_DIM
SC_DIM = D_MODEL
SC_CONV = 3
D_FF = 5632
FFN_CONV = 3
PLE_DIM = 256
EPS = 1e-6
DEC_SEQ = 8
PROJ_DIM = D_INNER + XBC_DIM + 5 * D_MODEL
DT_PAD = 128

SUBLANES = 8
LOG2E = 1.4426950408889634
VMEM_LIMIT = 52 * 1024 * 1024


def _cparams(sem):
    return pltpu.CompilerParams(dimension_semantics=sem, vmem_limit_bytes=VMEM_LIMIT)


def _dot(a, b):
    return jnp.dot(a, b, preferred_element_type=F32)


def _dot_nt(a, b):
    return lax.dot_general(a, b, (((1,), (1,)), ((), ())), preferred_element_type=F32)


def _sigmoid(x):
    return 0.5 * jnp.tanh(0.5 * x) + 0.5


def _silu(x):
    h = 0.5 * x
    return h * jnp.tanh(h) + h


def _softplus(x):
    return jnp.maximum(x, 0.0) + jnp.log(1.0 + jnp.exp(-jnp.abs(x)))


def _rms(x, g):
    ms = jnp.mean(x * x, axis=-1, keepdims=True)
    return x * lax.rsqrt(ms + EPS) * g


def _split_bf16(a, terms):
    parts = []
    r = a
    for _ in range(terms - 1):
        p = r.astype(BF16)
        parts.append(p)
        r = r - p.astype(F32)
    parts.append(r.astype(BF16))
    return parts


def _dot_sel_right(a, sel, terms):
    out = None
    for p in _split_bf16(a, terms):
        d = _dot(p, sel)
        out = d if out is None else out + d
    return out


def _dot_sel_left(sel, a, terms):
    out = None
    for p in _split_bf16(a, terms):
        d = _dot(sel, p)
        out = d if out is None else out + d
    return out


def _shift_long(x, halo, k):
    r = pltpu.roll(x, k, 0)
    h = pltpu.roll(halo, k, 0)
    row = lax.broadcasted_iota(jnp.int32, halo.shape, 0)
    first = jnp.where(row < k, h, r[:SUBLANES])
    return jnp.concatenate([first, r[SUBLANES:]], axis=0)


def _shift_short(x, past, k, taps):
    n = x.shape[0]
    r = pltpu.roll(x, k, 0)
    s = (n + k - (taps - 1)) % n
    p = past if s == 0 else pltpu.roll(past, s, 0)
    t = lax.broadcasted_iota(jnp.int32, x.shape, 0) & (DEC_SEQ - 1)
    return jnp.where(t < k, p, r)


def _causal_conv(x, w, shift):
    taps = w.shape[0]
    y = w[taps - 1:taps] * x
    for k in range(1, taps):
        y = y + w[taps - 1 - k:taps - k] * shift(k)
    return y


def _in_proj_kernel(x_ref, g_ref, w_ref, wdt_ref, proj_ref, dt_ref, n1_scr, *, sub):
    @pl.when(pl.program_id(1) == 0)
    def _():
        n1 = _rms(x_ref[...], g_ref[...]).astype(BF16)
        n1_scr[...] = n1
        dt_ref[...] = _dot_nt(n1, wdt_ref[...].astype(BF16))

    n1 = n1_scr[...]
    for c in range(w_ref.shape[0] // sub):
        rs = slice(c * sub, (c + 1) * sub)
        proj_ref[:, rs] = _dot_nt(n1, w_ref[rs, :].astype(BF16)).astype(BF16)


def _in_proj(x, g, w_t, tm, tn):
    m = x.shape[0]
    dt_lo = D_INNER + XBC_DIM
    dt_hi = dt_lo + SSM_HEADS
    n_lo = dt_lo // tn
    w_row = lambda i, j: (
        pl.multiple_of(jnp.where(j < n_lo, j * tn, dt_hi + (j - n_lo) * tn), SSM_HEADS), 0)
    return pl.pallas_call(
        functools.partial(_in_proj_kernel, sub=256),
        grid=(m // tm, PROJ_DIM // tn),
        in_specs=[
            pl.BlockSpec((tm, D_MODEL), lambda i, j: (i, 0), pipeline_mode=pl.Buffered(1)),
            pl.BlockSpec((1, D_MODEL), lambda i, j: (0, 0)),
            pl.BlockSpec((pl.Element(tn), pl.Element(D_MODEL)), w_row),
            pl.BlockSpec((DT_PAD, D_MODEL), lambda i, j: (dt_lo // DT_PAD, 0)),
        ],
        out_specs=[
            pl.BlockSpec((tm, tn), lambda i, j: (i, j)),
            pl.BlockSpec((tm, DT_PAD), lambda i, j: (i, 0)),
        ],
        out_shape=[
            jax.ShapeDtypeStruct((m, PROJ_DIM), BF16),
            jax.ShapeDtypeStruct((m, DT_PAD), F32),
        ],
        scratch_shapes=[pltpu.VMEM((tm, D_MODEL), BF16)],
        compiler_params=_cparams(("parallel", "arbitrary")),
        name="in_proj",
    )(x, g, w_t, w_t)


def _ssd_consts(q, short):
    l = np.arange(q)[:, None]
    s = np.arange(q)[None, :]
    if short:
        tri = (l // DEC_SEQ == s // DEC_SEQ) & (s <= l)
        end = s == (l // DEC_SEQ) * DEC_SEQ + DEC_SEQ - 1
    else:
        tri = s <= l
        end = np.broadcast_to(s == q - 1, (q, q))
    mask = np.where(tri, 0.0, -np.inf).astype(np.float32)
    h = np.arange(128)[None, :, None]
    g = np.arange(SSM_GROUPS)[:, None, None]
    c = np.arange(GROUP_WIDTH)[None, None, :]
    expand = h == g * HEADS_PER_GROUP + c // SSM_HEAD_DIM
    return (jnp.asarray(tri, BF16), jnp.asarray(end, BF16), jnp.asarray(mask),
            jnp.asarray(expand, BF16))


def _ssd_kernel(*refs, short, q, seqs):
    z_ref, xs_ref, bc_ref, dt_ref, tri_ref, end_ref, mask_ref, exp_ref = refs[:8]
    refs = refs[8:]
    cwx_ref, cbx_ref, cwb_ref, cbb_ref, dtb_ref, a_ref, dexp_ref, gn_ref = refs[:8]
    refs = refs[8:]
    past_xs_ref = past_bc_ref = tail_xs = tail_bc = None
    if short:
        past_xs_ref, past_bc_ref, h0_ref = refs[:3]
        refs = refs[3:]
    y_ref, ht_ref = refs[:2]
    refs = refs[2:]
    xs_scr, b_scr, c_scr, y_scr, ea_scr, we_scr, aend_scr, acumt_scr, arow_scr = refs[:9]
    refs = refs[9:]
    if short:
        eae_scr, xwt_scr = refs
    else:
        h_scr, tail_xs, tail_bc = refs

    step = pl.program_id(1)
    nstep = pl.num_programs(1)

    def conv_block(src_ref, past_ref, tail, cw, cb_ref, cs):
        xb = src_ref[:, cs]
        x = xb.astype(F32)
        if short:
            past = past_ref[:, cs]
            shift = lambda k: _shift_short(x, past, k, SSM_CONV)
        else:
            halo = tail[:, cs]
            shift = lambda k: _shift_long(x, halo, k)
            tail[:, cs] = x[q - SUBLANES:]
        return _silu(cb_ref[:, cs] + _causal_conv(x, cw[:, cs], shift))

    def conv_group(g):
        cs = slice(g * GROUP_WIDTH, (g + 1) * GROUP_WIDTH)
        bs = slice(g * SSM_STATE, (g + 1) * SSM_STATE)
        cc = slice((SSM_GROUPS + g) * SSM_STATE, (SSM_GROUPS + g + 1) * SSM_STATE)
        cwx = cwx_ref[...]
        cwb = cwb_ref[...]
        return (conv_block(xs_ref, past_xs_ref, tail_xs, cwx, cbx_ref, cs),
                conv_block(bc_ref, past_bc_ref, tail_bc, cwb, cbb_ref, bs),
                conv_block(bc_ref, past_bc_ref, tail_bc, cwb, cbb_ref, cc))

    def dt_prelude():
        dt = _softplus(dt_ref[...] + dtb_ref[...])
        acum = _dot_sel_left(tri_ref[...], dt * a_ref[...], 3)
        aend = _dot_sel_left(end_ref[...], acum, 3)
        acumt = acum.T
        acumt_scr[...] = acumt * LOG2E
        arow_scr[...] = (acumt - jnp.log(dt.T)) * LOG2E
        aend_scr[...] = aend
        ea_scr[...] = jnp.exp(acum)
        we_scr[...] = jnp.exp(aend - acum) * dt

    def diag_group(g, xs, bg, cg):
        cb = _dot_nt(cg.astype(BF16), bg.astype(BF16))
        mask = mask_ref[...]
        lane = lax.broadcasted_iota(jnp.int32, (q, 128), 1)
        ys = []
        for pair in range(HEADS_PER_GROUP // 2):
            ws = []
            for half in range(2):
                h = g * HEADS_PER_GROUP + 2 * pair + half
                colb = jnp.broadcast_to(acumt_scr[pl.ds(h, 1), :], (q, q)).T
                seg = colb - arow_scr[pl.ds(h, 1), :] + mask
                ws.append((cb * jnp.exp2(seg)).astype(BF16))
            xp = xs[:, pair * 128:(pair + 1) * 128]
            top = jnp.where(lane < SSM_HEAD_DIM, xp, 0.0).astype(BF16)
            bot = jnp.where(lane >= SSM_HEAD_DIM, xp, 0.0).astype(BF16)
            ys.append(_dot(jnp.concatenate(ws, axis=1), jnp.concatenate([top, bot], axis=0)))
        return jnp.concatenate(ys, axis=1)

    def finish_group(g, y, xs):
        cs = slice(g * GROUP_WIDTH, (g + 1) * GROUP_WIDTH)
        y = y + dexp_ref[:, cs] * xs
        y = y * _silu(z_ref[:, cs].astype(F32))
        ms = jnp.mean(y * y, axis=-1, keepdims=True)
        y_ref[:, cs] = (y * lax.rsqrt(ms + EPS) * gn_ref[:, cs]).astype(BF16)

    if not short:
        @pl.when(step == 0)
        def _():
            h_scr[...] = jnp.zeros_like(h_scr)
            tail_xs[...] = jnp.zeros_like(tail_xs)
            tail_bc[...] = jnp.zeros_like(tail_bc)

        dt_prelude()
        for g in range(SSM_GROUPS):
            xs_scr[g], b_scr[g], c_scr[g] = conv_group(g)
        for g in range(SSM_GROUPS):
            xs, bg, cg = xs_scr[g], b_scr[g], c_scr[g]
            expand = exp_ref[g]
            y_diag = diag_group(g, xs, bg, cg)
            hg = h_scr[g]
            y_off = _dot(cg.astype(BF16), hg.astype(BF16))
            y_scr[g] = y_diag + y_off * _dot_sel_right(ea_scr[...], expand, 2)
            xw = (xs * _dot_sel_right(we_scr[...], expand, 2)).astype(BF16)
            st = _dot(bg.T.astype(BF16), xw)
            dec = jnp.exp(_dot_sel_right(aend_scr[:SUBLANES, :], expand, 3))[:1]
            h_scr[g] = dec * hg + st
        for g in range(SSM_GROUPS):
            finish_group(g, y_scr[g], xs_scr[g])

        @pl.when(step == nstep - 1)
        def _():
            for g in range(SSM_GROUPS):
                ht_ref[0, g * GROUP_WIDTH:(g + 1) * GROUP_WIDTH, :] = h_scr[g].T
    else:
        @pl.when(step == 0)
        def _():
            dt_prelude()
            for g in range(SSM_GROUPS):
                xs, bg, cg = conv_group(g)
                expand = exp_ref[g]
                xs_scr[g] = xs
                b_scr[g] = bg
                c_scr[g] = cg
                y_scr[g] = diag_group(g, xs, bg, cg)
                eae_scr[g] = _dot_sel_right(ea_scr[...], expand, 2)
                xwt_scr[g] = (xs * _dot_sel_right(we_scr[...], expand, 2)).T.astype(BF16)

        for u in range(seqs):
            sidx = step * seqs + u
            row = lax.broadcasted_iota(jnp.int32, (q, 128), 0)
            onehot = jnp.where(row == sidx * DEC_SEQ + DEC_SEQ - 1, 1.0, 0.0).astype(BF16)
            cdec = jnp.exp2(_dot_sel_right(acumt_scr[...], onehot, 3))
            in_seq = (row // DEC_SEQ) == sidx
            seq_rows = pl.ds(pl.multiple_of(sidx * DEC_SEQ, DEC_SEQ), DEC_SEQ)
            for g in range(SSM_GROUPS):
                rows = slice(g * GROUP_WIDTH, (g + 1) * GROUP_WIDTH)
                hg = h0_ref[u, rows, :]
                cj = c_scr[g, seq_rows, :].astype(BF16)
                y_off = _dot_nt(cj, hg.astype(BF16))
                y_scr[g, seq_rows, :] = y_scr[g, seq_rows, :] + y_off * eae_scr[g, seq_rows, :]
                bj = jnp.where(in_seq, b_scr[g], 0.0).astype(BF16)
                st = _dot(xwt_scr[g], bj)
                dec = [jnp.broadcast_to(cdec[g * HEADS_PER_GROUP + r:g * HEADS_PER_GROUP + r + 1],
                                        (SSM_HEAD_DIM, 128)) for r in range(HEADS_PER_GROUP)]
                ht_ref[u, rows, :] = jnp.concatenate(dec, axis=0) * hg + st

        @pl.when(step == nstep - 1)
        def _():
            for g in range(SSM_GROUPS):
                finish_group(g, y_scr[g], xs_scr[g])


def _ssd(proj, dt_raw, consts, wts, nseq, short, past=None, h0=None):
    m = proj.shape[0]
    q = SSM_CHUNK
    nblk = m // q
    if short:
        seqs = 2
        steps = q // DEC_SEQ // seqs
        grid = (nblk, steps)
        blk = lambda a, b: a
        st_idx = lambda a, b: (a * steps + b, 0, 0)
    else:
        seqs = 1
        steps = m // nseq // q
        grid = (nseq, steps)
        blk = lambda a, b: a * steps + b
        st_idx = lambda a, b: (a, 0, 0)
    const2 = lambda shape: pl.BlockSpec(shape, lambda a, b: (0, 0))
    const3 = lambda shape: pl.BlockSpec(shape, lambda a, b: (0, 0, 0))
    in_specs = [
        pl.BlockSpec((q, D_INNER), lambda a, b: (blk(a, b), 0)),
        pl.BlockSpec((q, D_INNER), lambda a, b: (blk(a, b), 1)),
        pl.BlockSpec((q, BC_DIM), lambda a, b: (blk(a, b), 2 * D_INNER // BC_DIM)),
        pl.BlockSpec((q, DT_PAD), lambda a, b: (blk(a, b), 0)),
        const2((q, q)), const2((q, q)), const2((q, q)),
        const3((SSM_GROUPS, 128, GROUP_WIDTH)),
        const2((SSM_CONV, D_INNER)), const2((1, D_INNER)),
        const2((SSM_CONV, BC_DIM)), const2((1, BC_DIM)),
        const2((1, DT_PAD)), const2((1, DT_PAD)),
        const2((1, D_INNER)), const2((1, D_INNER)),
    ]
    args = [proj, proj, proj, dt_raw, *consts, *wts]
    state_spec = pl.BlockSpec((seqs, D_INNER, SSM_STATE), st_idx)
    scratch = [
        pltpu.VMEM((SSM_GROUPS, q, GROUP_WIDTH), F32),
        pltpu.VMEM((SSM_GROUPS, q, SSM_STATE), F32),
        pltpu.VMEM((SSM_GROUPS, q, SSM_STATE), F32),
        pltpu.VMEM((SSM_GROUPS, q, GROUP_WIDTH), F32),
        pltpu.VMEM((q, 128), F32), pltpu.VMEM((q, 128), F32), pltpu.VMEM((q, 128), F32),
        pltpu.VMEM((128, q), F32), pltpu.VMEM((128, q), F32),
    ]
    if short:
        in_specs += [
            pl.BlockSpec((q, D_INNER), lambda a, b: (a, 0)),
            pl.BlockSpec((q, BC_DIM), lambda a, b: (a, 0)),
            state_spec,
        ]
        args += [*past, h0]
        scratch += [pltpu.VMEM((SSM_GROUPS, q, GROUP_WIDTH), F32),
                    pltpu.VMEM((SSM_GROUPS, GROUP_WIDTH, q), BF16)]
    else:
        scratch += [pltpu.VMEM((SSM_GROUPS, SSM_STATE, GROUP_WIDTH), F32),
                    pltpu.VMEM((SUBLANES, D_INNER), F32),
                    pltpu.VMEM((SUBLANES, BC_DIM), F32)]
    return pl.pallas_call(
        functools.partial(_ssd_kernel, short=short, q=q, seqs=seqs),
        grid=grid,
        in_specs=in_specs,
        out_specs=[pl.BlockSpec((q, D_INNER), lambda a, b: (blk(a, b), 0)), state_spec],
        out_shape=[jax.ShapeDtypeStruct((m, D_INNER), BF16),
                   jax.ShapeDtypeStruct((nseq, D_INNER, SSM_STATE), F32)],
        scratch_shapes=scratch,
        compiler_params=_cparams(("arbitrary", "arbitrary")),
        name="ssd_short" if short else "ssd_long",
    )(*args)


def _mix_kernel(*refs, short, tiles_per_seq):
    yn_ref, scb_ref, scc_ref, sch_ref, ga_ref, gb_ref, cw_ref, wa_ref, wb_ref = refs[:9]
    refs = refs[9:]
    if short:
        past_ref = refs[0]
        refs = refs[1:]
    mixed_ref, utail_ref = refs[:2]
    tm = yn_ref.shape[0]

    u = scc_ref[...].astype(F32) * sch_ref[...].astype(F32)
    if short:
        shift = lambda k: _shift_short(u, past_ref[...], k, SC_CONV)
        utail_ref[...] = u
    else:
        halo_scr = refs[2]
        first = pl.program_id(0) % tiles_per_seq == 0
        halo = jnp.where(first, 0.0, halo_scr[...])
        shift = lambda k: _shift_long(u, halo, k)
        halo_scr[...] = u[tm - SUBLANES:]
        utail_ref[...] = u[tm - SUBLANES:]
    s = (scb_ref[...].astype(F32) * _causal_conv(u, cw_ref[...], shift)).astype(BF16)

    a = _dot(yn_ref[...], wa_ref[...])
    b = _dot(s, wb_ref[...])
    mixed = _sigmoid(ga_ref[...].astype(F32)) * a + _sigmoid(gb_ref[...].astype(F32)) * b
    mixed_ref[...] = mixed.astype(BF16)


def _mix(yn, proj, cw, wa, wb, seq_len, tm, short, past=None):
    m = yn.shape[0]
    col = lambda c: (D_INNER + XBC_DIM) // D_MODEL + c
    rows = lambda c: pl.BlockSpec((tm, D_MODEL), lambda i: (i, col(c)))
    resident = lambda shape: pl.BlockSpec(shape, lambda i: (0, 0), pipeline_mode=pl.Buffered(1))
    in_specs = [
        pl.BlockSpec((tm, D_INNER), lambda i: (i, 0)),
        rows(0), rows(1), rows(2), rows(3), rows(4),
        resident((SC_CONV, SC_DIM)),
        resident((D_INNER, D_MODEL)),
        resident((SC_DIM, D_MODEL)),
    ]
    args = [yn, proj, proj, proj, proj, proj, cw, wa, wb]
    scratch = []
    if short:
        in_specs.append(pl.BlockSpec((tm, SC_DIM), lambda i: (i, 0)))
        args.append(past)
        tail_rows = tm
    else:
        scratch.append(pltpu.VMEM((SUBLANES, SC_DIM), F32))
        tail_rows = SUBLANES
    return pl.pallas_call(
        functools.partial(_mix_kernel, short=short, tiles_per_seq=max(seq_len // tm, 1)),
        grid=(m // tm,),
        in_specs=in_specs,
        out_specs=[pl.BlockSpec((tm, D_MODEL), lambda i: (i, 0)),
                   pl.BlockSpec((tail_rows, SC_DIM), lambda i: (i, 0))],
        out_shape=[jax.ShapeDtypeStruct((m, D_MODEL), BF16),
                   jax.ShapeDtypeStruct((m // tm * tail_rows, SC_DIM), F32)],
        scratch_shapes=scratch,
        compiler_params=_cparams(("arbitrary",)),
        name="mix",
    )(*args)


def _row_halves(ref):
    half = ref.shape[0] // 2
    return [slice(0, half), slice(half, 2 * half)]


def _out_proj_kernel(mixed_ref, x_ref, wo_ref, g_ref, x1_ref, n2_ref):
    for rs in _row_halves(x_ref):
        x1 = x_ref[rs, :] + _dot(mixed_ref[rs, :], wo_ref[...])
        x1_ref[rs, :] = x1
        n2_ref[rs, :] = _rms(x1, g_ref[...]).astype(BF16)


def _out_proj(mixed, x, wo, g, tm):
    m = x.shape[0]
    row = lambda i: (i, 0)
    const = lambda i: (0, 0)
    return pl.pallas_call(
        _out_proj_kernel,
        grid=(m // tm,),
        in_specs=[pl.BlockSpec((tm, D_MODEL), row), pl.BlockSpec((tm, D_MODEL), row),
                  pl.BlockSpec((D_MODEL, D_MODEL), const, pipeline_mode=pl.Buffered(1)),
                  pl.BlockSpec((1, D_MODEL), const)],
        out_specs=[pl.BlockSpec((tm, D_MODEL), row), pl.BlockSpec((tm, D_MODEL), row)],
        out_shape=[jax.ShapeDtypeStruct((m, D_MODEL), F32),
                   jax.ShapeDtypeStruct((m, D_MODEL), BF16)],
        compiler_params=_cparams(("parallel",)),
        name="out_proj",
    )(mixed, x, wo, g)


def _ffn_up_kernel(*refs, short, tiles_per_seq, splits):
    n2_ref, wg_ref, wu_ref, cw_ref, cb_ref = refs[:5]
    refs = refs[5:]
    if short:
        past_ref = refs[0]
        refs = refs[1:]
    act_ref, gtail_ref = refs[:2]
    tm = n2_ref.shape[0]
    n2 = n2_ref[...]
    j = pl.program_id(1)
    first = pl.program_id(0) % tiles_per_seq == 0
    lo = 0
    for width in splits:
        cs = slice(lo, lo + width)
        lo += width
        gate = _dot(n2, wg_ref[:, cs].astype(BF16))
        up = _dot(n2, wu_ref[:, cs].astype(BF16))
        if short:
            past = past_ref[:, cs]
            shift = lambda k, gate=gate, past=past: _shift_short(gate, past, k, FFN_CONV)
            gtail_ref[:, cs] = gate
        else:
            halo_scr = refs[2]
            halo = jnp.where(first, 0.0, halo_scr[j, :, cs])
            shift = lambda k, gate=gate, halo=halo: _shift_long(gate, halo, k)
            halo_scr[j, :, cs] = gate[tm - SUBLANES:]
            gtail_ref[:, cs] = gate[tm - SUBLANES:]
        conv = cb_ref[:, cs] + _causal_conv(gate, cw_ref[:, cs], shift)
        act_ref[:, cs] = (_silu(conv) * up).astype(BF16)


def _ffn_up(n2, wg, wu, cw, cb, seq_len, tm, tn, splits, short, past=None):
    m = n2.shape[0]
    nj = D_FF // tn
    assert sum(splits) == tn
    in_specs = [
        pl.BlockSpec((tm, D_MODEL), lambda i, j: (i, 0)),
        pl.BlockSpec((D_MODEL, tn), lambda i, j: (0, j)),
        pl.BlockSpec((D_MODEL, tn), lambda i, j: (0, j)),
        pl.BlockSpec((FFN_CONV, tn), lambda i, j: (0, j)),
        pl.BlockSpec((1, tn), lambda i, j: (0, j)),
    ]
    args = [n2, wg, wu, cw, cb]
    scratch = []
    if short:
        in_specs.append(pl.BlockSpec((tm, tn), lambda i, j: (i, j)))
        args.append(past)
        tail_rows = tm
    else:
        scratch.append(pltpu.VMEM((nj, SUBLANES, tn), F32))
        tail_rows = SUBLANES
    return pl.pallas_call(
        functools.partial(_ffn_up_kernel, short=short, tiles_per_seq=max(seq_len // tm, 1),
                          splits=splits),
        grid=(m // tm, nj),
        in_specs=in_specs,
        out_specs=[pl.BlockSpec((tm, tn), lambda i, j: (i, j)),
                   pl.BlockSpec((tail_rows, tn), lambda i, j: (i, j))],
        out_shape=[jax.ShapeDtypeStruct((m, D_FF), BF16),
                   jax.ShapeDtypeStruct((m // tm * tail_rows, D_FF), F32)],
        scratch_shapes=scratch,
        compiler_params=_cparams(("arbitrary", "arbitrary")),
        name="ffn_up",
    )(*args)


def _ffn_down_kernel(act_ref, wd_ref, x1_ref, x2_ref):
    x2_ref[...] = x1_ref[...] + _dot(act_ref[...], wd_ref[...])


def _ffn_down(act, wd, x1, tm):
    m = x1.shape[0]
    return pl.pallas_call(
        _ffn_down_kernel,
        grid=(m // tm,),
        in_specs=[pl.BlockSpec((tm, D_FF), lambda i: (i, 0)),
                  pl.BlockSpec((D_FF, D_MODEL), lambda i: (0, 0), pipeline_mode=pl.Buffered(1)),
                  pl.BlockSpec((tm, D_MODEL), lambda i: (i, 0))],
        out_specs=pl.BlockSpec((tm, D_MODEL), lambda i: (i, 0)),
        out_shape=jax.ShapeDtypeStruct((m, D_MODEL), F32),
        compiler_params=_cparams(("parallel",)),
        name="ffn_down",
    )(act, wd, x1)


def _ple_kernel(x2_ref, pe_ref, wple_ref, wgate_ref, gp_ref, gf_ref, y_ref):
    for rs in _row_halves(x2_ref):
        x2 = x2_ref[rs, :]
        n3 = _rms(x2, gp_ref[...]).astype(BF16)
        gate = _sigmoid(_dot(n3, wgate_ref[...]))
        emb = _dot(pe_ref[rs, :].astype(BF16), wple_ref[...])
        y_ref[rs, :] = _rms(x2 + emb * gate, gf_ref[...])


def _ple(x2, pe, wple, wgate, gp, gf, tm):
    m = x2.shape[0]
    row = lambda i: (i, 0)
    const = lambda i: (0, 0)
    return pl.pallas_call(
        _ple_kernel,
        grid=(m // tm,),
        in_specs=[pl.BlockSpec((tm, D_MODEL), row), pl.BlockSpec((tm, PLE_DIM), row),
                  pl.BlockSpec((PLE_DIM, D_MODEL), const),
                  pl.BlockSpec((D_MODEL, D_MODEL), const, pipeline_mode=pl.Buffered(1)),
                  pl.BlockSpec((1, D_MODEL), const), pl.BlockSpec((1, D_MODEL), const)],
        out_specs=pl.BlockSpec((tm, D_MODEL), row),
        out_shape=jax.ShapeDtypeStruct((m, D_MODEL), F32),
        compiler_params=_cparams(("parallel",)),
        name="ple",
    )(x2, pe, wple, wgate, gp, gf)


def _pad_steps(state):
    s, k, c = state.shape
    return jnp.pad(state, ((0, 0), (0, DEC_SEQ - k), (0, 0))).reshape(s * DEC_SEQ, c)


def _tile(m, want):
    return min(m, want)


def _layer(x, pe, w, short, states=None):
    nseq, seq_len, _ = x.shape
    m = nseq * seq_len
    x2d = x.reshape(m, D_MODEL)
    pe2d = pe.reshape(m, PLE_DIM)
    if short:
        ssm0, xbc0, sc0, ffn0 = states
        past_xs = _pad_steps(xbc0[:, :, :D_INNER])
        past_bc = _pad_steps(xbc0[:, :, D_INNER:])
        past_sc = _pad_steps(sc0)
        past_ffn = _pad_steps(ffn0)
        h0 = ssm0.reshape(nseq, D_INNER, SSM_STATE)

    proj, dt_raw = _in_proj(x2d, w["norm_mix"], w["w_in_t"], _tile(m, 1024), 1024)

    ssd_w = (w["cw_xs"], w["cb_xs"], w["cw_bc"], w["cb_bc"], w["dt_bias"], w["a_neg"],
             w["d_exp"], w["ssm_norm"])
    consts = _ssd_consts(SSM_CHUNK, short)
    if short:
        yn, ssm_new = _ssd(proj, dt_raw, consts, ssd_w, nseq, True, (past_xs, past_bc), h0)
    else:
        yn, ssm_new = _ssd(proj, dt_raw, consts, ssd_w, nseq, False)

    tm = _tile(m, 512)
    tm_mix = _tile(m, 128 if short else 256)
    mixed, u_tail = _mix(yn, proj, w["sc_conv_w"], w["w_ssm_out"], w["w_sc_out"], seq_len,
                         tm_mix, short, past_sc if short else None)
    x1, n2 = _out_proj(mixed, x2d, w["w_o"], w["norm_ffn"], tm)

    tm_ffn = _tile(m, 1024)
    act, g_tail = _ffn_up(n2, w["w_ffn_gate"], w["w_ffn_up"], w["ffn_conv_w"], w["ffn_conv_b"],
                          seq_len, tm_ffn, 512, (256, 256), short, past_ffn if short else None)
    x2 = _ffn_down(act, w["w_ffn_down"], x1, _tile(m, 256))
    y = _ple(x2, pe2d, w["w_ple"], w["w_ple_gate"], w["norm_ple"], w["norm_final"], tm)

    proj3 = proj.reshape(nseq, seq_len, PROJ_DIM)
    xbc_new = proj3[:, seq_len - (SSM_CONV - 1):, D_INNER:D_INNER + XBC_DIM].astype(F32)

    def last_rows(tail, tile_rows, taps, width):
        if short:
            t = tail.reshape(nseq, seq_len, width)
        else:
            per_seq = seq_len // tile_rows
            t = tail.reshape(nseq, per_seq, SUBLANES, width)[:, per_seq - 1]
        return t[:, t.shape[1] - (taps - 1):]

    sc_new = last_rows(u_tail, tm_mix, SC_CONV, SC_DIM)
    ffn_new = last_rows(g_tail, tm_ffn, FFN_CONV, D_FF)
    ssm_new = ssm_new.reshape(nseq, SSM_HEADS, SSM_HEAD_DIM, SSM_STATE)
    return y.reshape(nseq, seq_len, D_MODEL), ssm_new, xbc_new, sc_new, ffn_new


def _prep_weights(norm_mix, w_in, ssm_conv_w, ssm_conv_b, ssm_dt_bias, ssm_a_log, ssm_d, ssm_norm,
                  w_ssm_out, sc_conv_w, w_sc_out, w_o, norm_ffn, w_ffn_gate, w_ffn_up,
                  ffn_conv_w, ffn_conv_b, w_ffn_down, norm_ple, w_ple, w_ple_gate, norm_final):
    row = lambda v: v.reshape(1, -1).astype(F32)
    pad_heads = lambda v: jnp.pad(v.astype(F32), (0, DT_PAD - SSM_HEADS)).reshape(1, DT_PAD)
    return {
        "norm_mix": row(norm_mix),
        "w_in_t": w_in.T,
        "cw_xs": ssm_conv_w[:, :D_INNER].astype(F32),
        "cb_xs": row(ssm_conv_b[:D_INNER]),
        "cw_bc": ssm_conv_w[:, D_INNER:].astype(F32),
        "cb_bc": row(ssm_conv_b[D_INNER:]),
        "dt_bias": pad_heads(ssm_dt_bias),
        "a_neg": pad_heads(-jnp.exp(ssm_a_log.astype(F32))),
        "d_exp": row(jnp.repeat(ssm_d, SSM_HEAD_DIM)),
        "ssm_norm": row(ssm_norm),
        "w_ssm_out": w_ssm_out.astype(BF16),
        "sc_conv_w": sc_conv_w.astype(F32),
        "w_sc_out": w_sc_out.astype(BF16),
        "w_o": w_o.astype(BF16),
        "norm_ffn": row(norm_ffn),
        "w_ffn_gate": w_ffn_gate,
        "w_ffn_up": w_ffn_up,
        "ffn_conv_w": ffn_conv_w.astype(F32),
        "ffn_conv_b": row(ffn_conv_b),
        "w_ffn_down": w_ffn_down.astype(BF16),
        "norm_ple": row(norm_ple),
        "w_ple": w_ple.astype(BF16),
        "w_ple_gate": w_ple_gate.astype(BF16),
        "norm_final": row(norm_final),
    }


def kernel(x_prompt, x_sample, state_ssm, state_ssm_conv, state_short_conv, state_ffn_conv, p_prompt, p_sample, norm_mix, w_in, ssm_conv_w, ssm_conv_b, ssm_dt_bias, ssm_a_log, ssm_d, ssm_norm, w_ssm_out, sc_conv_w, w_sc_out, w_o, norm_ffn, w_ffn_gate, w_ffn_up, ffn_conv_w, ffn_conv_b, w_ffn_down, norm_ple, w_ple, w_ple_gate, norm_final):
    depth = w_in.shape[0]
    assert depth == 1, "the final norm is fused into the layer's last kernel"
    assert x_sample.shape[1] == DEC_SEQ
    layer_w = (norm_mix, w_in, ssm_conv_w, ssm_conv_b, ssm_dt_bias, ssm_a_log, ssm_d, ssm_norm,
               w_ssm_out, sc_conv_w, w_sc_out, w_o, norm_ffn, w_ffn_gate, w_ffn_up,
               ffn_conv_w, ffn_conv_b, w_ffn_down, norm_ple, w_ple, w_ple_gate)
    w = _prep_weights(*[t[0] for t in layer_w], norm_final)
    yp, *prompt_states = _layer(x_prompt, p_prompt[0], w, short=False)
    ys, *sample_states = _layer(
        x_sample, p_sample[0], w, short=True,
        states=(state_ssm[0], state_ssm_conv[0], state_short_conv[0], state_ffn_conv[0]))
    outs = [yp, ys]
    for sp, ss in zip(prompt_states, sample_states):
        outs += [sp[None], ss[None]]
    return tuple(outs)
```

```python
import functools

import numpy as np
import jax
import jax.numpy as jnp
from jax import lax
from jax.experimental import pallas as pl
from jax.experimental.pallas import tpu as pltpu

F32 = jnp.float32
BF16 = jnp.bfloat16

D_MODEL = 2048
D_INNER = 4096
SSM_HEADS = 64
SSM_HEAD_DIM = 64
SSM_GROUPS = 8
HEADS_PER_GROUP = SSM_HEADS // SSM_GROUPS
GROUP_WIDTH = D_INNER // SSM_GROUPS
SSM_STATE = 128
SSM_CHUNK = 128
SSM_CONV = 4
BC_DIM = 2 * SSM_GROUPS * SSM_STATE
XBC_DIM = D_INNER + BC_DIM
SC_DIM = D_MODEL
SC_CONV = 3
D_FF = 5632
FFN_CONV = 3
PLE_DIM = 256
EPS = 1e-6
DEC_SEQ = 8
PROJ_DIM = D_INNER + XBC_DIM + 5 * D_MODEL
DT_PAD = 128

SUBLANES = 8
LOG2E = 1.4426950408889634
VMEM_LIMIT = 52 * 1024 * 1024


def _cparams(sem):
    return pltpu.CompilerParams(dimension_semantics=sem, vmem_limit_bytes=VMEM_LIMIT)


def _dot(a, b):
    return jnp.dot(a, b, preferred_element_type=F32)


def _dot_nt(a, b):
    return lax.dot_general(a, b, (((1,), (1,)), ((), ())), preferred_element_type=F32)


def _sigmoid(x):
    return 0.5 * jnp.tanh(0.5 * x) + 0.5


def _silu(x):
    h = 0.5 * x
    return h * jnp.tanh(h) + h


def _softplus(x):
    return jnp.maximum(x, 0.0) + jnp.log(1.0 + jnp.exp(-jnp.abs(x)))


def _rms(x, g):
    ms = jnp.mean(x * x, axis=-1, keepdims=True)
    return x * lax.rsqrt(ms + EPS) * g


def _split_bf16(a, terms):
    parts = []
    r = a
    for _ in range(terms - 1):
        p = r.astype(BF16)
        parts.append(p)
        r = r - p.astype(F32)
    parts.append(r.astype(BF16))
    return parts


def _dot_sel_right(a, sel, terms):
    out = None
    for p in _split_bf16(a, terms):
        d = _dot(p, sel)
        out = d if out is None else out + d
    return out


def _dot_sel_left(sel, a, terms):
    out = None
    for p in _split_bf16(a, terms):
        d = _dot(sel, p)
        out = d if out is None else out + d
    return out


def _shift_long(x, halo, k):
    r = pltpu.roll(x, k, 0)
    h = pltpu.roll(halo, k, 0)
    row = lax.broadcasted_iota(jnp.int32, halo.shape, 0)
    first = jnp.where(row < k, h, r[:SUBLANES])
    return jnp.concatenate([first, r[SUBLANES:]], axis=0)


def _shift_short(x, past, k, taps):
    n = x.shape[0]
    r = pltpu.roll(x, k, 0)
    s = (n + k - (taps - 1)) % n
    p = past if s == 0 else pltpu.roll(past, s, 0)
    t = lax.broadcasted_iota(jnp.int32, x.shape, 0) & (DEC_SEQ - 1)
    return jnp.where(t < k, p, r)


def _past_rows(state):
    n, k, c = state.shape
    t = lax.broadcasted_iota(jnp.int32, (n, DEC_SEQ, c), 1)
    out = jnp.zeros((n, DEC_SEQ, c), F32)
    for r in range(k):
        out = jnp.where(t == r, jnp.broadcast_to(state[:, r:r + 1, :], (n, DEC_SEQ, c)), out)
    return out.reshape(n * DEC_SEQ, c)


def _seq_tails(x, taps):
    rows, width = x.shape
    return x.reshape(rows // DEC_SEQ, DEC_SEQ, width)[:, DEC_SEQ - (taps - 1):, :]


def _causal_conv(x, w, shift):
    taps = w.shape[0]
    y = w[taps - 1:taps] * x
    for k in range(1, taps):
        y = y + w[taps - 1 - k:taps - k] * shift(k)
    return y


def _in_proj_kernel(x_ref, g_ref, w_ref, wdt_ref, proj_ref, dt_ref, n1_scr, *, sub):
    @pl.when(pl.program_id(1) == 0)
    def _():
        n1 = _rms(x_ref[...], g_ref[...]).astype(BF16)
        n1_scr[...] = n1
        dt_ref[...] = _dot_nt(n1, wdt_ref[...].astype(BF16))

    n1 = n1_scr[...]
    for c in range(w_ref.shape[0] // sub):
        rs = slice(c * sub, (c + 1) * sub)
        proj_ref[:, rs] = _dot_nt(n1, w_ref[rs, :].astype(BF16)).astype(BF16)


def _in_proj(x, g, w_t, tm, tn):
    m = x.shape[0]
    dt_lo = D_INNER + XBC_DIM
    dt_hi = dt_lo + SSM_HEADS
    n_lo = dt_lo // tn
    w_row = lambda i, j: (
        pl.multiple_of(jnp.where(j < n_lo, j * tn, dt_hi + (j - n_lo) * tn), SSM_HEADS), 0)
    return pl.pallas_call(
        functools.partial(_in_proj_kernel, sub=256),
        grid=(m // tm, PROJ_DIM // tn),
        in_specs=[
            pl.BlockSpec((tm, D_MODEL), lambda i, j: (i, 0), pipeline_mode=pl.Buffered(1)),
            pl.BlockSpec((1, D_MODEL), lambda i, j: (0, 0)),
            pl.BlockSpec((pl.Element(tn), pl.Element(D_MODEL)), w_row),
            pl.BlockSpec((DT_PAD, D_MODEL), lambda i, j: (dt_lo // DT_PAD, 0)),
        ],
        out_specs=[
            pl.BlockSpec((tm, tn), lambda i, j: (i, j)),
            pl.BlockSpec((tm, DT_PAD), lambda i, j: (i, 0)),
        ],
        out_shape=[
            jax.ShapeDtypeStruct((m, PROJ_DIM), BF16),
            jax.ShapeDtypeStruct((m, DT_PAD), F32),
        ],
        scratch_shapes=[pltpu.VMEM((tm, D_MODEL), BF16)],
        compiler_params=_cparams(("parallel", "arbitrary")),
        name="in_proj",
    )(x, g, w_t, w_t)


def _ssd_consts(q, short):
    l = np.arange(q)[:, None]
    s = np.arange(q)[None, :]
    if short:
        tri = (l // DEC_SEQ == s // DEC_SEQ) & (s <= l)
        end = s == (l // DEC_SEQ) * DEC_SEQ + DEC_SEQ - 1
    else:
        tri = s <= l
        end = np.broadcast_to(s == q - 1, (q, q))
    mask = np.where(tri, 0.0, -np.inf).astype(np.float32)
    h = np.arange(128)[None, :, None]
    g = np.arange(SSM_GROUPS)[:, None, None]
    c = np.arange(GROUP_WIDTH)[None, None, :]
    expand = h == g * HEADS_PER_GROUP + c // SSM_HEAD_DIM
    return (jnp.asarray(tri, BF16), jnp.asarray(end, BF16), jnp.asarray(mask),
            jnp.asarray(expand, BF16))


def _ssd_kernel(*refs, short, q, seqs):
    z_ref, xs_ref, bc_ref, dt_ref, tri_ref, end_ref, mask_ref, exp_ref = refs[:8]
    refs = refs[8:]
    cwx_ref, cbx_ref, cwb_ref, cbb_ref, dtb_ref, a_ref, dexp_ref, gn_ref = refs[:8]
    refs = refs[8:]
    past_xs_ref = past_bc_ref = tail_xs = tail_bc = None
    if short:
        past_xs_ref, past_bc_ref, h0_ref = refs[:3]
        refs = refs[3:]
    y_ref, ht_ref = refs[:2]
    refs = refs[2:]
    if short:
        xtail_ref = refs[0]
        refs = refs[1:]
    xs_scr, b_scr, c_scr, y_scr, ea_scr, we_scr, aend_scr, acumt_scr, arow_scr = refs[:9]
    refs = refs[9:]
    if short:
        eae_scr, xwt_scr = refs
    else:
        h_scr, tail_xs, tail_bc = refs

    step = pl.program_id(1)
    nstep = pl.num_programs(1)

    def conv_block(src_ref, past_ref, tail, cw, cb_ref, cs):
        xb = src_ref[:, cs]
        x = xb.astype(F32)
        if short:
            past = _past_rows(past_ref[:, :, cs])
            shift = lambda k: _shift_short(x, past, k, SSM_CONV)
        else:
            halo = tail[:, cs]
            shift = lambda k: _shift_long(x, halo, k)
            tail[:, cs] = x[q - SUBLANES:]
        return _silu(cb_ref[:, cs] + _causal_conv(x, cw[:, cs], shift))

    def conv_group(g):
        cs = slice(g * GROUP_WIDTH, (g + 1) * GROUP_WIDTH)
        bs = slice(g * SSM_STATE, (g + 1) * SSM_STATE)
        cc = slice((SSM_GROUPS + g) * SSM_STATE, (SSM_GROUPS + g + 1) * SSM_STATE)
        cwx = cwx_ref[...]
        cwb = cwb_ref[...]
        return (conv_block(xs_ref, past_xs_ref, tail_xs, cwx, cbx_ref, cs),
                conv_block(bc_ref, past_bc_ref, tail_bc, cwb, cbb_ref, bs),
                conv_block(bc_ref, past_bc_ref, tail_bc, cwb, cbb_ref, cc))

    def dt_prelude():
        dt = _softplus(dt_ref[...] + dtb_ref[...])
        acum = _dot_sel_left(tri_ref[...], dt * a_ref[...], 3)
        aend = _dot_sel_left(end_ref[...], acum, 3)
        acumt = acum.T
        acumt_scr[...] = acumt * LOG2E
        arow_scr[...] = (acumt - jnp.log(dt.T)) * LOG2E
        aend_scr[...] = aend
        ea_scr[...] = jnp.exp(acum)
        we_scr[...] = jnp.exp(aend - acum) * dt

    def diag_group(g, xs, bg, cg):
        cb = _dot_nt(cg.astype(BF16), bg.astype(BF16))
        mask = mask_ref[...]
        lane = lax.broadcasted_iota(jnp.int32, (q, 128), 1)
        ys = []
        for pair in range(HEADS_PER_GROUP // 2):
            ws = []
            for half in range(2):
                h = g * HEADS_PER_GROUP + 2 * pair + half
                colb = jnp.broadcast_to(acumt_scr[pl.ds(h, 1), :], (q, q)).T
                seg = colb - arow_scr[pl.ds(h, 1), :] + mask
                ws.append((cb * jnp.exp2(seg)).astype(BF16))
            xp = xs[:, pair * 128:(pair + 1) * 128]
            top = jnp.where(lane < SSM_HEAD_DIM, xp, 0.0).astype(BF16)
            bot = jnp.where(lane >= SSM_HEAD_DIM, xp, 0.0).astype(BF16)
            ys.append(_dot(jnp.concatenate(ws, axis=1), jnp.concatenate([top, bot], axis=0)))
        return jnp.concatenate(ys, axis=1)

    def finish_group(g, y, xs):
        cs = slice(g * GROUP_WIDTH, (g + 1) * GROUP_WIDTH)
        y = y + dexp_ref[:, cs] * xs
        y = y * _silu(z_ref[:, cs].astype(F32))
        ms = jnp.mean(y * y, axis=-1, keepdims=True)
        y_ref[:, cs] = (y * lax.rsqrt(ms + EPS) * gn_ref[:, cs]).astype(BF16)

    if not short:
        @pl.when(step == 0)
        def _():
            h_scr[...] = jnp.zeros_like(h_scr)
            tail_xs[...] = jnp.zeros_like(tail_xs)
            tail_bc[...] = jnp.zeros_like(tail_bc)

        dt_prelude()
        for g in range(SSM_GROUPS):
            xs_scr[g], b_scr[g], c_scr[g] = conv_group(g)
        for g in range(SSM_GROUPS):
            xs, bg, cg = xs_scr[g], b_scr[g], c_scr[g]
            expand = exp_ref[g]
            y_diag = diag_group(g, xs, bg, cg)
            hg = h_scr[g]
            y_off = _dot(cg.astype(BF16), hg.astype(BF16))
            y_scr[g] = y_diag + y_off * _dot_sel_right(ea_scr[...], expand, 2)
            xw = (xs * _dot_sel_right(we_scr[...], expand, 2)).astype(BF16)
            st = _dot(bg.T.astype(BF16), xw)
            dec = jnp.exp(_dot_sel_right(aend_scr[:SUBLANES, :], expand, 3))[:1]
            h_scr[g] = dec * hg + st
        for g in range(SSM_GROUPS):
            finish_group(g, y_scr[g], xs_scr[g])

        @pl.when(step == nstep - 1)
        def _():
            for g in range(SSM_GROUPS):
                ht_ref[0, g * GROUP_WIDTH:(g + 1) * GROUP_WIDTH, :] = h_scr[g].T
    else:
        @pl.when(step == 0)
        def _():
            xtail_ref[:, :, :D_INNER] = _seq_tails(xs_ref[...].astype(F32), SSM_CONV)
            xtail_ref[:, :, D_INNER:] = _seq_tails(bc_ref[...].astype(F32), SSM_CONV)
            dt_prelude()
            for g in range(SSM_GROUPS):
                xs, bg, cg = conv_group(g)
                expand = exp_ref[g]
                xs_scr[g] = xs
                b_scr[g] = bg
                c_scr[g] = cg
                y_scr[g] = diag_group(g, xs, bg, cg)
                eae_scr[g] = _dot_sel_right(ea_scr[...], expand, 2)
                xwt_scr[g] = (xs * _dot_sel_right(we_scr[...], expand, 2)).T.astype(BF16)

        for u in range(seqs):
            sidx = step * seqs + u
            row = lax.broadcasted_iota(jnp.int32, (q, 128), 0)
            onehot = jnp.where(row == sidx * DEC_SEQ + DEC_SEQ - 1, 1.0, 0.0).astype(BF16)
            cdec = jnp.exp2(_dot_sel_right(acumt_scr[...], onehot, 3))
            in_seq = (row // DEC_SEQ) == sidx
            seq_rows = pl.ds(pl.multiple_of(sidx * DEC_SEQ, DEC_SEQ), DEC_SEQ)
            for g in range(SSM_GROUPS):
                rows = slice(g * GROUP_WIDTH, (g + 1) * GROUP_WIDTH)
                hg = h0_ref[u, rows, :]
                cj = c_scr[g, seq_rows, :].astype(BF16)
                y_off = _dot_nt(cj, hg.astype(BF16))
                y_scr[g, seq_rows, :] = y_scr[g, seq_rows, :] + y_off * eae_scr[g, seq_rows, :]
                bj = jnp.where(in_seq, b_scr[g], 0.0).astype(BF16)
                st = _dot(xwt_scr[g], bj)
                dec = [jnp.broadcast_to(cdec[g * HEADS_PER_GROUP + r:g * HEADS_PER_GROUP + r + 1],
                                        (SSM_HEAD_DIM, 128)) for r in range(HEADS_PER_GROUP)]
                ht_ref[u, rows, :] = jnp.concatenate(dec, axis=0) * hg + st

        @pl.when(step == nstep - 1)
        def _():
            for g in range(SSM_GROUPS):
                finish_group(g, y_scr[g], xs_scr[g])


def _ssd(proj, dt_raw, consts, wts, nseq, short, past=None, h0=None):
    m = proj.shape[0]
    q = SSM_CHUNK
    nblk = m // q
    if short:
        seqs = 2
        steps = q // DEC_SEQ // seqs
        grid = (nblk, steps)
        blk = lambda a, b: a
        st_idx = lambda a, b: (a * steps + b, 0, 0)
    else:
        seqs = 1
        steps = m // nseq // q
        grid = (nseq, steps)
        blk = lambda a, b: a * steps + b
        st_idx = lambda a, b: (a, 0, 0)
    const2 = lambda shape: pl.BlockSpec(shape, lambda a, b: (0, 0))
    const3 = lambda shape: pl.BlockSpec(shape, lambda a, b: (0, 0, 0))
    in_specs = [
        pl.BlockSpec((q, D_INNER), lambda a, b: (blk(a, b), 0)),
        pl.BlockSpec((q, D_INNER), lambda a, b: (blk(a, b), 1)),
        pl.BlockSpec((q, BC_DIM), lambda a, b: (blk(a, b), 2 * D_INNER // BC_DIM)),
        pl.BlockSpec((q, DT_PAD), lambda a, b: (blk(a, b), 0)),
        const2((q, q)), const2((q, q)), const2((q, q)),
        const3((SSM_GROUPS, 128, GROUP_WIDTH)),
        const2((SSM_CONV, D_INNER)), const2((1, D_INNER)),
        const2((SSM_CONV, BC_DIM)), const2((1, BC_DIM)),
        const2((1, DT_PAD)), const2((1, DT_PAD)),
        const2((1, D_INNER)), const2((1, D_INNER)),
    ]
    args = [proj, proj, proj, dt_raw, *consts, *wts]
    state_spec = pl.BlockSpec((seqs, D_INNER, SSM_STATE), st_idx)
    scratch = [
        pltpu.VMEM((SSM_GROUPS, q, GROUP_WIDTH), F32),
        pltpu.VMEM((SSM_GROUPS, q, SSM_STATE), F32),
        pltpu.VMEM((SSM_GROUPS, q, SSM_STATE), F32),
        pltpu.VMEM((SSM_GROUPS, q, GROUP_WIDTH), F32),
        pltpu.VMEM((q, 128), F32), pltpu.VMEM((q, 128), F32), pltpu.VMEM((q, 128), F32),
        pltpu.VMEM((128, q), F32), pltpu.VMEM((128, q), F32),
    ]
    out_specs = [pl.BlockSpec((q, D_INNER), lambda a, b: (blk(a, b), 0)), state_spec]
    out_shape = [jax.ShapeDtypeStruct((m, D_INNER), BF16),
                 jax.ShapeDtypeStruct((nseq, D_INNER, SSM_STATE), F32)]
    if short:
        per_blk = q // DEC_SEQ
        in_specs += [
            pl.BlockSpec((per_blk, SSM_CONV - 1, D_INNER), lambda a, b: (a, 0, 0)),
            pl.BlockSpec((per_blk, SSM_CONV - 1, BC_DIM), lambda a, b: (a, 0, D_INNER // BC_DIM)),
            state_spec,
        ]
        args += [past, past, h0]
        out_specs.append(pl.BlockSpec((per_blk, SSM_CONV - 1, XBC_DIM), lambda a, b: (a, 0, 0)))
        out_shape.append(jax.ShapeDtypeStruct((nseq, SSM_CONV - 1, XBC_DIM), F32))
        scratch += [pltpu.VMEM((SSM_GROUPS, q, GROUP_WIDTH), F32),
                    pltpu.VMEM((SSM_GROUPS, GROUP_WIDTH, q), BF16)]
    else:
        scratch += [pltpu.VMEM((SSM_GROUPS, SSM_STATE, GROUP_WIDTH), F32),
                    pltpu.VMEM((SUBLANES, D_INNER), F32),
                    pltpu.VMEM((SUBLANES, BC_DIM), F32)]
    return pl.pallas_call(
        functools.partial(_ssd_kernel, short=short, q=q, seqs=seqs),
        grid=grid,
        in_specs=in_specs,
        out_specs=out_specs,
        out_shape=out_shape,
        scratch_shapes=scratch,
        compiler_params=_cparams(("arbitrary", "arbitrary")),
        name="ssd_short" if short else "ssd_long",
    )(*args)


def _mix_kernel(*refs, short, tiles_per_seq):
    yn_ref, scb_ref, scc_ref, sch_ref, ga_ref, gb_ref, cw_ref, wa_ref, wb_ref = refs[:9]
    refs = refs[9:]
    if short:
        past_ref = refs[0]
        refs = refs[1:]
    mixed_ref, utail_ref = refs[:2]
    tm = yn_ref.shape[0]

    u = scc_ref[...].astype(F32) * sch_ref[...].astype(F32)
    if short:
        past = _past_rows(past_ref[...])
        shift = lambda k: _shift_short(u, past, k, SC_CONV)
        utail_ref[...] = _seq_tails(u, SC_CONV)
    else:
        halo_scr = refs[2]
        first = pl.program_id(0) % tiles_per_seq == 0
        halo = jnp.where(first, 0.0, halo_scr[...])
        shift = lambda k: _shift_long(u, halo, k)
        halo_scr[...] = u[tm - SUBLANES:]
        utail_ref[...] = u[tm - SUBLANES:]
    s = (scb_ref[...].astype(F32) * _causal_conv(u, cw_ref[...], shift)).astype(BF16)

    a = _dot(yn_ref[...], wa_ref[...])
    b = _dot(s, wb_ref[...])
    mixed = _sigmoid(ga_ref[...].astype(F32)) * a + _sigmoid(gb_ref[...].astype(F32)) * b
    mixed_ref[...] = mixed.astype(BF16)


def _mix(yn, proj, cw, wa, wb, seq_len, tm, short, past=None):
    m = yn.shape[0]
    col = lambda c: (D_INNER + XBC_DIM) // D_MODEL + c
    rows = lambda c: pl.BlockSpec((tm, D_MODEL), lambda i: (i, col(c)))
    resident = lambda shape: pl.BlockSpec(shape, lambda i: (0, 0), pipeline_mode=pl.Buffered(1))
    in_specs = [
        pl.BlockSpec((tm, D_INNER), lambda i: (i, 0)),
        rows(0), rows(1), rows(2), rows(3), rows(4),
        resident((SC_CONV, SC_DIM)),
        resident((D_INNER, D_MODEL)),
        resident((SC_DIM, D_MODEL)),
    ]
    args = [yn, proj, proj, proj, proj, proj, cw, wa, wb]
    scratch = []
    if short:
        seq_blk = (tm // DEC_SEQ, SC_CONV - 1, SC_DIM)
        in_specs.append(pl.BlockSpec(seq_blk, lambda i: (i, 0, 0)))
        args.append(past)
        tail_spec = pl.BlockSpec(seq_blk, lambda i: (i, 0, 0))
        tail_shape = (m // DEC_SEQ, SC_CONV - 1, SC_DIM)
    else:
        scratch.append(pltpu.VMEM((SUBLANES, SC_DIM), F32))
        tail_spec = pl.BlockSpec((SUBLANES, SC_DIM), lambda i: (i, 0))
        tail_shape = (m // tm * SUBLANES, SC_DIM)
    return pl.pallas_call(
        functools.partial(_mix_kernel, short=short, tiles_per_seq=max(seq_len // tm, 1)),
        grid=(m // tm,),
        in_specs=in_specs,
        out_specs=[pl.BlockSpec((tm, D_MODEL), lambda i: (i, 0)), tail_spec],
        out_shape=[jax.ShapeDtypeStruct((m, D_MODEL), BF16),
                   jax.ShapeDtypeStruct(tail_shape, F32)],
        scratch_shapes=scratch,
        compiler_params=_cparams(("arbitrary",)),
        name="mix",
    )(*args)


def _row_halves(ref):
    half = ref.shape[0] // 2
    return [slice(0, half), slice(half, 2 * half)]


def _out_proj_kernel(mixed_ref, x_ref, wo_ref, g_ref, x1_ref, n2_ref):
    for rs in _row_halves(x_ref):
        x1 = x_ref[rs, :] + _dot(mixed_ref[rs, :], wo_ref[...])
        x1_ref[rs, :] = x1
        n2_ref[rs, :] = _rms(x1, g_ref[...]).astype(BF16)


def _out_proj(mixed, x, wo, g, tm):
    m = x.shape[0]
    row = lambda i: (i, 0)
    const = lambda i: (0, 0)
    return pl.pallas_call(
        _out_proj_kernel,
        grid=(m // tm,),
        in_specs=[pl.BlockSpec((tm, D_MODEL), row), pl.BlockSpec((tm, D_MODEL), row),
                  pl.BlockSpec((D_MODEL, D_MODEL), const, pipeline_mode=pl.Buffered(1)),
                  pl.BlockSpec((1, D_MODEL), const)],
        out_specs=[pl.BlockSpec((tm, D_MODEL), row), pl.BlockSpec((tm, D_MODEL), row)],
        out_shape=[jax.ShapeDtypeStruct((m, D_MODEL), F32),
                   jax.ShapeDtypeStruct((m, D_MODEL), BF16)],
        compiler_params=_cparams(("parallel",)),
        name="out_proj",
    )(mixed, x, wo, g)


def _ffn_up_kernel(*refs, short, tiles_per_seq, splits):
    n2_ref, wg_ref, wu_ref, cw_ref, cb_ref = refs[:5]
    refs = refs[5:]
    if short:
        past_ref = refs[0]
        refs = refs[1:]
    act_ref, gtail_ref = refs[:2]
    tm = n2_ref.shape[0]
    n2 = n2_ref[...]
    j = pl.program_id(1)
    first = pl.program_id(0) % tiles_per_seq == 0
    lo = 0
    for width in splits:
        cs = slice(lo, lo + width)
        lo += width
        gate = _dot(n2, wg_ref[:, cs].astype(BF16))
        up = _dot(n2, wu_ref[:, cs].astype(BF16))
        if short:
            past = _past_rows(past_ref[:, :, cs])
            shift = lambda k, gate=gate, past=past: _shift_short(gate, past, k, FFN_CONV)
            gtail_ref[:, :, cs] = _seq_tails(gate, FFN_CONV)
        else:
            halo_scr = refs[2]
            halo = jnp.where(first, 0.0, halo_scr[j, :, cs])
            shift = lambda k, gate=gate, halo=halo: _shift_long(gate, halo, k)
            halo_scr[j, :, cs] = gate[tm - SUBLANES:]
            gtail_ref[:, cs] = gate[tm - SUBLANES:]
        conv = cb_ref[:, cs] + _causal_conv(gate, cw_ref[:, cs], shift)
        act_ref[:, cs] = (_silu(conv) * up).astype(BF16)


def _ffn_up(n2, wg, wu, cw, cb, seq_len, tm, tn, splits, short, past=None):
    m = n2.shape[0]
    nj = D_FF // tn
    assert sum(splits) == tn
    in_specs = [
        pl.BlockSpec((tm, D_MODEL), lambda i, j: (i, 0)),
        pl.BlockSpec((D_MODEL, tn), lambda i, j: (0, j)),
        pl.BlockSpec((D_MODEL, tn), lambda i, j: (0, j)),
        pl.BlockSpec((FFN_CONV, tn), lambda i, j: (0, j)),
        pl.BlockSpec((1, tn), lambda i, j: (0, j)),
    ]
    args = [n2, wg, wu, cw, cb]
    scratch = []
    if short:
        tail_spec = pl.BlockSpec((tm // DEC_SEQ, FFN_CONV - 1, tn), lambda i, j: (i, 0, j))
        in_specs.append(tail_spec)
        args.append(past)
        tail_shape = (m // DEC_SEQ, FFN_CONV - 1, D_FF)
    else:
        scratch.append(pltpu.VMEM((nj, SUBLANES, tn), F32))
        tail_spec = pl.BlockSpec((SUBLANES, tn), lambda i, j: (i, j))
        tail_shape = (m // tm * SUBLANES, D_FF)
    return pl.pallas_call(
        functools.partial(_ffn_up_kernel, short=short, tiles_per_seq=max(seq_len // tm, 1),
                          splits=splits),
        grid=(m // tm, nj),
        in_specs=in_specs,
        out_specs=[pl.BlockSpec((tm, tn), lambda i, j: (i, j)), tail_spec],
        out_shape=[jax.ShapeDtypeStruct((m, D_FF), BF16),
                   jax.ShapeDtypeStruct(tail_shape, F32)],
        scratch_shapes=scratch,
        compiler_params=_cparams(("arbitrary", "arbitrary")),
        name="ffn_up",
    )(*args)


def _ffn_down_kernel(act_ref, wd_ref, x1_ref, x2_ref):
    x2_ref[...] = x1_ref[...] + _dot(act_ref[...], wd_ref[...])


def _ffn_down(act, wd, x1, tm):
    m = x1.shape[0]
    return pl.pallas_call(
        _ffn_down_kernel,
        grid=(m // tm,),
        in_specs=[pl.BlockSpec((tm, D_FF), lambda i: (i, 0)),
                  pl.BlockSpec((D_FF, D_MODEL), lambda i: (0, 0), pipeline_mode=pl.Buffered(1)),
                  pl.BlockSpec((tm, D_MODEL), lambda i: (i, 0))],
        out_specs=pl.BlockSpec((tm, D_MODEL), lambda i: (i, 0)),
        out_shape=jax.ShapeDtypeStruct((m, D_MODEL), F32),
        compiler_params=_cparams(("parallel",)),
        name="ffn_down",
    )(act, wd, x1)


def _ple_kernel(x2_ref, pe_ref, wple_ref, wgate_ref, gp_ref, gf_ref, y_ref):
    for rs in _row_halves(x2_ref):
        x2 = x2_ref[rs, :]
        n3 = _rms(x2, gp_ref[...]).astype(BF16)
        gate = _sigmoid(_dot(n3, wgate_ref[...]))
        emb = _dot(pe_ref[rs, :].astype(BF16), wple_ref[...])
        y_ref[rs, :] = _rms(x2 + emb * gate, gf_ref[...])


def _ple(x2, pe, wple, wgate, gp, gf, tm):
    m = x2.shape[0]
    row = lambda i: (i, 0)
    const = lambda i: (0, 0)
    return pl.pallas_call(
        _ple_kernel,
        grid=(m // tm,),
        in_specs=[pl.BlockSpec((tm, D_MODEL), row), pl.BlockSpec((tm, PLE_DIM), row),
                  pl.BlockSpec((PLE_DIM, D_MODEL), const),
                  pl.BlockSpec((D_MODEL, D_MODEL), const, pipeline_mode=pl.Buffered(1)),
                  pl.BlockSpec((1, D_MODEL), const), pl.BlockSpec((1, D_MODEL), const)],
        out_specs=pl.BlockSpec((tm, D_MODEL), row),
        out_shape=jax.ShapeDtypeStruct((m, D_MODEL), F32),
        compiler_params=_cparams(("parallel",)),
        name="ple",
    )(x2, pe, wple, wgate, gp, gf)


def _tile(m, want):
    return min(m, want)


def _layer(x, pe, w, short, states=None):
    nseq, seq_len, _ = x.shape
    m = nseq * seq_len
    x2d = x.reshape(m, D_MODEL)
    pe2d = pe.reshape(m, PLE_DIM)
    if short:
        ssm0, xbc0, sc0, ffn0 = states
        h0 = ssm0.reshape(nseq, D_INNER, SSM_STATE)

    proj, dt_raw = _in_proj(x2d, w["norm_mix"], w["w_in_t"], _tile(m, 1024), 1024)

    ssd_w = (w["cw_xs"], w["cb_xs"], w["cw_bc"], w["cb_bc"], w["dt_bias"], w["a_neg"],
             w["d_exp"], w["ssm_norm"])
    consts = _ssd_consts(SSM_CHUNK, short)
    if short:
        yn, ssm_new, xbc_new = _ssd(proj, dt_raw, consts, ssd_w, nseq, True, xbc0, h0)
    else:
        yn, ssm_new = _ssd(proj, dt_raw, consts, ssd_w, nseq, False)
        proj3 = proj.reshape(nseq, seq_len, PROJ_DIM)
        xbc_new = proj3[:, seq_len - (SSM_CONV - 1):, D_INNER:D_INNER + XBC_DIM].astype(F32)

    tm = _tile(m, 512)
    tm_mix = _tile(m, 256)
    mixed, u_tail = _mix(yn, proj, w["sc_conv_w"], w["w_ssm_out"], w["w_sc_out"], seq_len,
                         tm_mix, short, sc0 if short else None)
    x1, n2 = _out_proj(mixed, x2d, w["w_o"], w["norm_ffn"], tm)

    tm_ffn = _tile(m, 1024)
    act, g_tail = _ffn_up(n2, w["w_ffn_gate"], w["w_ffn_up"], w["ffn_conv_w"], w["ffn_conv_b"],
                          seq_len, tm_ffn, 512, (256, 256), short, ffn0 if short else None)
    x2 = _ffn_down(act, w["w_ffn_down"], x1, _tile(m, 256))
    y = _ple(x2, pe2d, w["w_ple"], w["w_ple_gate"], w["norm_ple"], w["norm_final"], tm)

    def last_rows(tail, tile_rows, taps, width):
        if short:
            return tail
        per_seq = seq_len // tile_rows
        t = tail.reshape(nseq, per_seq, SUBLANES, width)[:, per_seq - 1]
        return t[:, SUBLANES - (taps - 1):]

    sc_new = last_rows(u_tail, tm_mix, SC_CONV, SC_DIM)
    ffn_new = last_rows(g_tail, tm_ffn, FFN_CONV, D_FF)
    ssm_new = ssm_new.reshape(nseq, SSM_HEADS, SSM_HEAD_DIM, SSM_STATE)
    return y.reshape(nseq, seq_len, D_MODEL), ssm_new, xbc_new, sc_new, ffn_new


def _prep_weights(norm_mix, w_in, ssm_conv_w, ssm_conv_b, ssm_dt_bias, ssm_a_log, ssm_d, ssm_norm,
                  w_ssm_out, sc_conv_w, w_sc_out, w_o, norm_ffn, w_ffn_gate, w_ffn_up,
                  ffn_conv_w, ffn_conv_b, w_ffn_down, norm_ple, w_ple, w_ple_gate, norm_final):
    row = lambda v: v.reshape(1, -1).astype(F32)
    pad_heads = lambda v: jnp.pad(v.astype(F32), (0, DT_PAD - SSM_HEADS)).reshape(1, DT_PAD)
    return {
        "norm_mix": row(norm_mix),
        "w_in_t": w_in.T,
        "cw_xs": ssm_conv_w[:, :D_INNER].astype(F32),
        "cb_xs": row(ssm_conv_b[:D_INNER]),
        "cw_bc": ssm_conv_w[:, D_INNER:].astype(F32),
        "cb_bc": row(ssm_conv_b[D_INNER:]),
        "dt_bias": pad_heads(ssm_dt_bias),
        "a_neg": pad_heads(-jnp.exp(ssm_a_log.astype(F32))),
        "d_exp": row(jnp.repeat(ssm_d, SSM_HEAD_DIM)),
        "ssm_norm": row(ssm_norm),
        "w_ssm_out": w_ssm_out.astype(BF16),
        "sc_conv_w": sc_conv_w.astype(F32),
        "w_sc_out": w_sc_out.astype(BF16),
        "w_o": w_o.astype(BF16),
        "norm_ffn": row(norm_ffn),
        "w_ffn_gate": w_ffn_gate,
        "w_ffn_up": w_ffn_up,
        "ffn_conv_w": ffn_conv_w.astype(F32),
        "ffn_conv_b": row(ffn_conv_b),
        "w_ffn_down": w_ffn_down.astype(BF16),
        "norm_ple": row(norm_ple),
        "w_ple": w_ple.astype(BF16),
        "w_ple_gate": w_ple_gate.astype(BF16),
        "norm_final": row(norm_final),
    }


def kernel(x_prompt, x_sample, state_ssm, state_ssm_conv, state_short_conv, state_ffn_conv, p_prompt, p_sample, norm_mix, w_in, ssm_conv_w, ssm_conv_b, ssm_dt_bias, ssm_a_log, ssm_d, ssm_norm, w_ssm_out, sc_conv_w, w_sc_out, w_o, norm_ffn, w_ffn_gate, w_ffn_up, ffn_conv_w, ffn_conv_b, w_ffn_down, norm_ple, w_ple, w_ple_gate, norm_final):
    depth = w_in.shape[0]
    assert depth == 1, "the final norm is fused into the layer's last kernel"
    assert x_sample.shape[1] == DEC_SEQ
    layer_w = (norm_mix, w_in, ssm_conv_w, ssm_conv_b, ssm_dt_bias, ssm_a_log, ssm_d, ssm_norm,
               w_ssm_out, sc_conv_w, w_sc_out, w_o, norm_ffn, w_ffn_gate, w_ffn_up,
               ffn_conv_w, ffn_conv_b, w_ffn_down, norm_ple, w_ple, w_ple_gate)
    w = _prep_weights(*[t[0] for t in layer_w], norm_final)
    yp, *prompt_states = _layer(x_prompt, p_prompt[0], w, short=False)
    ys, *sample_states = _layer(
        x_sample, p_sample[0], w, short=True,
        states=(state_ssm[0], state_ssm_conv[0], state_short_conv[0], state_ffn_conv[0]))
    outs = [yp, ys]
    for sp, ss in zip(prompt_states, sample_states):
        outs += [sp[None], ss[None]]
    return tuple(outs)
```

```python
import functools

import numpy as np
import jax
import jax.numpy as jnp
from jax import lax
from jax.experimental import pallas as pl
from jax.experimental.pallas import tpu as pltpu

F32 = jnp.float32
BF16 = jnp.bfloat16

D_MODEL = 2048
D_INNER = 4096
SSM_HEADS = 64
SSM_HEAD_DIM = 64
SSM_GROUPS = 8
HEADS_PER_GROUP = SSM_HEADS // SSM_GROUPS
GROUP_WIDTH = D_INNER // SSM_GROUPS
SSM_STATE = 128
SSM_CHUNK = 128
SSM_CONV = 4
BC_DIM = 2 * SSM_GROUPS * SSM_STATE
XBC_DIM = D_INNER + BC_DIM
SC_DIM = D_MODEL
SC_CONV = 3
D_FF = 5632
FFN_CONV = 3
PLE_DIM = 256
EPS = 1e-6
DEC_SEQ = 8
PROJ_DIM = D_INNER + XBC_DIM + 5 * D_MODEL
DT_PAD = 128

SUBLANES = 8
STATE_SLOTS = 3
LOG2E = 1.4426950408889634
VMEM_LIMIT = 52 * 1024 * 1024


def _cparams(sem):
    return pltpu.CompilerParams(dimension_semantics=sem, vmem_limit_bytes=VMEM_LIMIT)


def _dot(a, b):
    return jnp.dot(a, b, preferred_element_type=F32)


def _dot_nt(a, b):
    return lax.dot_general(a, b, (((1,), (1,)), ((), ())), preferred_element_type=F32)


def _sigmoid(x):
    return 0.5 * jnp.tanh(0.5 * x) + 0.5


def _silu(x):
    h = 0.5 * x
    return h * jnp.tanh(h) + h


def _softplus(x):
    return jnp.maximum(x, 0.0) + jnp.log(1.0 + jnp.exp(-jnp.abs(x)))


def _rms(x, g):
    ms = jnp.mean(x * x, axis=-1, keepdims=True)
    return x * lax.rsqrt(ms + EPS) * g


def _split_bf16(a, terms):
    parts = []
    r = a
    for _ in range(terms - 1):
        p = r.astype(BF16)
        parts.append(p)
        r = r - p.astype(F32)
    parts.append(r.astype(BF16))
    return parts


def _dot_sel_right(a, sel, terms):
    out = None
    for p in _split_bf16(a, terms):
        d = _dot(p, sel)
        out = d if out is None else out + d
    return out


def _dot_sel_left(sel, a, terms):
    out = None
    for p in _split_bf16(a, terms):
        d = _dot(sel, p)
        out = d if out is None else out + d
    return out


def _shift_long(x, halo, k):
    r = pltpu.roll(x, k, 0)
    h = pltpu.roll(halo, k, 0)
    row = lax.broadcasted_iota(jnp.int32, halo.shape, 0)
    first = jnp.where(row < k, h, r[:SUBLANES])
    return jnp.concatenate([first, r[SUBLANES:]], axis=0)


def _shift_short(x, past, k, taps):
    n = x.shape[0]
    r = pltpu.roll(x, k, 0)
    s = (n + k - (taps - 1)) % n
    p = past if s == 0 else pltpu.roll(past, s, 0)
    t = lax.broadcasted_iota(jnp.int32, x.shape, 0) & (DEC_SEQ - 1)
    return jnp.where(t < k, p, r)


def _past_rows(state):
    n, k, c = state.shape
    t = lax.broadcasted_iota(jnp.int32, (n, DEC_SEQ, c), 1)
    out = jnp.zeros((n, DEC_SEQ, c), F32)
    for r in range(k):
        out = jnp.where(t == r, jnp.broadcast_to(state[:, r:r + 1, :], (n, DEC_SEQ, c)), out)
    return out.reshape(n * DEC_SEQ, c)


def _seq_tails(x, taps):
    rows, width = x.shape
    return x.reshape(rows // DEC_SEQ, DEC_SEQ, width)[:, DEC_SEQ - (taps - 1):, :]


def _causal_conv(x, w, shift):
    taps = w.shape[0]
    y = w[taps - 1:taps] * x
    for k in range(1, taps):
        y = y + w[taps - 1 - k:taps - k] * shift(k)
    return y


def _in_proj_kernel(x_ref, g_ref, w_ref, wdt_ref, proj_ref, dt_ref, n1_scr, *, sub):
    @pl.when(pl.program_id(1) == 0)
    def _():
        n1 = _rms(x_ref[...], g_ref[...]).astype(BF16)
        n1_scr[...] = n1
        dt_ref[...] = _dot_nt(n1, wdt_ref[...].astype(BF16))

    n1 = n1_scr[...]
    for c in range(w_ref.shape[0] // sub):
        rs = slice(c * sub, (c + 1) * sub)
        proj_ref[:, rs] = _dot_nt(n1, w_ref[rs, :].astype(BF16)).astype(BF16)


def _in_proj(x, g, w_t, tm, tn):
    m = x.shape[0]
    dt_lo = D_INNER + XBC_DIM
    dt_hi = dt_lo + SSM_HEADS
    n_lo = dt_lo // tn
    w_row = lambda i, j: (
        pl.multiple_of(jnp.where(j < n_lo, j * tn, dt_hi + (j - n_lo) * tn), SSM_HEADS), 0)
    return pl.pallas_call(
        functools.partial(_in_proj_kernel, sub=256),
        grid=(m // tm, PROJ_DIM // tn),
        in_specs=[
            pl.BlockSpec((tm, D_MODEL), lambda i, j: (i, 0), pipeline_mode=pl.Buffered(1)),
            pl.BlockSpec((1, D_MODEL), lambda i, j: (0, 0)),
            pl.BlockSpec((pl.Element(tn), pl.Element(D_MODEL)), w_row),
            pl.BlockSpec((DT_PAD, D_MODEL), lambda i, j: (dt_lo // DT_PAD, 0)),
        ],
        out_specs=[
            pl.BlockSpec((tm, tn), lambda i, j: (i, j)),
            pl.BlockSpec((tm, DT_PAD), lambda i, j: (i, 0)),
        ],
        out_shape=[
            jax.ShapeDtypeStruct((m, PROJ_DIM), BF16),
            jax.ShapeDtypeStruct((m, DT_PAD), F32),
        ],
        scratch_shapes=[pltpu.VMEM((tm, D_MODEL), BF16)],
        compiler_params=_cparams(("parallel", "arbitrary")),
        name="in_proj",
    )(x, g, w_t, w_t)


def _ssd_consts(q, short):
    l = np.arange(q)[:, None]
    s = np.arange(q)[None, :]
    if short:
        tri = (l // DEC_SEQ == s // DEC_SEQ) & (s <= l)
        end = s == (l // DEC_SEQ) * DEC_SEQ + DEC_SEQ - 1
    else:
        tri = s <= l
        end = np.broadcast_to(s == q - 1, (q, q))
    mask = np.where(tri, 0.0, -np.inf).astype(np.float32)
    h = np.arange(128)[None, :, None]
    g = np.arange(SSM_GROUPS)[:, None, None]
    c = np.arange(GROUP_WIDTH)[None, None, :]
    expand = h == g * HEADS_PER_GROUP + c // SSM_HEAD_DIM
    return (jnp.asarray(tri, BF16), jnp.asarray(end, BF16), jnp.asarray(mask),
            jnp.asarray(expand, BF16))


def _ssd_kernel(*refs, short, q, seqs):
    z_ref, xs_ref, bc_ref, dt_ref, tri_ref, end_ref, mask_ref, exp_ref = refs[:8]
    refs = refs[8:]
    cwx_ref, cbx_ref, cwb_ref, cbb_ref, dtb_ref, a_ref, dexp_ref, gn_ref = refs[:8]
    refs = refs[8:]
    past_xs_ref = past_bc_ref = tail_xs = tail_bc = None
    if short:
        past_xs_ref, past_bc_ref, h0_ref = refs[:3]
        refs = refs[3:]
    y_ref, ht_ref = refs[:2]
    refs = refs[2:]
    if short:
        xtail_ref = refs[0]
        refs = refs[1:]
    xs_scr, b_scr, c_scr, y_scr, ea_scr, we_scr, aend_scr, acumt_scr, arow_scr = refs[:9]
    refs = refs[9:]
    if short:
        eae_scr, xwt_scr, h_ring, h_sem = refs
    else:
        h_scr, tail_xs, tail_bc = refs

    step = pl.program_id(1)
    nstep = pl.num_programs(1)

    if short:
        t = pl.program_id(0) * nstep + step
        total = pl.num_programs(0) * nstep

        def state_copy(k):
            slot = k % STATE_SLOTS
            return pltpu.make_async_copy(h0_ref.at[pl.ds(k * seqs, seqs)], h_ring.at[slot],
                                         h_sem.at[slot])

        @pl.when(t == 0)
        def _():
            for k in range(STATE_SLOTS - 1):
                state_copy(k).start()

        @pl.when(t + STATE_SLOTS - 1 < total)
        def _():
            state_copy(t + STATE_SLOTS - 1).start()

    def conv_block(src_ref, past_ref, tail, cw, cb_ref, cs):
        xb = src_ref[:, cs]
        x = xb.astype(F32)
        if short:
            past = _past_rows(past_ref[:, :, cs])
            shift = lambda k: _shift_short(x, past, k, SSM_CONV)
        else:
            halo = tail[:, cs]
            shift = lambda k: _shift_long(x, halo, k)
            tail[:, cs] = x[q - SUBLANES:]
        return _silu(cb_ref[:, cs] + _causal_conv(x, cw[:, cs], shift))

    def conv_group(g):
        cs = slice(g * GROUP_WIDTH, (g + 1) * GROUP_WIDTH)
        bs = slice(g * SSM_STATE, (g + 1) * SSM_STATE)
        cc = slice((SSM_GROUPS + g) * SSM_STATE, (SSM_GROUPS + g + 1) * SSM_STATE)
        cwx = cwx_ref[...]
        cwb = cwb_ref[...]
        return (conv_block(xs_ref, past_xs_ref, tail_xs, cwx, cbx_ref, cs),
                conv_block(bc_ref, past_bc_ref, tail_bc, cwb, cbb_ref, bs),
                conv_block(bc_ref, past_bc_ref, tail_bc, cwb, cbb_ref, cc))

    def dt_prelude():
        dt = _softplus(dt_ref[...] + dtb_ref[...])
        acum = _dot_sel_left(tri_ref[...], dt * a_ref[...], 3)
        aend = _dot_sel_left(end_ref[...], acum, 3)
        acumt = acum.T
        acumt_scr[...] = acumt * LOG2E
        arow_scr[...] = (acumt - jnp.log(dt.T)) * LOG2E
        aend_scr[...] = aend
        ea_scr[...] = jnp.exp(acum)
        we_scr[...] = jnp.exp(aend - acum) * dt

    def diag_group(g, xs, bg, cg):
        cb = _dot_nt(cg.astype(BF16), bg.astype(BF16))
        mask = mask_ref[...]
        lane = lax.broadcasted_iota(jnp.int32, (q, 128), 1)
        ys = []
        for pair in range(HEADS_PER_GROUP // 2):
            ws = []
            for half in range(2):
                h = g * HEADS_PER_GROUP + 2 * pair + half
                colb = jnp.broadcast_to(acumt_scr[pl.ds(h, 1), :], (q, q)).T
                seg = colb - arow_scr[pl.ds(h, 1), :] + mask
                ws.append((cb * jnp.exp2(seg)).astype(BF16))
            xp = xs[:, pair * 128:(pair + 1) * 128]
            top = jnp.where(lane < SSM_HEAD_DIM, xp, 0.0).astype(BF16)
            bot = jnp.where(lane >= SSM_HEAD_DIM, xp, 0.0).astype(BF16)
            ys.append(_dot(jnp.concatenate(ws, axis=1), jnp.concatenate([top, bot], axis=0)))
        return jnp.concatenate(ys, axis=1)

    def finish_group(g, y, xs):
        cs = slice(g * GROUP_WIDTH, (g + 1) * GROUP_WIDTH)
        y = y + dexp_ref[:, cs] * xs
        y = y * _silu(z_ref[:, cs].astype(F32))
        ms = jnp.mean(y * y, axis=-1, keepdims=True)
        y_ref[:, cs] = (y * lax.rsqrt(ms + EPS) * gn_ref[:, cs]).astype(BF16)

    if not short:
        @pl.when(step == 0)
        def _():
            h_scr[...] = jnp.zeros_like(h_scr)
            tail_xs[...] = jnp.zeros_like(tail_xs)
            tail_bc[...] = jnp.zeros_like(tail_bc)

        dt_prelude()
        for g in range(SSM_GROUPS):
            xs_scr[g], b_scr[g], c_scr[g] = conv_group(g)
        for g in range(SSM_GROUPS):
            xs, bg, cg = xs_scr[g], b_scr[g], c_scr[g]
            expand = exp_ref[g]
            y_diag = diag_group(g, xs, bg, cg)
            hg = h_scr[g]
            y_off = _dot(cg.astype(BF16), hg.astype(BF16))
            y_scr[g] = y_diag + y_off * _dot_sel_right(ea_scr[...], expand, 2)
            xw = (xs * _dot_sel_right(we_scr[...], expand, 2)).astype(BF16)
            st = _dot(bg.T.astype(BF16), xw)
            dec = jnp.exp(_dot_sel_right(aend_scr[:SUBLANES, :], expand, 3))[:1]
            h_scr[g] = dec * hg + st
        for g in range(SSM_GROUPS):
            finish_group(g, y_scr[g], xs_scr[g])

        @pl.when(step == nstep - 1)
        def _():
            for g in range(SSM_GROUPS):
                ht_ref[0, g * GROUP_WIDTH:(g + 1) * GROUP_WIDTH, :] = h_scr[g].T
    else:
        @pl.when(step == 0)
        def _():
            xtail_ref[:, :, :D_INNER] = _seq_tails(xs_ref[...].astype(F32), SSM_CONV)
            xtail_ref[:, :, D_INNER:] = _seq_tails(bc_ref[...].astype(F32), SSM_CONV)
            dt_prelude()
            for g in range(SSM_GROUPS):
                xs, bg, cg = conv_group(g)
                expand = exp_ref[g]
                xs_scr[g] = xs
                b_scr[g] = bg
                c_scr[g] = cg
                y_scr[g] = diag_group(g, xs, bg, cg)
                eae_scr[g] = _dot_sel_right(ea_scr[...], expand, 2)
                xwt_scr[g] = (xs * _dot_sel_right(we_scr[...], expand, 2)).T.astype(BF16)

        state_copy(t).wait()
        h_cur = h_ring.at[t % STATE_SLOTS]
        for u in range(seqs):
            sidx = step * seqs + u
            row = lax.broadcasted_iota(jnp.int32, (q, 128), 0)
            onehot = jnp.where(row == sidx * DEC_SEQ + DEC_SEQ - 1, 1.0, 0.0).astype(BF16)
            cdec = jnp.exp2(_dot_sel_right(acumt_scr[...], onehot, 3))
            in_seq = (row // DEC_SEQ) == sidx
            seq_rows = pl.ds(pl.multiple_of(sidx * DEC_SEQ, DEC_SEQ), DEC_SEQ)
            for g in range(SSM_GROUPS):
                rows = slice(g * GROUP_WIDTH, (g + 1) * GROUP_WIDTH)
                hg = h_cur[u, rows, :]
                cj = c_scr[g, seq_rows, :].astype(BF16)
                y_off = _dot_nt(cj, hg.astype(BF16))
                y_scr[g, seq_rows, :] = y_scr[g, seq_rows, :] + y_off * eae_scr[g, seq_rows, :]
                bj = jnp.where(in_seq, b_scr[g], 0.0).astype(BF16)
                st = _dot(xwt_scr[g], bj)
                dec = [jnp.broadcast_to(cdec[g * HEADS_PER_GROUP + r:g * HEADS_PER_GROUP + r + 1],
                                        (SSM_HEAD_DIM, 128)) for r in range(HEADS_PER_GROUP)]
                ht_ref[u, rows, :] = jnp.concatenate(dec, axis=0) * hg + st

        @pl.when(step == nstep - 1)
        def _():
            for g in range(SSM_GROUPS):
                finish_group(g, y_scr[g], xs_scr[g])


def _ssd(proj, dt_raw, consts, wts, nseq, short, past=None, h0=None):
    m = proj.shape[0]
    q = SSM_CHUNK
    nblk = m // q
    if short:
        seqs = 2
        steps = q // DEC_SEQ // seqs
        grid = (nblk, steps)
        blk = lambda a, b: a
        st_idx = lambda a, b: (a * steps + b, 0, 0)
    else:
        seqs = 1
        steps = m // nseq // q
        grid = (nseq, steps)
        blk = lambda a, b: a * steps + b
        st_idx = lambda a, b: (a, 0, 0)
    const2 = lambda shape: pl.BlockSpec(shape, lambda a, b: (0, 0))
    const3 = lambda shape: pl.BlockSpec(shape, lambda a, b: (0, 0, 0))
    in_specs = [
        pl.BlockSpec((q, D_INNER), lambda a, b: (blk(a, b), 0)),
        pl.BlockSpec((q, D_INNER), lambda a, b: (blk(a, b), 1)),
        pl.BlockSpec((q, BC_DIM), lambda a, b: (blk(a, b), 2 * D_INNER // BC_DIM)),
        pl.BlockSpec((q, DT_PAD), lambda a, b: (blk(a, b), 0)),
        const2((q, q)), const2((q, q)), const2((q, q)),
        const3((SSM_GROUPS, 128, GROUP_WIDTH)),
        const2((SSM_CONV, D_INNER)), const2((1, D_INNER)),
        const2((SSM_CONV, BC_DIM)), const2((1, BC_DIM)),
        const2((1, DT_PAD)), const2((1, DT_PAD)),
        const2((1, D_INNER)), const2((1, D_INNER)),
    ]
    args = [proj, proj, proj, dt_raw, *consts, *wts]
    state_spec = pl.BlockSpec((seqs, D_INNER, SSM_STATE), st_idx)
    scratch = [
        pltpu.VMEM((SSM_GROUPS, q, GROUP_WIDTH), F32),
        pltpu.VMEM((SSM_GROUPS, q, SSM_STATE), F32),
        pltpu.VMEM((SSM_GROUPS, q, SSM_STATE), F32),
        pltpu.VMEM((SSM_GROUPS, q, GROUP_WIDTH), F32),
        pltpu.VMEM((q, 128), F32), pltpu.VMEM((q, 128), F32), pltpu.VMEM((q, 128), F32),
        pltpu.VMEM((128, q), F32), pltpu.VMEM((128, q), F32),
    ]
    out_specs = [pl.BlockSpec((q, D_INNER), lambda a, b: (blk(a, b), 0)), state_spec]
    out_shape = [jax.ShapeDtypeStruct((m, D_INNER), BF16),
                 jax.ShapeDtypeStruct((nseq, D_INNER, SSM_STATE), F32)]
    if short:
        per_blk = q // DEC_SEQ
        in_specs += [
            pl.BlockSpec((per_blk, SSM_CONV - 1, D_INNER), lambda a, b: (a, 0, 0)),
            pl.BlockSpec((per_blk, SSM_CONV - 1, BC_DIM), lambda a, b: (a, 0, D_INNER // BC_DIM)),
            pl.BlockSpec(memory_space=pl.ANY),
        ]
        args += [past, past, h0]
        out_specs.append(pl.BlockSpec((per_blk, SSM_CONV - 1, XBC_DIM), lambda a, b: (a, 0, 0)))
        out_shape.append(jax.ShapeDtypeStruct((nseq, SSM_CONV - 1, XBC_DIM), F32))
        scratch += [pltpu.VMEM((SSM_GROUPS, q, GROUP_WIDTH), F32),
                    pltpu.VMEM((SSM_GROUPS, GROUP_WIDTH, q), BF16),
                    pltpu.VMEM((STATE_SLOTS, seqs, D_INNER, SSM_STATE), F32),
                    pltpu.SemaphoreType.DMA((STATE_SLOTS,))]
    else:
        scratch += [pltpu.VMEM((SSM_GROUPS, SSM_STATE, GROUP_WIDTH), F32),
                    pltpu.VMEM((SUBLANES, D_INNER), F32),
                    pltpu.VMEM((SUBLANES, BC_DIM), F32)]
    return pl.pallas_call(
        functools.partial(_ssd_kernel, short=short, q=q, seqs=seqs),
        grid=grid,
        in_specs=in_specs,
        out_specs=out_specs,
        out_shape=out_shape,
        scratch_shapes=scratch,
        compiler_params=_cparams(("arbitrary", "arbitrary")),
        name="ssd_short" if short else "ssd_long",
    )(*args)


def _mix_kernel(*refs, short, tiles_per_seq):
    yn_ref, scb_ref, scc_ref, sch_ref, ga_ref, gb_ref, cw_ref, wa_ref, wb_ref = refs[:9]
    refs = refs[9:]
    if short:
        past_ref = refs[0]
        refs = refs[1:]
    mixed_ref, utail_ref = refs[:2]
    tm = yn_ref.shape[0]

    u = scc_ref[...].astype(F32) * sch_ref[...].astype(F32)
    if short:
        past = _past_rows(past_ref[...])
        shift = lambda k: _shift_short(u, past, k, SC_CONV)
        utail_ref[...] = _seq_tails(u, SC_CONV)
    else:
        halo_scr = refs[2]
        first = pl.program_id(0) % tiles_per_seq == 0
        halo = jnp.where(first, 0.0, halo_scr[...])
        shift = lambda k: _shift_long(u, halo, k)
        halo_scr[...] = u[tm - SUBLANES:]
        utail_ref[...] = u[tm - SUBLANES:]
    s = (scb_ref[...].astype(F32) * _causal_conv(u, cw_ref[...], shift)).astype(BF16)

    a = _dot(yn_ref[...], wa_ref[...])
    b = _dot(s, wb_ref[...])
    mixed = _sigmoid(ga_ref[...].astype(F32)) * a + _sigmoid(gb_ref[...].astype(F32)) * b
    mixed_ref[...] = mixed.astype(BF16)


def _mix(yn, proj, cw, wa, wb, seq_len, tm, short, past=None):
    m = yn.shape[0]
    col = lambda c: (D_INNER + XBC_DIM) // D_MODEL + c
    rows = lambda c: pl.BlockSpec((tm, D_MODEL), lambda i: (i, col(c)))
    resident = lambda shape: pl.BlockSpec(shape, lambda i: (0, 0), pipeline_mode=pl.Buffered(1))
    in_specs = [
        pl.BlockSpec((tm, D_INNER), lambda i: (i, 0)),
        rows(0), rows(1), rows(2), rows(3), rows(4),
        resident((SC_CONV, SC_DIM)),
        resident((D_INNER, D_MODEL)),
        resident((SC_DIM, D_MODEL)),
    ]
    args = [yn, proj, proj, proj, proj, proj, cw, wa, wb]
    scratch = []
    if short:
        seq_blk = (tm // DEC_SEQ, SC_CONV - 1, SC_DIM)
        in_specs.append(pl.BlockSpec(seq_blk, lambda i: (i, 0, 0)))
        args.append(past)
        tail_spec = pl.BlockSpec(seq_blk, lambda i: (i, 0, 0))
        tail_shape = (m // DEC_SEQ, SC_CONV - 1, SC_DIM)
    else:
        scratch.append(pltpu.VMEM((SUBLANES, SC_DIM), F32))
        tail_spec = pl.BlockSpec((SUBLANES, SC_DIM), lambda i: (i, 0))
        tail_shape = (m // tm * SUBLANES, SC_DIM)
    return pl.pallas_call(
        functools.partial(_mix_kernel, short=short, tiles_per_seq=max(seq_len // tm, 1)),
        grid=(m // tm,),
        in_specs=in_specs,
        out_specs=[pl.BlockSpec((tm, D_MODEL), lambda i: (i, 0)), tail_spec],
        out_shape=[jax.ShapeDtypeStruct((m, D_MODEL), BF16),
                   jax.ShapeDtypeStruct(tail_shape, F32)],
        scratch_shapes=scratch,
        compiler_params=_cparams(("arbitrary",)),
        name="mix",
    )(*args)


def _row_halves(ref):
    half = ref.shape[0] // 2
    return [slice(0, half), slice(half, 2 * half)]


def _out_proj_kernel(mixed_ref, x_ref, wo_ref, g_ref, x1_ref, n2_ref):
    for rs in _row_halves(x_ref):
        x1 = x_ref[rs, :] + _dot(mixed_ref[rs, :], wo_ref[...])
        x1_ref[rs, :] = x1
        n2_ref[rs, :] = _rms(x1, g_ref[...]).astype(BF16)


def _out_proj(mixed, x, wo, g, tm):
    m = x.shape[0]
    row = lambda i: (i, 0)
    const = lambda i: (0, 0)
    return pl.pallas_call(
        _out_proj_kernel,
        grid=(m // tm,),
        in_specs=[pl.BlockSpec((tm, D_MODEL), row), pl.BlockSpec((tm, D_MODEL), row),
                  pl.BlockSpec((D_MODEL, D_MODEL), const, pipeline_mode=pl.Buffered(1)),
                  pl.BlockSpec((1, D_MODEL), const)],
        out_specs=[pl.BlockSpec((tm, D_MODEL), row), pl.BlockSpec((tm, D_MODEL), row)],
        out_shape=[jax.ShapeDtypeStruct((m, D_MODEL), F32),
                   jax.ShapeDtypeStruct((m, D_MODEL), BF16)],
        compiler_params=_cparams(("parallel",)),
        name="out_proj",
    )(mixed, x, wo, g)


def _ffn_up_kernel(*refs, short, tiles_per_seq, splits):
    n2_ref, wg_ref, wu_ref, cw_ref, cb_ref = refs[:5]
    refs = refs[5:]
    if short:
        past_ref = refs[0]
        refs = refs[1:]
    act_ref, gtail_ref = refs[:2]
    tm = n2_ref.shape[0]
    n2 = n2_ref[...]
    j = pl.program_id(1)
    first = pl.program_id(0) % tiles_per_seq == 0
    lo = 0
    for width in splits:
        cs = slice(lo, lo + width)
        lo += width
        gate = _dot(n2, wg_ref[:, cs].astype(BF16))
        up = _dot(n2, wu_ref[:, cs].astype(BF16))
        if short:
            past = _past_rows(past_ref[:, :, cs])
            shift = lambda k, gate=gate, past=past: _shift_short(gate, past, k, FFN_CONV)
            gtail_ref[:, :, cs] = _seq_tails(gate, FFN_CONV)
        else:
            halo_scr = refs[2]
            halo = jnp.where(first, 0.0, halo_scr[j, :, cs])
            shift = lambda k, gate=gate, halo=halo: _shift_long(gate, halo, k)
            halo_scr[j, :, cs] = gate[tm - SUBLANES:]
            gtail_ref[:, cs] = gate[tm - SUBLANES:]
        conv = cb_ref[:, cs] + _causal_conv(gate, cw_ref[:, cs], shift)
        act_ref[:, cs] = (_silu(conv) * up).astype(BF16)


def _ffn_up(n2, wg, wu, cw, cb, seq_len, tm, tn, splits, short, past=None):
    m = n2.shape[0]
    nj = D_FF // tn
    assert sum(splits) == tn
    in_specs = [
        pl.BlockSpec((tm, D_MODEL), lambda i, j: (i, 0)),
        pl.BlockSpec((D_MODEL, tn), lambda i, j: (0, j)),
        pl.BlockSpec((D_MODEL, tn), lambda i, j: (0, j)),
        pl.BlockSpec((FFN_CONV, tn), lambda i, j: (0, j)),
        pl.BlockSpec((1, tn), lambda i, j: (0, j)),
    ]
    args = [n2, wg, wu, cw, cb]
    scratch = []
    if short:
        tail_spec = pl.BlockSpec((tm // DEC_SEQ, FFN_CONV - 1, tn), lambda i, j: (i, 0, j))
        in_specs.append(tail_spec)
        args.append(past)
        tail_shape = (m // DEC_SEQ, FFN_CONV - 1, D_FF)
    else:
        scratch.append(pltpu.VMEM((nj, SUBLANES, tn), F32))
        tail_spec = pl.BlockSpec((SUBLANES, tn), lambda i, j: (i, j))
        tail_shape = (m // tm * SUBLANES, D_FF)
    return pl.pallas_call(
        functools.partial(_ffn_up_kernel, short=short, tiles_per_seq=max(seq_len // tm, 1),
                          splits=splits),
        grid=(m // tm, nj),
        in_specs=in_specs,
        out_specs=[pl.BlockSpec((tm, tn), lambda i, j: (i, j)), tail_spec],
        out_shape=[jax.ShapeDtypeStruct((m, D_FF), BF16),
                   jax.ShapeDtypeStruct(tail_shape, F32)],
        scratch_shapes=scratch,
        compiler_params=_cparams(("arbitrary", "arbitrary")),
        name="ffn_up",
    )(*args)


def _ffn_down_kernel(act_ref, wd_ref, x1_ref, x2_ref):
    x2_ref[...] = x1_ref[...] + _dot(act_ref[...], wd_ref[...])


def _ffn_down(act, wd, x1, tm):
    m = x1.shape[0]
    return pl.pallas_call(
        _ffn_down_kernel,
        grid=(m // tm,),
        in_specs=[pl.BlockSpec((tm, D_FF), lambda i: (i, 0)),
                  pl.BlockSpec((D_FF, D_MODEL), lambda i: (0, 0), pipeline_mode=pl.Buffered(1)),
                  pl.BlockSpec((tm, D_MODEL), lambda i: (i, 0))],
        out_specs=pl.BlockSpec((tm, D_MODEL), lambda i: (i, 0)),
        out_shape=jax.ShapeDtypeStruct((m, D_MODEL), F32),
        compiler_params=_cparams(("parallel",)),
        name="ffn_down",
    )(act, wd, x1)


def _ple_kernel(x2_ref, pe_ref, wple_ref, wgate_ref, gp_ref, gf_ref, y_ref):
    for rs in _row_halves(x2_ref):
        x2 = x2_ref[rs, :]
        n3 = _rms(x2, gp_ref[...]).astype(BF16)
        gate = _sigmoid(_dot(n3, wgate_ref[...]))
        emb = _dot(pe_ref[rs, :].astype(BF16), wple_ref[...])
        y_ref[rs, :] = _rms(x2 + emb * gate, gf_ref[...])


def _ple(x2, pe, wple, wgate, gp, gf, tm):
    m = x2.shape[0]
    row = lambda i: (i, 0)
    const = lambda i: (0, 0)
    return pl.pallas_call(
        _ple_kernel,
        grid=(m // tm,),
        in_specs=[pl.BlockSpec((tm, D_MODEL), row), pl.BlockSpec((tm, PLE_DIM), row),
                  pl.BlockSpec((PLE_DIM, D_MODEL), const),
                  pl.BlockSpec((D_MODEL, D_MODEL), const, pipeline_mode=pl.Buffered(1)),
                  pl.BlockSpec((1, D_MODEL), const), pl.BlockSpec((1, D_MODEL), const)],
        out_specs=pl.BlockSpec((tm, D_MODEL), row),
        out_shape=jax.ShapeDtypeStruct((m, D_MODEL), F32),
        compiler_params=_cparams(("parallel",)),
        name="ple",
    )(x2, pe, wple, wgate, gp, gf)


def _tile(m, want):
    return min(m, want)


def _layer(x, pe, w, short, states=None):
    nseq, seq_len, _ = x.shape
    m = nseq * seq_len
    x2d = x.reshape(m, D_MODEL)
    pe2d = pe.reshape(m, PLE_DIM)
    if short:
        ssm0, xbc0, sc0, ffn0 = states
        h0 = ssm0.reshape(nseq, D_INNER, SSM_STATE)

    proj, dt_raw = _in_proj(x2d, w["norm_mix"], w["w_in_t"], _tile(m, 1024), 1024)

    ssd_w = (w["cw_xs"], w["cb_xs"], w["cw_bc"], w["cb_bc"], w["dt_bias"], w["a_neg"],
             w["d_exp"], w["ssm_norm"])
    consts = _ssd_consts(SSM_CHUNK, short)
    if short:
        yn, ssm_new, xbc_new = _ssd(proj, dt_raw, consts, ssd_w, nseq, True, xbc0, h0)
    else:
        yn, ssm_new = _ssd(proj, dt_raw, consts, ssd_w, nseq, False)
        proj3 = proj.reshape(nseq, seq_len, PROJ_DIM)
        xbc_new = proj3[:, seq_len - (SSM_CONV - 1):, D_INNER:D_INNER + XBC_DIM].astype(F32)

    tm = _tile(m, 512)
    tm_mix = _tile(m, 256)
    mixed, u_tail = _mix(yn, proj, w["sc_conv_w"], w["w_ssm_out"], w["w_sc_out"], seq_len,
                         tm_mix, short, sc0 if short else None)
    x1, n2 = _out_proj(mixed, x2d, w["w_o"], w["norm_ffn"], tm)

    tm_ffn = _tile(m, 1024)
    act, g_tail = _ffn_up(n2, w["w_ffn_gate"], w["w_ffn_up"], w["ffn_conv_w"], w["ffn_conv_b"],
                          seq_len, tm_ffn, 512, (256, 256), short, ffn0 if short else None)
    x2 = _ffn_down(act, w["w_ffn_down"], x1, _tile(m, 256))
    y = _ple(x2, pe2d, w["w_ple"], w["w_ple_gate"], w["norm_ple"], w["norm_final"], tm)

    def last_rows(tail, tile_rows, taps, width):
        if short:
            return tail
        per_seq = seq_len // tile_rows
        t = tail.reshape(nseq, per_seq, SUBLANES, width)[:, per_seq - 1]
        return t[:, SUBLANES - (taps - 1):]

    sc_new = last_rows(u_tail, tm_mix, SC_CONV, SC_DIM)
    ffn_new = last_rows(g_tail, tm_ffn, FFN_CONV, D_FF)
    ssm_new = ssm_new.reshape(nseq, SSM_HEADS, SSM_HEAD_DIM, SSM_STATE)
    return y.reshape(nseq, seq_len, D_MODEL), ssm_new, xbc_new, sc_new, ffn_new


def _prep_weights(norm_mix, w_in, ssm_conv_w, ssm_conv_b, ssm_dt_bias, ssm_a_log, ssm_d, ssm_norm,
                  w_ssm_out, sc_conv_w, w_sc_out, w_o, norm_ffn, w_ffn_gate, w_ffn_up,
                  ffn_conv_w, ffn_conv_b, w_ffn_down, norm_ple, w_ple, w_ple_gate, norm_final):
    row = lambda v: v.reshape(1, -1).astype(F32)
    pad_heads = lambda v: jnp.pad(v.astype(F32), (0, DT_PAD - SSM_HEADS)).reshape(1, DT_PAD)
    return {
        "norm_mix": row(norm_mix),
        "w_in_t": w_in.T,
        "cw_xs": ssm_conv_w[:, :D_INNER].astype(F32),
        "cb_xs": row(ssm_conv_b[:D_INNER]),
        "cw_bc": ssm_conv_w[:, D_INNER:].astype(F32),
        "cb_bc": row(ssm_conv_b[D_INNER:]),
        "dt_bias": pad_heads(ssm_dt_bias),
        "a_neg": pad_heads(-jnp.exp(ssm_a_log.astype(F32))),
        "d_exp": row(jnp.repeat(ssm_d, SSM_HEAD_DIM)),
        "ssm_norm": row(ssm_norm),
        "w_ssm_out": w_ssm_out.astype(BF16),
        "sc_conv_w": sc_conv_w.astype(F32),
        "w_sc_out": w_sc_out.astype(BF16),
        "w_o": w_o.astype(BF16),
        "norm_ffn": row(norm_ffn),
        "w_ffn_gate": w_ffn_gate,
        "w_ffn_up": w_ffn_up,
        "ffn_conv_w": ffn_conv_w.astype(F32),
        "ffn_conv_b": row(ffn_conv_b),
        "w_ffn_down": w_ffn_down.astype(BF16),
        "norm_ple": row(norm_ple),
        "w_ple": w_ple.astype(BF16),
        "w_ple_gate": w_ple_gate.astype(BF16),
        "norm_final": row(norm_final),
    }


def kernel(x_prompt, x_sample, state_ssm, state_ssm_conv, state_short_conv, state_ffn_conv, p_prompt, p_sample, norm_mix, w_in, ssm_conv_w, ssm_conv_b, ssm_dt_bias, ssm_a_log, ssm_d, ssm_norm, w_ssm_out, sc_conv_w, w_sc_out, w_o, norm_ffn, w_ffn_gate, w_ffn_up, ffn_conv_w, ffn_conv_b, w_ffn_down, norm_ple, w_ple, w_ple_gate, norm_final):
    depth = w_in.shape[0]
    assert depth == 1, "the final norm is fused into the layer's last kernel"
    assert x_sample.shape[1] == DEC_SEQ
    layer_w = (norm_mix, w_in, ssm_conv_w, ssm_conv_b, ssm_dt_bias, ssm_a_log, ssm_d, ssm_norm,
               w_ssm_out, sc_conv_w, w_sc_out, w_o, norm_ffn, w_ffn_gate, w_ffn_up,
               ffn_conv_w, ffn_conv_b, w_ffn_down, norm_ple, w_ple, w_ple_gate)
    w = _prep_weights(*[t[0] for t in layer_w], norm_final)
    yp, *prompt_states = _layer(x_prompt, p_prompt[0], w, short=False)
    ys, *sample_states = _layer(
        x_sample, p_sample[0], w, short=True,
        states=(state_ssm[0], state_ssm_conv[0], state_short_conv[0], state_ffn_conv[0]))
    outs = [yp, ys]
    for sp, ss in zip(prompt_states, sample_states):
        outs += [sp[None], ss[None]]
    return tuple(outs)
```

```python
import functools

import numpy as np
import jax
import jax.numpy as jnp
from jax import lax
from jax.experimental import pallas as pl
from jax.experimental.pallas import tpu as pltpu

F32 = jnp.float32
BF16 = jnp.bfloat16

D_MODEL = 2048
D_INNER = 4096
SSM_HEADS = 64
SSM_HEAD_DIM = 64
SSM_GROUPS = 8
HEADS_PER_GROUP = SSM_HEADS // SSM_GROUPS
GROUP_WIDTH = D_INNER // SSM_GROUPS
SSM_STATE = 128
SSM_CHUNK = 128
SSM_CONV = 4
BC_DIM = 2 * SSM_GROUPS * SSM_STATE
XBC_DIM = D_INNER + BC_DIM
SC_DIM = D_MODEL
SC_CONV = 3
D_FF = 5632
FFN_CONV = 3
PLE_DIM = 256
EPS = 1e-6
DEC_SEQ = 8
PROJ_DIM = D_INNER + XBC_DIM + 5 * D_MODEL
DT_PAD = 128

SUBLANES = 8
RING_SLOTS = 3
LOG2E = 1.4426950408889634
VMEM_LIMIT = 52 * 1024 * 1024


def _cparams(sem):
    return pltpu.CompilerParams(dimension_semantics=sem, vmem_limit_bytes=VMEM_LIMIT)


def _dot(a, b):
    return jnp.dot(a, b, preferred_element_type=F32)


def _dot_nt(a, b):
    return lax.dot_general(a, b, (((1,), (1,)), ((), ())), preferred_element_type=F32)


def _sigmoid(x):
    return 0.5 * jnp.tanh(0.5 * x) + 0.5


def _silu(x):
    h = 0.5 * x
    return h * jnp.tanh(h) + h


def _softplus(x):
    return jnp.maximum(x, 0.0) + jnp.log(1.0 + jnp.exp(-jnp.abs(x)))


def _rms(x, g):
    ms = jnp.mean(x * x, axis=-1, keepdims=True)
    return x * lax.rsqrt(ms + EPS) * g


def _split_bf16(a, terms):
    parts = []
    r = a
    for _ in range(terms - 1):
        p = r.astype(BF16)
        parts.append(p)
        r = r - p.astype(F32)
    parts.append(r.astype(BF16))
    return parts


def _dot_sel_right(a, sel, terms):
    out = None
    for p in _split_bf16(a, terms):
        d = _dot(p, sel)
        out = d if out is None else out + d
    return out


def _dot_sel_left(sel, a, terms):
    out = None
    for p in _split_bf16(a, terms):
        d = _dot(sel, p)
        out = d if out is None else out + d
    return out


def _shift_long(x, halo, k):
    r = pltpu.roll(x, k, 0)
    h = pltpu.roll(halo, k, 0)
    row = lax.broadcasted_iota(jnp.int32, halo.shape, 0)
    first = jnp.where(row < k, h, r[:SUBLANES])
    return jnp.concatenate([first, r[SUBLANES:]], axis=0)


def _shift_short(x, past, k, taps):
    n = x.shape[0]
    r = pltpu.roll(x, k, 0)
    s = (n + k - (taps - 1)) % n
    p = past if s == 0 else pltpu.roll(past, s, 0)
    t = lax.broadcasted_iota(jnp.int32, x.shape, 0) & (DEC_SEQ - 1)
    return jnp.where(t < k, p, r)


def _past_rows(state):
    n, k, c = state.shape
    t = lax.broadcasted_iota(jnp.int32, (n, DEC_SEQ, c), 1)
    out = jnp.zeros((n, DEC_SEQ, c), F32)
    for r in range(k):
        out = jnp.where(t == r, jnp.broadcast_to(state[:, r:r + 1, :], (n, DEC_SEQ, c)), out)
    return out.reshape(n * DEC_SEQ, c)


def _seq_tails(x, taps):
    rows, width = x.shape
    return x.reshape(rows // DEC_SEQ, DEC_SEQ, width)[:, DEC_SEQ - (taps - 1):, :]


def _causal_conv(x, w, shift):
    taps = w.shape[0]
    y = w[taps - 1:taps] * x
    for k in range(1, taps):
        y = y + w[taps - 1 - k:taps - k] * shift(k)
    return y


def _in_proj_kernel(x_ref, g_ref, w_hbm, wdt_ref, proj_ref, dt_ref, n1_scr, w_ring, w_sem, *,
                    sub, tn, n_lo, dt_hi):
    nj = pl.num_programs(1)
    j = pl.program_id(1)
    t = pl.program_id(0) * nj + j
    total = pl.num_programs(0) * nj

    def w_copy(k):
        jj = k % nj
        row0 = jnp.where(jj < n_lo, jj * tn, dt_hi + (jj - n_lo) * tn)
        slot = k % RING_SLOTS
        return pltpu.make_async_copy(w_hbm.at[pl.ds(pl.multiple_of(row0, SSM_HEADS), tn)],
                                     w_ring.at[slot], w_sem.at[slot])

    @pl.when(t == 0)
    def _():
        for k in range(RING_SLOTS - 1):
            w_copy(t + k).start()

    @pl.when(t + RING_SLOTS - 1 < total)
    def _():
        w_copy(t + RING_SLOTS - 1).start()

    @pl.when(j == 0)
    def _():
        n1 = _rms(x_ref[...], g_ref[...]).astype(BF16)
        n1_scr[...] = n1
        dt_ref[...] = _dot_nt(n1, wdt_ref[...].astype(BF16))

    w_copy(t).wait()
    w_cur = w_ring.at[t % RING_SLOTS]
    n1 = n1_scr[...]
    for c in range(tn // sub):
        rs = slice(c * sub, (c + 1) * sub)
        proj_ref[:, rs] = _dot_nt(n1, w_cur[rs, :].astype(BF16)).astype(BF16)


def _in_proj(x, g, w_t, tm, tn):
    m = x.shape[0]
    dt_lo = D_INNER + XBC_DIM
    dt_hi = dt_lo + SSM_HEADS
    return pl.pallas_call(
        functools.partial(_in_proj_kernel, sub=256, tn=tn, n_lo=dt_lo // tn, dt_hi=dt_hi),
        grid=(m // tm, PROJ_DIM // tn),
        in_specs=[
            pl.BlockSpec((tm, D_MODEL), lambda i, j: (i, 0), pipeline_mode=pl.Buffered(1)),
            pl.BlockSpec((1, D_MODEL), lambda i, j: (0, 0)),
            pl.BlockSpec(memory_space=pl.ANY),
            pl.BlockSpec((DT_PAD, D_MODEL), lambda i, j: (dt_lo // DT_PAD, 0)),
        ],
        out_specs=[
            pl.BlockSpec((tm, tn), lambda i, j: (i, j)),
            pl.BlockSpec((tm, DT_PAD), lambda i, j: (i, 0)),
        ],
        out_shape=[
            jax.ShapeDtypeStruct((m, PROJ_DIM), BF16),
            jax.ShapeDtypeStruct((m, DT_PAD), F32),
        ],
        scratch_shapes=[pltpu.VMEM((tm, D_MODEL), BF16),
                        pltpu.VMEM((RING_SLOTS, tn, D_MODEL), F32),
                        pltpu.SemaphoreType.DMA((RING_SLOTS,))],
        compiler_params=_cparams(("arbitrary", "arbitrary")),
        name="in_proj",
    )(x, g, w_t, w_t)


def _ssd_consts(q, short):
    l = np.arange(q)[:, None]
    s = np.arange(q)[None, :]
    if short:
        tri = (l // DEC_SEQ == s // DEC_SEQ) & (s <= l)
        end = s == (l // DEC_SEQ) * DEC_SEQ + DEC_SEQ - 1
    else:
        tri = s <= l
        end = np.broadcast_to(s == q - 1, (q, q))
    mask = np.where(tri, 0.0, -np.inf).astype(np.float32)
    h = np.arange(128)[None, :, None]
    g = np.arange(SSM_GROUPS)[:, None, None]
    c = np.arange(GROUP_WIDTH)[None, None, :]
    expand = h == g * HEADS_PER_GROUP + c // SSM_HEAD_DIM
    return (jnp.asarray(tri, BF16), jnp.asarray(end, BF16), jnp.asarray(mask),
            jnp.asarray(expand, BF16))


def _ssd_kernel(*refs, short, q, seqs):
    z_ref, xs_ref, bc_ref, dt_ref, tri_ref, end_ref, mask_ref, exp_ref = refs[:8]
    refs = refs[8:]
    cwx_ref, cbx_ref, cwb_ref, cbb_ref, dtb_ref, a_ref, dexp_ref, gn_ref = refs[:8]
    refs = refs[8:]
    past_xs_ref = past_bc_ref = tail_xs = tail_bc = None
    if short:
        past_xs_ref, past_bc_ref, h0_ref = refs[:3]
        refs = refs[3:]
    y_ref, ht_ref = refs[:2]
    refs = refs[2:]
    if short:
        xtail_ref = refs[0]
        refs = refs[1:]
    xs_scr, b_scr, c_scr, y_scr, ea_scr, we_scr, aend_scr, acumt_scr, arow_scr = refs[:9]
    refs = refs[9:]
    if short:
        eae_scr, xwt_scr, h_ring, h_sem = refs
    else:
        h_scr, tail_xs, tail_bc = refs

    step = pl.program_id(1)
    nstep = pl.num_programs(1)

    if short:
        t = pl.program_id(0) * nstep + step
        total = pl.num_programs(0) * nstep

        def state_copy(k):
            slot = k % RING_SLOTS
            return pltpu.make_async_copy(h0_ref.at[pl.ds(k * seqs, seqs)], h_ring.at[slot],
                                         h_sem.at[slot])

        @pl.when(t == 0)
        def _():
            for k in range(RING_SLOTS - 1):
                state_copy(k).start()

        @pl.when(t + RING_SLOTS - 1 < total)
        def _():
            state_copy(t + RING_SLOTS - 1).start()

    def conv_block(src_ref, past_ref, tail, cw, cb_ref, cs):
        xb = src_ref[:, cs]
        x = xb.astype(F32)
        if short:
            past = _past_rows(past_ref[:, :, cs])
            shift = lambda k: _shift_short(x, past, k, SSM_CONV)
        else:
            halo = tail[:, cs]
            shift = lambda k: _shift_long(x, halo, k)
            tail[:, cs] = x[q - SUBLANES:]
        return _silu(cb_ref[:, cs] + _causal_conv(x, cw[:, cs], shift))

    def conv_group(g):
        cs = slice(g * GROUP_WIDTH, (g + 1) * GROUP_WIDTH)
        bs = slice(g * SSM_STATE, (g + 1) * SSM_STATE)
        cc = slice((SSM_GROUPS + g) * SSM_STATE, (SSM_GROUPS + g + 1) * SSM_STATE)
        cwx = cwx_ref[...]
        cwb = cwb_ref[...]
        return (conv_block(xs_ref, past_xs_ref, tail_xs, cwx, cbx_ref, cs),
                conv_block(bc_ref, past_bc_ref, tail_bc, cwb, cbb_ref, bs),
                conv_block(bc_ref, past_bc_ref, tail_bc, cwb, cbb_ref, cc))

    def dt_prelude():
        dt = _softplus(dt_ref[...] + dtb_ref[...])
        acum = _dot_sel_left(tri_ref[...], dt * a_ref[...], 3)
        aend = _dot_sel_left(end_ref[...], acum, 3)
        acumt = acum.T
        acumt_scr[...] = acumt * LOG2E
        arow_scr[...] = (acumt - jnp.log(dt.T)) * LOG2E
        aend_scr[...] = aend
        ea_scr[...] = jnp.exp(acum)
        we_scr[...] = jnp.exp(aend - acum) * dt

    def diag_group(g, xs, bg, cg):
        cb = _dot_nt(cg.astype(BF16), bg.astype(BF16))
        mask = mask_ref[...]
        lane = lax.broadcasted_iota(jnp.int32, (q, 128), 1)
        ys = []
        for pair in range(HEADS_PER_GROUP // 2):
            ws = []
            for half in range(2):
                h = g * HEADS_PER_GROUP + 2 * pair + half
                colb = jnp.broadcast_to(acumt_scr[pl.ds(h, 1), :], (q, q)).T
                seg = colb - arow_scr[pl.ds(h, 1), :] + mask
                ws.append((cb * jnp.exp2(seg)).astype(BF16))
            xp = xs[:, pair * 128:(pair + 1) * 128]
            top = jnp.where(lane < SSM_HEAD_DIM, xp, 0.0).astype(BF16)
            bot = jnp.where(lane >= SSM_HEAD_DIM, xp, 0.0).astype(BF16)
            ys.append(_dot(jnp.concatenate(ws, axis=1), jnp.concatenate([top, bot], axis=0)))
        return jnp.concatenate(ys, axis=1)

    def finish_group(g, y, xs):
        cs = slice(g * GROUP_WIDTH, (g + 1) * GROUP_WIDTH)
        y = y + dexp_ref[:, cs] * xs
        y = y * _silu(z_ref[:, cs].astype(F32))
        ms = jnp.mean(y * y, axis=-1, keepdims=True)
        y_ref[:, cs] = (y * lax.rsqrt(ms + EPS) * gn_ref[:, cs]).astype(BF16)

    if not short:
        @pl.when(step == 0)
        def _():
            h_scr[...] = jnp.zeros_like(h_scr)
            tail_xs[...] = jnp.zeros_like(tail_xs)
            tail_bc[...] = jnp.zeros_like(tail_bc)

        dt_prelude()
        for g in range(SSM_GROUPS):
            xs_scr[g], b_scr[g], c_scr[g] = conv_group(g)
        for g in range(SSM_GROUPS):
            xs, bg, cg = xs_scr[g], b_scr[g], c_scr[g]
            expand = exp_ref[g]
            y_diag = diag_group(g, xs, bg, cg)
            hg = h_scr[g]
            y_off = _dot(cg.astype(BF16), hg.astype(BF16))
            y_scr[g] = y_diag + y_off * _dot_sel_right(ea_scr[...], expand, 2)
            xw = (xs * _dot_sel_right(we_scr[...], expand, 2)).astype(BF16)
            st = _dot(bg.T.astype(BF16), xw)
            dec = jnp.exp(_dot_sel_right(aend_scr[:SUBLANES, :], expand, 3))[:1]
            h_scr[g] = dec * hg + st
        for g in range(SSM_GROUPS):
            finish_group(g, y_scr[g], xs_scr[g])

        @pl.when(step == nstep - 1)
        def _():
            for g in range(SSM_GROUPS):
                ht_ref[0, g * GROUP_WIDTH:(g + 1) * GROUP_WIDTH, :] = h_scr[g].T
    else:
        @pl.when(step == 0)
        def _():
            xtail_ref[:, :, :D_INNER] = _seq_tails(xs_ref[...].astype(F32), SSM_CONV)
            xtail_ref[:, :, D_INNER:] = _seq_tails(bc_ref[...].astype(F32), SSM_CONV)
            dt_prelude()
            for g in range(SSM_GROUPS):
                xs, bg, cg = conv_group(g)
                expand = exp_ref[g]
                xs_scr[g] = xs
                b_scr[g] = bg
                c_scr[g] = cg
                y_scr[g] = diag_group(g, xs, bg, cg)
                eae_scr[g] = _dot_sel_right(ea_scr[...], expand, 2)
                xwt_scr[g] = (xs * _dot_sel_right(we_scr[...], expand, 2)).T.astype(BF16)

        state_copy(t).wait()
        h_cur = h_ring.at[t % RING_SLOTS]
        for u in range(seqs):
            sidx = step * seqs + u
            row = lax.broadcasted_iota(jnp.int32, (q, 128), 0)
            onehot = jnp.where(row == sidx * DEC_SEQ + DEC_SEQ - 1, 1.0, 0.0).astype(BF16)
            cdec = jnp.exp2(_dot_sel_right(acumt_scr[...], onehot, 3))
            in_seq = (row // DEC_SEQ) == sidx
            seq_rows = pl.ds(pl.multiple_of(sidx * DEC_SEQ, DEC_SEQ), DEC_SEQ)
            for g in range(SSM_GROUPS):
                rows = slice(g * GROUP_WIDTH, (g + 1) * GROUP_WIDTH)
                hg = h_cur[u, rows, :]
                cj = c_scr[g, seq_rows, :].astype(BF16)
                y_off = _dot_nt(cj, hg.astype(BF16))
                y_scr[g, seq_rows, :] = y_scr[g, seq_rows, :] + y_off * eae_scr[g, seq_rows, :]
                bj = jnp.where(in_seq, b_scr[g], 0.0).astype(BF16)
                st = _dot(xwt_scr[g], bj)
                dec = [jnp.broadcast_to(cdec[g * HEADS_PER_GROUP + r:g * HEADS_PER_GROUP + r + 1],
                                        (SSM_HEAD_DIM, 128)) for r in range(HEADS_PER_GROUP)]
                ht_ref[u, rows, :] = jnp.concatenate(dec, axis=0) * hg + st

        @pl.when(step == nstep - 1)
        def _():
            for g in range(SSM_GROUPS):
                finish_group(g, y_scr[g], xs_scr[g])


def _ssd(proj, dt_raw, consts, wts, nseq, short, past=None, h0=None):
    m = proj.shape[0]
    q = SSM_CHUNK
    nblk = m // q
    if short:
        seqs = 2
        steps = q // DEC_SEQ // seqs
        grid = (nblk, steps)
        blk = lambda a, b: a
        st_idx = lambda a, b: (a * steps + b, 0, 0)
    else:
        seqs = 1
        steps = m // nseq // q
        grid = (nseq, steps)
        blk = lambda a, b: a * steps + b
        st_idx = lambda a, b: (a, 0, 0)
    const2 = lambda shape: pl.BlockSpec(shape, lambda a, b: (0, 0))
    const3 = lambda shape: pl.BlockSpec(shape, lambda a, b: (0, 0, 0))
    in_specs = [
        pl.BlockSpec((q, D_INNER), lambda a, b: (blk(a, b), 0)),
        pl.BlockSpec((q, D_INNER), lambda a, b: (blk(a, b), 1)),
        pl.BlockSpec((q, BC_DIM), lambda a, b: (blk(a, b), 2 * D_INNER // BC_DIM)),
        pl.BlockSpec((q, DT_PAD), lambda a, b: (blk(a, b), 0)),
        const2((q, q)), const2((q, q)), const2((q, q)),
        const3((SSM_GROUPS, 128, GROUP_WIDTH)),
        const2((SSM_CONV, D_INNER)), const2((1, D_INNER)),
        const2((SSM_CONV, BC_DIM)), const2((1, BC_DIM)),
        const2((1, DT_PAD)), const2((1, DT_PAD)),
        const2((1, D_INNER)), const2((1, D_INNER)),
    ]
    args = [proj, proj, proj, dt_raw, *consts, *wts]
    state_spec = pl.BlockSpec((seqs, D_INNER, SSM_STATE), st_idx)
    scratch = [
        pltpu.VMEM((SSM_GROUPS, q, GROUP_WIDTH), F32),
        pltpu.VMEM((SSM_GROUPS, q, SSM_STATE), F32),
        pltpu.VMEM((SSM_GROUPS, q, SSM_STATE), F32),
        pltpu.VMEM((SSM_GROUPS, q, GROUP_WIDTH), F32),
        pltpu.VMEM((q, 128), F32), pltpu.VMEM((q, 128), F32), pltpu.VMEM((q, 128), F32),
        pltpu.VMEM((128, q), F32), pltpu.VMEM((128, q), F32),
    ]
    out_specs = [pl.BlockSpec((q, D_INNER), lambda a, b: (blk(a, b), 0)), state_spec]
    out_shape = [jax.ShapeDtypeStruct((m, D_INNER), BF16),
                 jax.ShapeDtypeStruct((nseq, D_INNER, SSM_STATE), F32)]
    if short:
        per_blk = q // DEC_SEQ
        in_specs += [
            pl.BlockSpec((per_blk, SSM_CONV - 1, D_INNER), lambda a, b: (a, 0, 0)),
            pl.BlockSpec((per_blk, SSM_CONV - 1, BC_DIM), lambda a, b: (a, 0, D_INNER // BC_DIM)),
            pl.BlockSpec(memory_space=pl.ANY),
        ]
        args += [past, past, h0]
        out_specs.append(pl.BlockSpec((per_blk, SSM_CONV - 1, XBC_DIM), lambda a, b: (a, 0, 0)))
        out_shape.append(jax.ShapeDtypeStruct((nseq, SSM_CONV - 1, XBC_DIM), F32))
        scratch += [pltpu.VMEM((SSM_GROUPS, q, GROUP_WIDTH), F32),
                    pltpu.VMEM((SSM_GROUPS, GROUP_WIDTH, q), BF16),
                    pltpu.VMEM((RING_SLOTS, seqs, D_INNER, SSM_STATE), F32),
                    pltpu.SemaphoreType.DMA((RING_SLOTS,))]
    else:
        scratch += [pltpu.VMEM((SSM_GROUPS, SSM_STATE, GROUP_WIDTH), F32),
                    pltpu.VMEM((SUBLANES, D_INNER), F32),
                    pltpu.VMEM((SUBLANES, BC_DIM), F32)]
    return pl.pallas_call(
        functools.partial(_ssd_kernel, short=short, q=q, seqs=seqs),
        grid=grid,
        in_specs=in_specs,
        out_specs=out_specs,
        out_shape=out_shape,
        scratch_shapes=scratch,
        compiler_params=_cparams(("arbitrary", "arbitrary")),
        name="ssd_short" if short else "ssd_long",
    )(*args)


def _mix_kernel(*refs, short, tiles_per_seq):
    yn_ref, scb_ref, scc_ref, sch_ref, ga_ref, gb_ref, cw_ref, wa_ref, wb_ref = refs[:9]
    refs = refs[9:]
    if short:
        past_ref = refs[0]
        refs = refs[1:]
    mixed_ref, utail_ref = refs[:2]
    tm = yn_ref.shape[0]

    u = scc_ref[...].astype(F32) * sch_ref[...].astype(F32)
    if short:
        past = _past_rows(past_ref[...])
        shift = lambda k: _shift_short(u, past, k, SC_CONV)
        utail_ref[...] = _seq_tails(u, SC_CONV)
    else:
        halo_scr = refs[2]
        first = pl.program_id(0) % tiles_per_seq == 0
        halo = jnp.where(first, 0.0, halo_scr[...])
        shift = lambda k: _shift_long(u, halo, k)
        halo_scr[...] = u[tm - SUBLANES:]
        utail_ref[...] = u[tm - SUBLANES:]
    s = (scb_ref[...].astype(F32) * _causal_conv(u, cw_ref[...], shift)).astype(BF16)

    a = _dot(yn_ref[...], wa_ref[...])
    b = _dot(s, wb_ref[...])
    mixed = _sigmoid(ga_ref[...].astype(F32)) * a + _sigmoid(gb_ref[...].astype(F32)) * b
    mixed_ref[...] = mixed.astype(BF16)


def _mix(yn, proj, cw, wa, wb, seq_len, tm, short, past=None):
    m = yn.shape[0]
    col = lambda c: (D_INNER + XBC_DIM) // D_MODEL + c
    rows = lambda c: pl.BlockSpec((tm, D_MODEL), lambda i: (i, col(c)))
    resident = lambda shape: pl.BlockSpec(shape, lambda i: (0, 0), pipeline_mode=pl.Buffered(1))
    in_specs = [
        pl.BlockSpec((tm, D_INNER), lambda i: (i, 0)),
        rows(0), rows(1), rows(2), rows(3), rows(4),
        resident((SC_CONV, SC_DIM)),
        resident((D_INNER, D_MODEL)),
        resident((SC_DIM, D_MODEL)),
    ]
    args = [yn, proj, proj, proj, proj, proj, cw, wa, wb]
    scratch = []
    if short:
        seq_blk = (tm // DEC_SEQ, SC_CONV - 1, SC_DIM)
        in_specs.append(pl.BlockSpec(seq_blk, lambda i: (i, 0, 0)))
        args.append(past)
        tail_spec = pl.BlockSpec(seq_blk, lambda i: (i, 0, 0))
        tail_shape = (m // DEC_SEQ, SC_CONV - 1, SC_DIM)
    else:
        scratch.append(pltpu.VMEM((SUBLANES, SC_DIM), F32))
        tail_spec = pl.BlockSpec((SUBLANES, SC_DIM), lambda i: (i, 0))
        tail_shape = (m // tm * SUBLANES, SC_DIM)
    return pl.pallas_call(
        functools.partial(_mix_kernel, short=short, tiles_per_seq=max(seq_len // tm, 1)),
        grid=(m // tm,),
        in_specs=in_specs,
        out_specs=[pl.BlockSpec((tm, D_MODEL), lambda i: (i, 0)), tail_spec],
        out_shape=[jax.ShapeDtypeStruct((m, D_MODEL), BF16),
                   jax.ShapeDtypeStruct(tail_shape, F32)],
        scratch_shapes=scratch,
        compiler_params=_cparams(("arbitrary",)),
        name="mix",
    )(*args)


def _row_halves(ref):
    half = ref.shape[0] // 2
    return [slice(0, half), slice(half, 2 * half)]


def _out_proj_kernel(mixed_ref, x_ref, wo_ref, g_ref, x1_ref, n2_ref):
    for rs in _row_halves(x_ref):
        x1 = x_ref[rs, :] + _dot(mixed_ref[rs, :], wo_ref[...])
        x1_ref[rs, :] = x1
        n2_ref[rs, :] = _rms(x1, g_ref[...]).astype(BF16)


def _out_proj(mixed, x, wo, g, tm):
    m = x.shape[0]
    row = lambda i: (i, 0)
    const = lambda i: (0, 0)
    return pl.pallas_call(
        _out_proj_kernel,
        grid=(m // tm,),
        in_specs=[pl.BlockSpec((tm, D_MODEL), row), pl.BlockSpec((tm, D_MODEL), row),
                  pl.BlockSpec((D_MODEL, D_MODEL), const, pipeline_mode=pl.Buffered(1)),
                  pl.BlockSpec((1, D_MODEL), const)],
        out_specs=[pl.BlockSpec((tm, D_MODEL), row), pl.BlockSpec((tm, D_MODEL), row)],
        out_shape=[jax.ShapeDtypeStruct((m, D_MODEL), F32),
                   jax.ShapeDtypeStruct((m, D_MODEL), BF16)],
        compiler_params=_cparams(("parallel",)),
        name="out_proj",
    )(mixed, x, wo, g)


def _ffn_up_kernel(*refs, short, tiles_per_seq, splits):
    n2_ref, wg_ref, wu_ref, cw_ref, cb_ref = refs[:5]
    refs = refs[5:]
    if short:
        past_ref = refs[0]
        refs = refs[1:]
    act_ref, gtail_ref = refs[:2]
    tm = n2_ref.shape[0]
    n2 = n2_ref[...]
    j = pl.program_id(1)
    first = pl.program_id(0) % tiles_per_seq == 0
    lo = 0
    for width in splits:
        cs = slice(lo, lo + width)
        lo += width
        gate = _dot(n2, wg_ref[:, cs].astype(BF16))
        up = _dot(n2, wu_ref[:, cs].astype(BF16))
        if short:
            past = _past_rows(past_ref[:, :, cs])
            shift = lambda k, gate=gate, past=past: _shift_short(gate, past, k, FFN_CONV)
            gtail_ref[:, :, cs] = _seq_tails(gate, FFN_CONV)
        else:
            halo_scr = refs[2]
            halo = jnp.where(first, 0.0, halo_scr[j, :, cs])
            shift = lambda k, gate=gate, halo=halo: _shift_long(gate, halo, k)
            halo_scr[j, :, cs] = gate[tm - SUBLANES:]
            gtail_ref[:, cs] = gate[tm - SUBLANES:]
        conv = cb_ref[:, cs] + _causal_conv(gate, cw_ref[:, cs], shift)
        act_ref[:, cs] = (_silu(conv) * up).astype(BF16)


def _ffn_up(n2, wg, wu, cw, cb, seq_len, tm, tn, splits, short, past=None):
    m = n2.shape[0]
    nj = D_FF // tn
    assert sum(splits) == tn
    in_specs = [
        pl.BlockSpec((tm, D_MODEL), lambda i, j: (i, 0)),
        pl.BlockSpec((D_MODEL, tn), lambda i, j: (0, j)),
        pl.BlockSpec((D_MODEL, tn), lambda i, j: (0, j)),
        pl.BlockSpec((FFN_CONV, tn), lambda i, j: (0, j)),
        pl.BlockSpec((1, tn), lambda i, j: (0, j)),
    ]
    args = [n2, wg, wu, cw, cb]
    scratch = []
    if short:
        tail_spec = pl.BlockSpec((tm // DEC_SEQ, FFN_CONV - 1, tn), lambda i, j: (i, 0, j))
        in_specs.append(tail_spec)
        args.append(past)
        tail_shape = (m // DEC_SEQ, FFN_CONV - 1, D_FF)
    else:
        scratch.append(pltpu.VMEM((nj, SUBLANES, tn), F32))
        tail_spec = pl.BlockSpec((SUBLANES, tn), lambda i, j: (i, j))
        tail_shape = (m // tm * SUBLANES, D_FF)
    return pl.pallas_call(
        functools.partial(_ffn_up_kernel, short=short, tiles_per_seq=max(seq_len // tm, 1),
                          splits=splits),
        grid=(m // tm, nj),
        in_specs=in_specs,
        out_specs=[pl.BlockSpec((tm, tn), lambda i, j: (i, j)), tail_spec],
        out_shape=[jax.ShapeDtypeStruct((m, D_FF), BF16),
                   jax.ShapeDtypeStruct(tail_shape, F32)],
        scratch_shapes=scratch,
        compiler_params=_cparams(("arbitrary", "arbitrary")),
        name="ffn_up",
    )(*args)


def _ffn_down_kernel(act_ref, wd_ref, x1_ref, x2_ref):
    x2_ref[...] = x1_ref[...] + _dot(act_ref[...], wd_ref[...])


def _ffn_down(act, wd, x1, tm):
    m = x1.shape[0]
    return pl.pallas_call(
        _ffn_down_kernel,
        grid=(m // tm,),
        in_specs=[pl.BlockSpec((tm, D_FF), lambda i: (i, 0)),
                  pl.BlockSpec((D_FF, D_MODEL), lambda i: (0, 0), pipeline_mode=pl.Buffered(1)),
                  pl.BlockSpec((tm, D_MODEL), lambda i: (i, 0))],
        out_specs=pl.BlockSpec((tm, D_MODEL), lambda i: (i, 0)),
        out_shape=jax.ShapeDtypeStruct((m, D_MODEL), F32),
        compiler_params=_cparams(("parallel",)),
        name="ffn_down",
    )(act, wd, x1)


def _ple_kernel(x2_ref, pe_ref, wple_ref, wgate_ref, gp_ref, gf_ref, y_ref):
    for rs in _row_halves(x2_ref):
        x2 = x2_ref[rs, :]
        n3 = _rms(x2, gp_ref[...]).astype(BF16)
        gate = _sigmoid(_dot(n3, wgate_ref[...]))
        emb = _dot(pe_ref[rs, :].astype(BF16), wple_ref[...])
        y_ref[rs, :] = _rms(x2 + emb * gate, gf_ref[...])


def _ple(x2, pe, wple, wgate, gp, gf, tm):
    m = x2.shape[0]
    row = lambda i: (i, 0)
    const = lambda i: (0, 0)
    return pl.pallas_call(
        _ple_kernel,
        grid=(m // tm,),
        in_specs=[pl.BlockSpec((tm, D_MODEL), row), pl.BlockSpec((tm, PLE_DIM), row),
                  pl.BlockSpec((PLE_DIM, D_MODEL), const),
                  pl.BlockSpec((D_MODEL, D_MODEL), const, pipeline_mode=pl.Buffered(1)),
                  pl.BlockSpec((1, D_MODEL), const), pl.BlockSpec((1, D_MODEL), const)],
        out_specs=pl.BlockSpec((tm, D_MODEL), row),
        out_shape=jax.ShapeDtypeStruct((m, D_MODEL), F32),
        compiler_params=_cparams(("parallel",)),
        name="ple",
    )(x2, pe, wple, wgate, gp, gf)


def _tile(m, want):
    return min(m, want)


def _layer(x, pe, w, short, states=None):
    nseq, seq_len, _ = x.shape
    m = nseq * seq_len
    x2d = x.reshape(m, D_MODEL)
    pe2d = pe.reshape(m, PLE_DIM)
    if short:
        ssm0, xbc0, sc0, ffn0 = states
        h0 = ssm0.reshape(nseq, D_INNER, SSM_STATE)

    proj, dt_raw = _in_proj(x2d, w["norm_mix"], w["w_in_t"], _tile(m, 1024), 1024)

    ssd_w = (w["cw_xs"], w["cb_xs"], w["cw_bc"], w["cb_bc"], w["dt_bias"], w["a_neg"],
             w["d_exp"], w["ssm_norm"])
    consts = _ssd_consts(SSM_CHUNK, short)
    if short:
        yn, ssm_new, xbc_new = _ssd(proj, dt_raw, consts, ssd_w, nseq, True, xbc0, h0)
    else:
        yn, ssm_new = _ssd(proj, dt_raw, consts, ssd_w, nseq, False)
        proj3 = proj.reshape(nseq, seq_len, PROJ_DIM)
        xbc_new = proj3[:, seq_len - (SSM_CONV - 1):, D_INNER:D_INNER + XBC_DIM].astype(F32)

    tm = _tile(m, 512)
    tm_mix = _tile(m, 256)
    mixed, u_tail = _mix(yn, proj, w["sc_conv_w"], w["w_ssm_out"], w["w_sc_out"], seq_len,
                         tm_mix, short, sc0 if short else None)
    x1, n2 = _out_proj(mixed, x2d, w["w_o"], w["norm_ffn"], tm)

    tm_ffn = _tile(m, 1024)
    act, g_tail = _ffn_up(n2, w["w_ffn_gate"], w["w_ffn_up"], w["ffn_conv_w"], w["ffn_conv_b"],
                          seq_len, tm_ffn, 512, (256, 256), short, ffn0 if short else None)
    x2 = _ffn_down(act, w["w_ffn_down"], x1, _tile(m, 256))
    y = _ple(x2, pe2d, w["w_ple"], w["w_ple_gate"], w["norm_ple"], w["norm_final"], tm)

    def last_rows(tail, tile_rows, taps, width):
        if short:
            return tail
        per_seq = seq_len // tile_rows
        t = tail.reshape(nseq, per_seq, SUBLANES, width)[:, per_seq - 1]
        return t[:, SUBLANES - (taps - 1):]

    sc_new = last_rows(u_tail, tm_mix, SC_CONV, SC_DIM)
    ffn_new = last_rows(g_tail, tm_ffn, FFN_CONV, D_FF)
    ssm_new = ssm_new.reshape(nseq, SSM_HEADS, SSM_HEAD_DIM, SSM_STATE)
    return y.reshape(nseq, seq_len, D_MODEL), ssm_new, xbc_new, sc_new, ffn_new


def _prep_weights(norm_mix, w_in, ssm_conv_w, ssm_conv_b, ssm_dt_bias, ssm_a_log, ssm_d, ssm_norm,
                  w_ssm_out, sc_conv_w, w_sc_out, w_o, norm_ffn, w_ffn_gate, w_ffn_up,
                  ffn_conv_w, ffn_conv_b, w_ffn_down, norm_ple, w_ple, w_ple_gate, norm_final):
    row = lambda v: v.reshape(1, -1).astype(F32)
    pad_heads = lambda v: jnp.pad(v.astype(F32), (0, DT_PAD - SSM_HEADS)).reshape(1, DT_PAD)
    return {
        "norm_mix": row(norm_mix),
        "w_in_t": w_in.T,
        "cw_xs": ssm_conv_w[:, :D_INNER].astype(F32),
        "cb_xs": row(ssm_conv_b[:D_INNER]),
        "cw_bc": ssm_conv_w[:, D_INNER:].astype(F32),
        "cb_bc": row(ssm_conv_b[D_INNER:]),
        "dt_bias": pad_heads(ssm_dt_bias),
        "a_neg": pad_heads(-jnp.exp(ssm_a_log.astype(F32))),
        "d_exp": row(jnp.repeat(ssm_d, SSM_HEAD_DIM)),
        "ssm_norm": row(ssm_norm),
        "w_ssm_out": w_ssm_out.astype(BF16),
        "sc_conv_w": sc_conv_w.astype(F32),
        "w_sc_out": w_sc_out.astype(BF16),
        "w_o": w_o.astype(BF16),
        "norm_ffn": row(norm_ffn),
        "w_ffn_gate": w_ffn_gate,
        "w_ffn_up": w_ffn_up,
        "ffn_conv_w": ffn_conv_w.astype(F32),
        "ffn_conv_b": row(ffn_conv_b),
        "w_ffn_down": w_ffn_down.astype(BF16),
        "norm_ple": row(norm_ple),
        "w_ple": w_ple.astype(BF16),
        "w_ple_gate": w_ple_gate.astype(BF16),
        "norm_final": row(norm_final),
    }


def kernel(x_prompt, x_sample, state_ssm, state_ssm_conv, state_short_conv, state_ffn_conv, p_prompt, p_sample, norm_mix, w_in, ssm_conv_w, ssm_conv_b, ssm_dt_bias, ssm_a_log, ssm_d, ssm_norm, w_ssm_out, sc_conv_w, w_sc_out, w_o, norm_ffn, w_ffn_gate, w_ffn_up, ffn_conv_w, ffn_conv_b, w_ffn_down, norm_ple, w_ple, w_ple_gate, norm_final):
    depth = w_in.shape[0]
    assert depth == 1, "the final norm is fused into the layer's last kernel"
    assert x_sample.shape[1] == DEC_SEQ
    layer_w = (norm_mix, w_in, ssm_conv_w, ssm_conv_b, ssm_dt_bias, ssm_a_log, ssm_d, ssm_norm,
               w_ssm_out, sc_conv_w, w_sc_out, w_o, norm_ffn, w_ffn_gate, w_ffn_up,
               ffn_conv_w, ffn_conv_b, w_ffn_down, norm_ple, w_ple, w_ple_gate)
    w = _prep_weights(*[t[0] for t in layer_w], norm_final)
    yp, *prompt_states = _layer(x_prompt, p_prompt[0], w, short=False)
    ys, *sample_states = _layer(
        x_sample, p_sample[0], w, short=True,
        states=(state_ssm[0], state_ssm_conv[0], state_short_conv[0], state_ffn_conv[0]))
    outs = [yp, ys]
    for sp, ss in zip(prompt_states, sample_states):
        outs += [sp[None], ss[None]]
    return tuple(outs)
```

```python
import functools

import numpy as np
import jax
import jax.numpy as jnp
from jax import lax
from jax.experimental import pallas as pl
from jax.experimental.pallas import tpu as pltpu

F32 = jnp.float32
BF16 = jnp.bfloat16

D_MODEL = 2048
D_INNER = 4096
SSM_HEADS = 64
SSM_HEAD_DIM = 64
SSM_GROUPS = 8
HEADS_PER_GROUP = SSM_HEADS // SSM_GROUPS
GROUP_WIDTH = D_INNER // SSM_GROUPS
SSM_STATE = 128
SSM_CHUNK = 128
SSM_CONV = 4
BC_DIM = 2 * SSM_GROUPS * SSM_STATE
XBC_DIM = D_INNER + BC_DIM
SC_DIM = D_MODEL
SC_CONV = 3
D_FF = 5632
FFN_CONV = 3
PLE_DIM = 256
EPS = 1e-6
DEC_SEQ = 8
PROJ_DIM = D_INNER + XBC_DIM + 5 * D_MODEL
DT_PAD = 128

SUBLANES = 8
STATE_SLOTS = 4
LOG2E = 1.4426950408889634
VMEM_LIMIT = 52 * 1024 * 1024


def _cparams(sem):
    return pltpu.CompilerParams(dimension_semantics=sem, vmem_limit_bytes=VMEM_LIMIT)


def _dot(a, b):
    return jnp.dot(a, b, preferred_element_type=F32)


def _dot_nt(a, b):
    return lax.dot_general(a, b, (((1,), (1,)), ((), ())), preferred_element_type=F32)


def _sigmoid(x):
    return 0.5 * jnp.tanh(0.5 * x) + 0.5


def _silu(x):
    h = 0.5 * x
    return h * jnp.tanh(h) + h


def _softplus(x):
    return jnp.maximum(x, 0.0) + jnp.log(1.0 + jnp.exp(-jnp.abs(x)))


def _rms(x, g):
    ms = jnp.mean(x * x, axis=-1, keepdims=True)
    return x * lax.rsqrt(ms + EPS) * g


def _split_bf16(a, terms):
    parts = []
    r = a
    for _ in range(terms - 1):
        p = r.astype(BF16)
        parts.append(p)
        r = r - p.astype(F32)
    parts.append(r.astype(BF16))
    return parts


def _dot_sel_right(a, sel, terms):
    out = None
    for p in _split_bf16(a, terms):
        d = _dot(p, sel)
        out = d if out is None else out + d
    return out


def _dot_sel_left(sel, a, terms):
    out = None
    for p in _split_bf16(a, terms):
        d = _dot(sel, p)
        out = d if out is None else out + d
    return out


def _shift_long(x, halo, k):
    r = pltpu.roll(x, k, 0)
    h = pltpu.roll(halo, k, 0)
    row = lax.broadcasted_iota(jnp.int32, halo.shape, 0)
    first = jnp.where(row < k, h, r[:SUBLANES])
    return jnp.concatenate([first, r[SUBLANES:]], axis=0)


def _shift_short(x, past, k, taps):
    n = x.shape[0]
    r = pltpu.roll(x, k, 0)
    s = (n + k - (taps - 1)) % n
    p = past if s == 0 else pltpu.roll(past, s, 0)
    t = lax.broadcasted_iota(jnp.int32, x.shape, 0) & (DEC_SEQ - 1)
    return jnp.where(t < k, p, r)


def _past_rows(state):
    n, k, c = state.shape
    t = lax.broadcasted_iota(jnp.int32, (n, DEC_SEQ, c), 1)
    out = jnp.zeros((n, DEC_SEQ, c), F32)
    for r in range(k):
        out = jnp.where(t == r, jnp.broadcast_to(state[:, r:r + 1, :], (n, DEC_SEQ, c)), out)
    return out.reshape(n * DEC_SEQ, c)


def _seq_tails(x, taps):
    rows, width = x.shape
    return x.reshape(rows // DEC_SEQ, DEC_SEQ, width)[:, DEC_SEQ - (taps - 1):, :]


def _causal_conv(x, w, shift):
    taps = w.shape[0]
    y = w[taps - 1:taps] * x
    for k in range(1, taps):
        y = y + w[taps - 1 - k:taps - k] * shift(k)
    return y


def _in_proj_kernel(x_ref, g_ref, w_ref, wdt_ref, proj_ref, dt_ref, n1_scr, *, sub):
    @pl.when(pl.program_id(1) == 0)
    def _():
        n1 = _rms(x_ref[...], g_ref[...]).astype(BF16)
        n1_scr[...] = n1
        dt_ref[...] = _dot_nt(n1, wdt_ref[...].astype(BF16))

    n1 = n1_scr[...]
    for c in range(w_ref.shape[0] // sub):
        rs = slice(c * sub, (c + 1) * sub)
        proj_ref[:, rs] = _dot_nt(n1, w_ref[rs, :].astype(BF16)).astype(BF16)


def _in_proj(x, g, w_t, tm, tn):
    m = x.shape[0]
    dt_lo = D_INNER + XBC_DIM
    dt_hi = dt_lo + SSM_HEADS
    n_lo = dt_lo // tn
    w_row = lambda i, j: (
        pl.multiple_of(jnp.where(j < n_lo, j * tn, dt_hi + (j - n_lo) * tn), SSM_HEADS), 0)
    return pl.pallas_call(
        functools.partial(_in_proj_kernel, sub=256),
        grid=(m // tm, PROJ_DIM // tn),
        in_specs=[
            pl.BlockSpec((tm, D_MODEL), lambda i, j: (i, 0), pipeline_mode=pl.Buffered(1)),
            pl.BlockSpec((1, D_MODEL), lambda i, j: (0, 0)),
            pl.BlockSpec((pl.Element(tn), pl.Element(D_MODEL)), w_row),
            pl.BlockSpec((DT_PAD, D_MODEL), lambda i, j: (dt_lo // DT_PAD, 0)),
        ],
        out_specs=[
            pl.BlockSpec((tm, tn), lambda i, j: (i, j)),
            pl.BlockSpec((tm, DT_PAD), lambda i, j: (i, 0)),
        ],
        out_shape=[
            jax.ShapeDtypeStruct((m, PROJ_DIM), BF16),
            jax.ShapeDtypeStruct((m, DT_PAD), F32),
        ],
        scratch_shapes=[pltpu.VMEM((tm, D_MODEL), BF16)],
        compiler_params=_cparams(("parallel", "arbitrary")),
        name="in_proj",
    )(x, g, w_t, w_t)


def _ssd_consts(q, short):
    l = np.arange(q)[:, None]
    s = np.arange(q)[None, :]
    if short:
        tri = (l // DEC_SEQ == s // DEC_SEQ) & (s <= l)
        end = s == (l // DEC_SEQ) * DEC_SEQ + DEC_SEQ - 1
    else:
        tri = s <= l
        end = np.broadcast_to(s == q - 1, (q, q))
    mask = np.where(tri, 0.0, -np.inf).astype(np.float32)
    h = np.arange(128)[None, :, None]
    g = np.arange(SSM_GROUPS)[:, None, None]
    c = np.arange(GROUP_WIDTH)[None, None, :]
    expand = h == g * HEADS_PER_GROUP + c // SSM_HEAD_DIM
    return (jnp.asarray(tri, BF16), jnp.asarray(end, BF16), jnp.asarray(mask),
            jnp.asarray(expand, BF16))


def _ssd_kernel(*refs, short, q, seqs):
    z_ref, xs_ref, bc_ref, dt_ref, tri_ref, end_ref, mask_ref, exp_ref = refs[:8]
    refs = refs[8:]
    cwx_ref, cbx_ref, cwb_ref, cbb_ref, dtb_ref, a_ref, dexp_ref, gn_ref = refs[:8]
    refs = refs[8:]
    past_xs_ref = past_bc_ref = tail_xs = tail_bc = None
    if short:
        past_xs_ref, past_bc_ref, h0_ref = refs[:3]
        refs = refs[3:]
    y_ref, ht_ref = refs[:2]
    refs = refs[2:]
    if short:
        xtail_ref = refs[0]
        refs = refs[1:]
    xs_scr, b_scr, c_scr, y_scr, ea_scr, we_scr, aend_scr, acumt_scr, arow_scr = refs[:9]
    refs = refs[9:]
    if short:
        eae_scr, xwt_scr, h_ring, h_sem = refs
    else:
        h_scr, tail_xs, tail_bc = refs

    step = pl.program_id(1)
    nstep = pl.num_programs(1)

    if short:
        t = pl.program_id(0) * nstep + step
        total = pl.num_programs(0) * nstep

        def state_copy(k):
            slot = k % STATE_SLOTS
            return pltpu.make_async_copy(h0_ref.at[pl.ds(k * seqs, seqs)], h_ring.at[slot],
                                         h_sem.at[slot])

        @pl.when(t == 0)
        def _():
            for k in range(STATE_SLOTS - 1):
                state_copy(k).start()

        @pl.when(t + STATE_SLOTS - 1 < total)
        def _():
            state_copy(t + STATE_SLOTS - 1).start()

    def conv_block(src_ref, past_ref, tail, cw, cb_ref, cs):
        xb = src_ref[:, cs]
        x = xb.astype(F32)
        if short:
            past = _past_rows(past_ref[:, :, cs])
            shift = lambda k: _shift_short(x, past, k, SSM_CONV)
        else:
            halo = tail[:, cs]
            shift = lambda k: _shift_long(x, halo, k)
            tail[:, cs] = x[q - SUBLANES:]
        return _silu(cb_ref[:, cs] + _causal_conv(x, cw[:, cs], shift))

    def conv_group(g):
        cs = slice(g * GROUP_WIDTH, (g + 1) * GROUP_WIDTH)
        bs = slice(g * SSM_STATE, (g + 1) * SSM_STATE)
        cc = slice((SSM_GROUPS + g) * SSM_STATE, (SSM_GROUPS + g + 1) * SSM_STATE)
        cwx = cwx_ref[...]
        cwb = cwb_ref[...]
        return (conv_block(xs_ref, past_xs_ref, tail_xs, cwx, cbx_ref, cs),
                conv_block(bc_ref, past_bc_ref, tail_bc, cwb, cbb_ref, bs),
                conv_block(bc_ref, past_bc_ref, tail_bc, cwb, cbb_ref, cc))

    def dt_prelude():
        dt = _softplus(dt_ref[...] + dtb_ref[...])
        acum = _dot_sel_left(tri_ref[...], dt * a_ref[...], 3)
        aend = _dot_sel_left(end_ref[...], acum, 3)
        acumt = acum.T
        acumt_scr[...] = acumt * LOG2E
        arow_scr[...] = (acumt - jnp.log(dt.T)) * LOG2E
        aend_scr[...] = aend
        ea_scr[...] = jnp.exp(acum)
        we_scr[...] = jnp.exp(aend - acum) * dt

    def diag_group(g, xs, bg, cg):
        cb = _dot_nt(cg.astype(BF16), bg.astype(BF16))
        mask = mask_ref[...]
        lane = lax.broadcasted_iota(jnp.int32, (q, 128), 1)
        ys = []
        for pair in range(HEADS_PER_GROUP // 2):
            ws = []
            for half in range(2):
                h = g * HEADS_PER_GROUP + 2 * pair + half
                colb = jnp.broadcast_to(acumt_scr[pl.ds(h, 1), :], (q, q)).T
                seg = colb - arow_scr[pl.ds(h, 1), :] + mask
                ws.append((cb * jnp.exp2(seg)).astype(BF16))
            xp = xs[:, pair * 128:(pair + 1) * 128]
            top = jnp.where(lane < SSM_HEAD_DIM, xp, 0.0).astype(BF16)
            bot = jnp.where(lane >= SSM_HEAD_DIM, xp, 0.0).astype(BF16)
            ys.append(_dot(jnp.concatenate(ws, axis=1), jnp.concatenate([top, bot], axis=0)))
        return jnp.concatenate(ys, axis=1)

    def finish_group(g, y, xs):
        cs = slice(g * GROUP_WIDTH, (g + 1) * GROUP_WIDTH)
        y = y + dexp_ref[:, cs] * xs
        y = y * _silu(z_ref[:, cs].astype(F32))
        ms = jnp.mean(y * y, axis=-1, keepdims=True)
        y_ref[:, cs] = (y * lax.rsqrt(ms + EPS) * gn_ref[:, cs]).astype(BF16)

    if not short:
        @pl.when(step == 0)
        def _():
            h_scr[...] = jnp.zeros_like(h_scr)
            tail_xs[...] = jnp.zeros_like(tail_xs)
            tail_bc[...] = jnp.zeros_like(tail_bc)

        dt_prelude()
        for g in range(SSM_GROUPS):
            xs_scr[g], b_scr[g], c_scr[g] = conv_group(g)
        for g in range(SSM_GROUPS):
            xs, bg, cg = xs_scr[g], b_scr[g], c_scr[g]
            expand = exp_ref[g]
            y_diag = diag_group(g, xs, bg, cg)
            hg = h_scr[g]
            y_off = _dot(cg.astype(BF16), hg.astype(BF16))
            y_scr[g] = y_diag + y_off * _dot_sel_right(ea_scr[...], expand, 2)
            xw = (xs * _dot_sel_right(we_scr[...], expand, 2)).astype(BF16)
            st = _dot(bg.T.astype(BF16), xw)
            dec = jnp.exp(_dot_sel_right(aend_scr[:SUBLANES, :], expand, 3))[:1]
            h_scr[g] = dec * hg + st
        for g in range(SSM_GROUPS):
            finish_group(g, y_scr[g], xs_scr[g])

        @pl.when(step == nstep - 1)
        def _():
            for g in range(SSM_GROUPS):
                ht_ref[0, g * GROUP_WIDTH:(g + 1) * GROUP_WIDTH, :] = h_scr[g].T
    else:
        @pl.when(step == 0)
        def _():
            xtail_ref[:, :, :D_INNER] = _seq_tails(xs_ref[...].astype(F32), SSM_CONV)
            xtail_ref[:, :, D_INNER:] = _seq_tails(bc_ref[...].astype(F32), SSM_CONV)
            dt_prelude()
            for g in range(SSM_GROUPS):
                xs, bg, cg = conv_group(g)
                expand = exp_ref[g]
                xs_scr[g] = xs
                b_scr[g] = bg
                c_scr[g] = cg
                y_scr[g] = diag_group(g, xs, bg, cg)
                eae_scr[g] = _dot_sel_right(ea_scr[...], expand, 2)
                xwt_scr[g] = (xs * _dot_sel_right(we_scr[...], expand, 2)).T.astype(BF16)

        state_copy(t).wait()
        h_cur = h_ring.at[t % STATE_SLOTS]
        for u in range(seqs):
            sidx = step * seqs + u
            row = lax.broadcasted_iota(jnp.int32, (q, 128), 0)
            onehot = jnp.where(row == sidx * DEC_SEQ + DEC_SEQ - 1, 1.0, 0.0).astype(BF16)
            cdec = jnp.exp2(_dot_sel_right(acumt_scr[...], onehot, 3))
            in_seq = (row // DEC_SEQ) == sidx
            seq_rows = pl.ds(pl.multiple_of(sidx * DEC_SEQ, DEC_SEQ), DEC_SEQ)
            for g in range(SSM_GROUPS):
                rows = slice(g * GROUP_WIDTH, (g + 1) * GROUP_WIDTH)
                hg = h_cur[u, rows, :]
                cj = c_scr[g, seq_rows, :].astype(BF16)
                y_off = _dot_nt(cj, hg.astype(BF16))
                y_scr[g, seq_rows, :] = y_scr[g, seq_rows, :] + y_off * eae_scr[g, seq_rows, :]
                bj = jnp.where(in_seq, b_scr[g], 0.0).astype(BF16)
                st = _dot(xwt_scr[g], bj)
                dec = [jnp.broadcast_to(cdec[g * HEADS_PER_GROUP + r:g * HEADS_PER_GROUP + r + 1],
                                        (SSM_HEAD_DIM, 128)) for r in range(HEADS_PER_GROUP)]
                ht_ref[u, rows, :] = jnp.concatenate(dec, axis=0) * hg + st

        @pl.when(step == nstep - 1)
        def _():
            for g in range(SSM_GROUPS):
                finish_group(g, y_scr[g], xs_scr[g])


def _ssd(proj, dt_raw, consts, wts, nseq, short, past=None, h0=None):
    m = proj.shape[0]
    q = SSM_CHUNK
    nblk = m // q
    if short:
        seqs = 2
        steps = q // DEC_SEQ // seqs
        grid = (nblk, steps)
        blk = lambda a, b: a
        st_idx = lambda a, b: (a * steps + b, 0, 0)
    else:
        seqs = 1
        steps = m // nseq // q
        grid = (nseq, steps)
        blk = lambda a, b: a * steps + b
        st_idx = lambda a, b: (a, 0, 0)
    const2 = lambda shape: pl.BlockSpec(shape, lambda a, b: (0, 0))
    const3 = lambda shape: pl.BlockSpec(shape, lambda a, b: (0, 0, 0))
    in_specs = [
        pl.BlockSpec((q, D_INNER), lambda a, b: (blk(a, b), 0)),
        pl.BlockSpec((q, D_INNER), lambda a, b: (blk(a, b), 1)),
        pl.BlockSpec((q, BC_DIM), lambda a, b: (blk(a, b), 2 * D_INNER // BC_DIM)),
        pl.BlockSpec((q, DT_PAD), lambda a, b: (blk(a, b), 0)),
        const2((q, q)), const2((q, q)), const2((q, q)),
        const3((SSM_GROUPS, 128, GROUP_WIDTH)),
        const2((SSM_CONV, D_INNER)), const2((1, D_INNER)),
        const2((SSM_CONV, BC_DIM)), const2((1, BC_DIM)),
        const2((1, DT_PAD)), const2((1, DT_PAD)),
        const2((1, D_INNER)), const2((1, D_INNER)),
    ]
    args = [proj, proj, proj, dt_raw, *consts, *wts]
    state_spec = pl.BlockSpec((seqs, D_INNER, SSM_STATE), st_idx)
    scratch = [
        pltpu.VMEM((SSM_GROUPS, q, GROUP_WIDTH), F32),
        pltpu.VMEM((SSM_GROUPS, q, SSM_STATE), F32),
        pltpu.VMEM((SSM_GROUPS, q, SSM_STATE), F32),
        pltpu.VMEM((SSM_GROUPS, q, GROUP_WIDTH), F32),
        pltpu.VMEM((q, 128), F32), pltpu.VMEM((q, 128), F32), pltpu.VMEM((q, 128), F32),
        pltpu.VMEM((128, q), F32), pltpu.VMEM((128, q), F32),
    ]
    out_specs = [pl.BlockSpec((q, D_INNER), lambda a, b: (blk(a, b), 0)), state_spec]
    out_shape = [jax.ShapeDtypeStruct((m, D_INNER), BF16),
                 jax.ShapeDtypeStruct((nseq, D_INNER, SSM_STATE), F32)]
    if short:
        per_blk = q // DEC_SEQ
        in_specs += [
            pl.BlockSpec((per_blk, SSM_CONV - 1, D_INNER), lambda a, b: (a, 0, 0)),
            pl.BlockSpec((per_blk, SSM_CONV - 1, BC_DIM), lambda a, b: (a, 0, D_INNER // BC_DIM)),
            pl.BlockSpec(memory_space=pl.ANY),
        ]
        args += [past, past, h0]
        out_specs.append(pl.BlockSpec((per_blk, SSM_CONV - 1, XBC_DIM), lambda a, b: (a, 0, 0)))
        out_shape.append(jax.ShapeDtypeStruct((nseq, SSM_CONV - 1, XBC_DIM), F32))
        scratch += [pltpu.VMEM((SSM_GROUPS, q, GROUP_WIDTH), F32),
                    pltpu.VMEM((SSM_GROUPS, GROUP_WIDTH, q), BF16),
                    pltpu.VMEM((STATE_SLOTS, seqs, D_INNER, SSM_STATE), F32),
                    pltpu.SemaphoreType.DMA((STATE_SLOTS,))]
    else:
        scratch += [pltpu.VMEM((SSM_GROUPS, SSM_STATE, GROUP_WIDTH), F32),
                    pltpu.VMEM((SUBLANES, D_INNER), F32),
                    pltpu.VMEM((SUBLANES, BC_DIM), F32)]
    return pl.pallas_call(
        functools.partial(_ssd_kernel, short=short, q=q, seqs=seqs),
        grid=grid,
        in_specs=in_specs,
        out_specs=out_specs,
        out_shape=out_shape,
        scratch_shapes=scratch,
        compiler_params=_cparams(("arbitrary", "arbitrary")),
        name="ssd_short" if short else "ssd_long",
    )(*args)


def _mix_kernel(*refs, short, tiles_per_seq):
    yn_ref, scb_ref, scc_ref, sch_ref, ga_ref, gb_ref, cw_ref, wa_ref, wb_ref = refs[:9]
    refs = refs[9:]
    if short:
        past_ref = refs[0]
        refs = refs[1:]
    mixed_ref, utail_ref = refs[:2]
    tm = yn_ref.shape[0]

    u = scc_ref[...].astype(F32) * sch_ref[...].astype(F32)
    if short:
        past = _past_rows(past_ref[...])
        shift = lambda k: _shift_short(u, past, k, SC_CONV)
        utail_ref[...] = _seq_tails(u, SC_CONV)
    else:
        halo_scr = refs[2]
        first = pl.program_id(0) % tiles_per_seq == 0
        halo = jnp.where(first, 0.0, halo_scr[...])
        shift = lambda k: _shift_long(u, halo, k)
        halo_scr[...] = u[tm - SUBLANES:]
        utail_ref[...] = u[tm - SUBLANES:]
    s = (scb_ref[...].astype(F32) * _causal_conv(u, cw_ref[...], shift)).astype(BF16)

    a = _dot(yn_ref[...], wa_ref[...])
    b = _dot(s, wb_ref[...])
    mixed = _sigmoid(ga_ref[...].astype(F32)) * a + _sigmoid(gb_ref[...].astype(F32)) * b
    mixed_ref[...] = mixed.astype(BF16)


def _mix(yn, proj, cw, wa, wb, seq_len, tm, short, past=None):
    m = yn.shape[0]
    col = lambda c: (D_INNER + XBC_DIM) // D_MODEL + c
    rows = lambda c: pl.BlockSpec((tm, D_MODEL), lambda i: (i, col(c)))
    resident = lambda shape: pl.BlockSpec(shape, lambda i: (0, 0), pipeline_mode=pl.Buffered(1))
    in_specs = [
        pl.BlockSpec((tm, D_INNER), lambda i: (i, 0)),
        rows(0), rows(1), rows(2), rows(3), rows(4),
        resident((SC_CONV, SC_DIM)),
        resident((D_INNER, D_MODEL)),
        resident((SC_DIM, D_MODEL)),
    ]
    args = [yn, proj, proj, proj, proj, proj, cw, wa, wb]
    scratch = []
    if short:
        seq_blk = (tm // DEC_SEQ, SC_CONV - 1, SC_DIM)
        in_specs.append(pl.BlockSpec(seq_blk, lambda i: (i, 0, 0)))
        args.append(past)
        tail_spec = pl.BlockSpec(seq_blk, lambda i: (i, 0, 0))
        tail_shape = (m // DEC_SEQ, SC_CONV - 1, SC_DIM)
    else:
        scratch.append(pltpu.VMEM((SUBLANES, SC_DIM), F32))
        tail_spec = pl.BlockSpec((SUBLANES, SC_DIM), lambda i: (i, 0))
        tail_shape = (m // tm * SUBLANES, SC_DIM)
    return pl.pallas_call(
        functools.partial(_mix_kernel, short=short, tiles_per_seq=max(seq_len // tm, 1)),
        grid=(m // tm,),
        in_specs=in_specs,
        out_specs=[pl.BlockSpec((tm, D_MODEL), lambda i: (i, 0)), tail_spec],
        out_shape=[jax.ShapeDtypeStruct((m, D_MODEL), BF16),
                   jax.ShapeDtypeStruct(tail_shape, F32)],
        scratch_shapes=scratch,
        compiler_params=_cparams(("arbitrary",)),
        name="mix",
    )(*args)


def _row_halves(ref):
    half = ref.shape[0] // 2
    return [slice(0, half), slice(half, 2 * half)]


def _out_proj_kernel(mixed_ref, x_ref, wo_ref, g_ref, x1_ref, n2_ref):
    for rs in _row_halves(x_ref):
        x1 = x_ref[rs, :] + _dot(mixed_ref[rs, :], wo_ref[...])
        x1_ref[rs, :] = x1
        n2_ref[rs, :] = _rms(x1, g_ref[...]).astype(BF16)


def _out_proj(mixed, x, wo, g, tm):
    m = x.shape[0]
    row = lambda i: (i, 0)
    const = lambda i: (0, 0)
    return pl.pallas_call(
        _out_proj_kernel,
        grid=(m // tm,),
        in_specs=[pl.BlockSpec((tm, D_MODEL), row), pl.BlockSpec((tm, D_MODEL), row),
                  pl.BlockSpec((D_MODEL, D_MODEL), const, pipeline_mode=pl.Buffered(1)),
                  pl.BlockSpec((1, D_MODEL), const)],
        out_specs=[pl.BlockSpec((tm, D_MODEL), row), pl.BlockSpec((tm, D_MODEL), row)],
        out_shape=[jax.ShapeDtypeStruct((m, D_MODEL), F32),
                   jax.ShapeDtypeStruct((m, D_MODEL), BF16)],
        compiler_params=_cparams(("parallel",)),
        name="out_proj",
    )(mixed, x, wo, g)


def _ffn_up_kernel(*refs, short, tiles_per_seq, splits):
    n2_ref, wg_ref, wu_ref, cw_ref, cb_ref = refs[:5]
    refs = refs[5:]
    if short:
        past_ref = refs[0]
        refs = refs[1:]
    act_ref, gtail_ref = refs[:2]
    tm = n2_ref.shape[0]
    n2 = n2_ref[...]
    j = pl.program_id(1)
    first = pl.program_id(0) % tiles_per_seq == 0
    lo = 0
    for width in splits:
        cs = slice(lo, lo + width)
        lo += width
        gate = _dot(n2, wg_ref[:, cs].astype(BF16))
        up = _dot(n2, wu_ref[:, cs].astype(BF16))
        if short:
            past = _past_rows(past_ref[:, :, cs])
            shift = lambda k, gate=gate, past=past: _shift_short(gate, past, k, FFN_CONV)
            gtail_ref[:, :, cs] = _seq_tails(gate, FFN_CONV)
        else:
            halo_scr = refs[2]
            halo = jnp.where(first, 0.0, halo_scr[j, :, cs])
            shift = lambda k, gate=gate, halo=halo: _shift_long(gate, halo, k)
            halo_scr[j, :, cs] = gate[tm - SUBLANES:]
            gtail_ref[:, cs] = gate[tm - SUBLANES:]
        conv = cb_ref[:, cs] + _causal_conv(gate, cw_ref[:, cs], shift)
        act_ref[:, cs] = (_silu(conv) * up).astype(BF16)


def _ffn_up(n2, wg, wu, cw, cb, seq_len, tm, tn, splits, short, past=None):
    m = n2.shape[0]
    nj = D_FF // tn
    assert sum(splits) == tn
    in_specs = [
        pl.BlockSpec((tm, D_MODEL), lambda i, j: (i, 0)),
        pl.BlockSpec((D_MODEL, tn), lambda i, j: (0, j)),
        pl.BlockSpec((D_MODEL, tn), lambda i, j: (0, j)),
        pl.BlockSpec((FFN_CONV, tn), lambda i, j: (0, j)),
        pl.BlockSpec((1, tn), lambda i, j: (0, j)),
    ]
    args = [n2, wg, wu, cw, cb]
    scratch = []
    if short:
        tail_spec = pl.BlockSpec((tm // DEC_SEQ, FFN_CONV - 1, tn), lambda i, j: (i, 0, j))
        in_specs.append(tail_spec)
        args.append(past)
        tail_shape = (m // DEC_SEQ, FFN_CONV - 1, D_FF)
    else:
        scratch.append(pltpu.VMEM((nj, SUBLANES, tn), F32))
        tail_spec = pl.BlockSpec((SUBLANES, tn), lambda i, j: (i, j))
        tail_shape = (m // tm * SUBLANES, D_FF)
    return pl.pallas_call(
        functools.partial(_ffn_up_kernel, short=short, tiles_per_seq=max(seq_len // tm, 1),
                          splits=splits),
        grid=(m // tm, nj),
        in_specs=in_specs,
        out_specs=[pl.BlockSpec((tm, tn), lambda i, j: (i, j)), tail_spec],
        out_shape=[jax.ShapeDtypeStruct((m, D_FF), BF16),
                   jax.ShapeDtypeStruct(tail_shape, F32)],
        scratch_shapes=scratch,
        compiler_params=_cparams(("arbitrary", "arbitrary")),
        name="ffn_up",
    )(*args)


def _ffn_down_kernel(act_ref, wd_ref, x1_ref, x2_ref):
    x2_ref[...] = x1_ref[...] + _dot(act_ref[...], wd_ref[...])


def _ffn_down(act, wd, x1, tm):
    m = x1.shape[0]
    return pl.pallas_call(
        _ffn_down_kernel,
        grid=(m // tm,),
        in_specs=[pl.BlockSpec((tm, D_FF), lambda i: (i, 0)),
                  pl.BlockSpec((D_FF, D_MODEL), lambda i: (0, 0), pipeline_mode=pl.Buffered(1)),
                  pl.BlockSpec((tm, D_MODEL), lambda i: (i, 0))],
        out_specs=pl.BlockSpec((tm, D_MODEL), lambda i: (i, 0)),
        out_shape=jax.ShapeDtypeStruct((m, D_MODEL), F32),
        compiler_params=_cparams(("parallel",)),
        name="ffn_down",
    )(act, wd, x1)


def _ple_kernel(x2_ref, pe_ref, wple_ref, wgate_ref, gp_ref, gf_ref, y_ref):
    for rs in _row_halves(x2_ref):
        x2 = x2_ref[rs, :]
        n3 = _rms(x2, gp_ref[...]).astype(BF16)
        gate = _sigmoid(_dot(n3, wgate_ref[...]))
        emb = _dot(pe_ref[rs, :].astype(BF16), wple_ref[...])
        y_ref[rs, :] = _rms(x2 + emb * gate, gf_ref[...])


def _ple(x2, pe, wple, wgate, gp, gf, tm):
    m = x2.shape[0]
    row = lambda i: (i, 0)
    const = lambda i: (0, 0)
    return pl.pallas_call(
        _ple_kernel,
        grid=(m // tm,),
        in_specs=[pl.BlockSpec((tm, D_MODEL), row), pl.BlockSpec((tm, PLE_DIM), row),
                  pl.BlockSpec((PLE_DIM, D_MODEL), const),
                  pl.BlockSpec((D_MODEL, D_MODEL), const, pipeline_mode=pl.Buffered(1)),
                  pl.BlockSpec((1, D_MODEL), const), pl.BlockSpec((1, D_MODEL), const)],
        out_specs=pl.BlockSpec((tm, D_MODEL), row),
        out_shape=jax.ShapeDtypeStruct((m, D_MODEL), F32),
        compiler_params=_cparams(("parallel",)),
        name="ple",
    )(x2, pe, wple, wgate, gp, gf)


def _tile(m, want):
    return min(m, want)


def _layer(x, pe, w, short, states=None):
    nseq, seq_len, _ = x.shape
    m = nseq * seq_len
    x2d = x.reshape(m, D_MODEL)
    pe2d = pe.reshape(m, PLE_DIM)
    if short:
        ssm0, xbc0, sc0, ffn0 = states
        h0 = ssm0.reshape(nseq, D_INNER, SSM_STATE)

    proj, dt_raw = _in_proj(x2d, w["norm_mix"], w["w_in_t"], _tile(m, 1024), 1024)

    ssd_w = (w["cw_xs"], w["cb_xs"], w["cw_bc"], w["cb_bc"], w["dt_bias"], w["a_neg"],
             w["d_exp"], w["ssm_norm"])
    consts = _ssd_consts(SSM_CHUNK, short)
    if short:
        yn, ssm_new, xbc_new = _ssd(proj, dt_raw, consts, ssd_w, nseq, True, xbc0, h0)
    else:
        yn, ssm_new = _ssd(proj, dt_raw, consts, ssd_w, nseq, False)
        proj3 = proj.reshape(nseq, seq_len, PROJ_DIM)
        xbc_new = proj3[:, seq_len - (SSM_CONV - 1):, D_INNER:D_INNER + XBC_DIM].astype(F32)

    tm = _tile(m, 512)
    tm_mix = _tile(m, 256)
    mixed, u_tail = _mix(yn, proj, w["sc_conv_w"], w["w_ssm_out"], w["w_sc_out"], seq_len,
                         tm_mix, short, sc0 if short else None)
    x1, n2 = _out_proj(mixed, x2d, w["w_o"], w["norm_ffn"], tm)

    tm_ffn = _tile(m, 1024)
    act, g_tail = _ffn_up(n2, w["w_ffn_gate"], w["w_ffn_up"], w["ffn_conv_w"], w["ffn_conv_b"],
                          seq_len, tm_ffn, 512, (256, 256), short, ffn0 if short else None)
    x2 = _ffn_down(act, w["w_ffn_down"], x1, _tile(m, 256))
    y = _ple(x2, pe2d, w["w_ple"], w["w_ple_gate"], w["norm_ple"], w["norm_final"], tm)

    def last_rows(tail, tile_rows, taps, width):
        if short:
            return tail
        per_seq = seq_len // tile_rows
        t = tail.reshape(nseq, per_seq, SUBLANES, width)[:, per_seq - 1]
        return t[:, SUBLANES - (taps - 1):]

    sc_new = last_rows(u_tail, tm_mix, SC_CONV, SC_DIM)
    ffn_new = last_rows(g_tail, tm_ffn, FFN_CONV, D_FF)
    ssm_new = ssm_new.reshape(nseq, SSM_HEADS, SSM_HEAD_DIM, SSM_STATE)
    return y.reshape(nseq, seq_len, D_MODEL), ssm_new, xbc_new, sc_new, ffn_new


def _prep_weights(norm_mix, w_in, ssm_conv_w, ssm_conv_b, ssm_dt_bias, ssm_a_log, ssm_d, ssm_norm,
                  w_ssm_out, sc_conv_w, w_sc_out, w_o, norm_ffn, w_ffn_gate, w_ffn_up,
                  ffn_conv_w, ffn_conv_b, w_ffn_down, norm_ple, w_ple, w_ple_gate, norm_final):
    row = lambda v: v.reshape(1, -1).astype(F32)
    pad_heads = lambda v: jnp.pad(v.astype(F32), (0, DT_PAD - SSM_HEADS)).reshape(1, DT_PAD)
    return {
        "norm_mix": row(norm_mix),
        "w_in_t": w_in.T,
        "cw_xs": ssm_conv_w[:, :D_INNER].astype(F32),
        "cb_xs": row(ssm_conv_b[:D_INNER]),
        "cw_bc": ssm_conv_w[:, D_INNER:].astype(F32),
        "cb_bc": row(ssm_conv_b[D_INNER:]),
        "dt_bias": pad_heads(ssm_dt_bias),
        "a_neg": pad_heads(-jnp.exp(ssm_a_log.astype(F32))),
        "d_exp": row(jnp.repeat(ssm_d, SSM_HEAD_DIM)),
        "ssm_norm": row(ssm_norm),
        "w_ssm_out": w_ssm_out.astype(BF16),
        "sc_conv_w": sc_conv_w.astype(F32),
        "w_sc_out": w_sc_out.astype(BF16),
        "w_o": w_o.astype(BF16),
        "norm_ffn": row(norm_ffn),
        "w_ffn_gate": w_ffn_gate,
        "w_ffn_up": w_ffn_up,
        "ffn_conv_w": ffn_conv_w.astype(F32),
        "ffn_conv_b": row(ffn_conv_b),
        "w_ffn_down": w_ffn_down.astype(BF16),
        "norm_ple": row(norm_ple),
        "w_ple": w_ple.astype(BF16),
        "w_ple_gate": w_ple_gate.astype(BF16),
        "norm_final": row(norm_final),
    }


def kernel(x_prompt, x_sample, state_ssm, state_ssm_conv, state_short_conv, state_ffn_conv, p_prompt, p_sample, norm_mix, w_in, ssm_conv_w, ssm_conv_b, ssm_dt_bias, ssm_a_log, ssm_d, ssm_norm, w_ssm_out, sc_conv_w, w_sc_out, w_o, norm_ffn, w_ffn_gate, w_ffn_up, ffn_conv_w, ffn_conv_b, w_ffn_down, norm_ple, w_ple, w_ple_gate, norm_final):
    depth = w_in.shape[0]
    assert depth == 1, "the final norm is fused into the layer's last kernel"
    assert x_sample.shape[1] == DEC_SEQ
    layer_w = (norm_mix, w_in, ssm_conv_w, ssm_conv_b, ssm_dt_bias, ssm_a_log, ssm_d, ssm_norm,
               w_ssm_out, sc_conv_w, w_sc_out, w_o, norm_ffn, w_ffn_gate, w_ffn_up,
               ffn_conv_w, ffn_conv_b, w_ffn_down, norm_ple, w_ple, w_ple_gate)
    w = _prep_weights(*[t[0] for t in layer_w], norm_final)
    yp, *prompt_states = _layer(x_prompt, p_prompt[0], w, short=False)
    ys, *sample_states = _layer(
        x_sample, p_sample[0], w, short=True,
        states=(state_ssm[0], state_ssm_conv[0], state_short_conv[0], state_ffn_conv[0]))
    outs = [yp, ys]
    for sp, ss in zip(prompt_states, sample_states):
        outs += [sp[None], ss[None]]
    return tuple(outs)
```

```python
import functools

import numpy as np
import jax
import jax.numpy as jnp
from jax import lax
from jax.experimental import pallas as pl
from jax.experimental.pallas import tpu as pltpu

F32 = jnp.float32
BF16 = jnp.bfloat16

D_MODEL = 2048
D_INNER = 4096
SSM_HEADS = 64
SSM_HEAD_DIM = 64
SSM_GROUPS = 8
HEADS_PER_GROUP = SSM_HEADS // SSM_GROUPS
GROUP_WIDTH = D_INNER // SSM_GROUPS
SSM_STATE = 128
SSM_CHUNK = 128
SSM_CONV = 4
BC_DIM = 2 * SSM_GROUPS * SSM_STATE
XBC_DIM = D_INNER + BC_DIM
SC_DIM = D_MODEL
SC_CONV = 3
D_FF = 5632
FFN_CONV = 3
PLE_DIM = 256
EPS = 1e-6
DEC_SEQ = 8
PROJ_DIM = D_INNER + XBC_DIM + 5 * D_MODEL
DT_PAD = 128

SUBLANES = 8
STATE_SLOTS = 3
LOG2E = 1.4426950408889634
VMEM_LIMIT = 52 * 1024 * 1024


def _cparams(sem):
    return pltpu.CompilerParams(dimension_semantics=sem, vmem_limit_bytes=VMEM_LIMIT)


def _dot(a, b):
    return jnp.dot(a, b, preferred_element_type=F32)


def _dot_nt(a, b):
    return lax.dot_general(a, b, (((1,), (1,)), ((), ())), preferred_element_type=F32)


def _sigmoid(x):
    return 0.5 * jnp.tanh(0.5 * x) + 0.5


def _silu(x):
    h = 0.5 * x
    return h * jnp.tanh(h) + h


def _softplus(x):
    return jnp.maximum(x, 0.0) + jnp.log(1.0 + jnp.exp(-jnp.abs(x)))


def _rms(x, g):
    ms = jnp.mean(x * x, axis=-1, keepdims=True)
    return x * lax.rsqrt(ms + EPS) * g


def _split_bf16(a, terms):
    parts = []
    r = a
    for _ in range(terms - 1):
        p = r.astype(BF16)
        parts.append(p)
        r = r - p.astype(F32)
    parts.append(r.astype(BF16))
    return parts


def _dot_sel_right(a, sel, terms):
    out = None
    for p in _split_bf16(a, terms):
        d = _dot(p, sel)
        out = d if out is None else out + d
    return out


def _dot_sel_left(sel, a, terms):
    out = None
    for p in _split_bf16(a, terms):
        d = _dot(sel, p)
        out = d if out is None else out + d
    return out


def _shift_long(x, halo, k):
    r = pltpu.roll(x, k, 0)
    h = pltpu.roll(halo, k, 0)
    row = lax.broadcasted_iota(jnp.int32, halo.shape, 0)
    first = jnp.where(row < k, h, r[:SUBLANES])
    return jnp.concatenate([first, r[SUBLANES:]], axis=0)


def _shift_short(x, past, k, taps):
    n = x.shape[0]
    r = pltpu.roll(x, k, 0)
    s = (n + k - (taps - 1)) % n
    p = past if s == 0 else pltpu.roll(past, s, 0)
    t = lax.broadcasted_iota(jnp.int32, x.shape, 0) & (DEC_SEQ - 1)
    return jnp.where(t < k, p, r)


def _past_rows(state):
    n, k, c = state.shape
    t = lax.broadcasted_iota(jnp.int32, (n, DEC_SEQ, c), 1)
    out = jnp.zeros((n, DEC_SEQ, c), F32)
    for r in range(k):
        out = jnp.where(t == r, jnp.broadcast_to(state[:, r:r + 1, :], (n, DEC_SEQ, c)), out)
    return out.reshape(n * DEC_SEQ, c)


def _seq_tails(x, taps):
    rows, width = x.shape
    return x.reshape(rows // DEC_SEQ, DEC_SEQ, width)[:, DEC_SEQ - (taps - 1):, :]


def _causal_conv(x, w, shift):
    taps = w.shape[0]
    y = w[taps - 1:taps] * x
    for k in range(1, taps):
        y = y + w[taps - 1 - k:taps - k] * shift(k)
    return y


def _in_proj_kernel(x_ref, g_ref, w_ref, wdt_ref, proj_ref, dt_ref, n1_scr, *, sub):
    @pl.when(pl.program_id(1) == 0)
    def _():
        n1 = _rms(x_ref[...], g_ref[...]).astype(BF16)
        n1_scr[...] = n1
        dt_ref[...] = _dot_nt(n1, wdt_ref[...].astype(BF16))

    n1 = n1_scr[...]
    for c in range(w_ref.shape[0] // sub):
        rs = slice(c * sub, (c + 1) * sub)
        proj_ref[:, rs] = _dot_nt(n1, w_ref[rs, :].astype(BF16)).astype(BF16)


def _in_proj(x, g, w_t, tm, tn):
    m = x.shape[0]
    dt_lo = D_INNER + XBC_DIM
    dt_hi = dt_lo + SSM_HEADS
    n_lo = dt_lo // tn
    w_row = lambda i, j: (
        pl.multiple_of(jnp.where(j < n_lo, j * tn, dt_hi + (j - n_lo) * tn), SSM_HEADS), 0)
    return pl.pallas_call(
        functools.partial(_in_proj_kernel, sub=256),
        grid=(m // tm, PROJ_DIM // tn),
        in_specs=[
            pl.BlockSpec((tm, D_MODEL), lambda i, j: (i, 0), pipeline_mode=pl.Buffered(1)),
            pl.BlockSpec((1, D_MODEL), lambda i, j: (0, 0)),
            pl.BlockSpec((pl.Element(tn), pl.Element(D_MODEL)), w_row),
            pl.BlockSpec((DT_PAD, D_MODEL), lambda i, j: (dt_lo // DT_PAD, 0)),
        ],
        out_specs=[
            pl.BlockSpec((tm, tn), lambda i, j: (i, j)),
            pl.BlockSpec((tm, DT_PAD), lambda i, j: (i, 0)),
        ],
        out_shape=[
            jax.ShapeDtypeStruct((m, PROJ_DIM), BF16),
            jax.ShapeDtypeStruct((m, DT_PAD), F32),
        ],
        scratch_shapes=[pltpu.VMEM((tm, D_MODEL), BF16)],
        compiler_params=_cparams(("parallel", "arbitrary")),
        name="in_proj",
    )(x, g, w_t, w_t)


def _ssd_consts(q, short):
    l = np.arange(q)[:, None]
    s = np.arange(q)[None, :]
    if short:
        tri = (l // DEC_SEQ == s // DEC_SEQ) & (s <= l)
        end = s == (l // DEC_SEQ) * DEC_SEQ + DEC_SEQ - 1
    else:
        tri = s <= l
        end = np.broadcast_to(s == q - 1, (q, q))
    mask = np.where(tri, 0.0, -np.inf).astype(np.float32)
    h = np.arange(128)[None, :, None]
    g = np.arange(SSM_GROUPS)[:, None, None]
    c = np.arange(GROUP_WIDTH)[None, None, :]
    expand = h == g * HEADS_PER_GROUP + c // SSM_HEAD_DIM
    return (jnp.asarray(tri, BF16), jnp.asarray(end, BF16), jnp.asarray(mask),
            jnp.asarray(expand, BF16))


def _ssd_kernel(*refs, short, q, seqs):
    z_ref, xs_ref, bc_ref, dt_ref, tri_ref, end_ref, mask_ref, exp_ref = refs[:8]
    refs = refs[8:]
    cwx_ref, cbx_ref, cwb_ref, cbb_ref, dtb_ref, a_ref, dexp_ref, gn_ref = refs[:8]
    refs = refs[8:]
    past_xs_ref = past_bc_ref = tail_xs = tail_bc = None
    if short:
        past_xs_ref, past_bc_ref, h0_ref = refs[:3]
        refs = refs[3:]
    y_ref, ht_ref = refs[:2]
    refs = refs[2:]
    if short:
        xtail_ref = refs[0]
        refs = refs[1:]
    xs_scr, b_scr, c_scr, y_scr, ea_scr, we_scr, aend_scr, acumt_scr, arow_scr = refs[:9]
    refs = refs[9:]
    if short:
        eae_scr, xwt_scr, h_ring, h_sem = refs
    else:
        h_scr, tail_xs, tail_bc = refs

    step = pl.program_id(1)
    nstep = pl.num_programs(1)

    if short:
        t = pl.program_id(0) * nstep + step
        total = pl.num_programs(0) * nstep

        def state_copy(k):
            slot = k % STATE_SLOTS
            return pltpu.make_async_copy(h0_ref.at[pl.ds(k * seqs, seqs)], h_ring.at[slot],
                                         h_sem.at[slot])

        @pl.when(t == 0)
        def _():
            for k in range(STATE_SLOTS - 1):
                state_copy(k).start(priority=1)

        @pl.when(t + STATE_SLOTS - 1 < total)
        def _():
            state_copy(t + STATE_SLOTS - 1).start(priority=1)

    def conv_block(src_ref, past_ref, tail, cw, cb_ref, cs):
        xb = src_ref[:, cs]
        x = xb.astype(F32)
        if short:
            past = _past_rows(past_ref[:, :, cs])
            shift = lambda k: _shift_short(x, past, k, SSM_CONV)
        else:
            halo = tail[:, cs]
            shift = lambda k: _shift_long(x, halo, k)
            tail[:, cs] = x[q - SUBLANES:]
        return _silu(cb_ref[:, cs] + _causal_conv(x, cw[:, cs], shift))

    def conv_group(g):
        cs = slice(g * GROUP_WIDTH, (g + 1) * GROUP_WIDTH)
        bs = slice(g * SSM_STATE, (g + 1) * SSM_STATE)
        cc = slice((SSM_GROUPS + g) * SSM_STATE, (SSM_GROUPS + g + 1) * SSM_STATE)
        cwx = cwx_ref[...]
        cwb = cwb_ref[...]
        return (conv_block(xs_ref, past_xs_ref, tail_xs, cwx, cbx_ref, cs),
                conv_block(bc_ref, past_bc_ref, tail_bc, cwb, cbb_ref, bs),
                conv_block(bc_ref, past_bc_ref, tail_bc, cwb, cbb_ref, cc))

    def dt_prelude():
        dt = _softplus(dt_ref[...] + dtb_ref[...])
        acum = _dot_sel_left(tri_ref[...], dt * a_ref[...], 3)
        aend = _dot_sel_left(end_ref[...], acum, 3)
        acumt = acum.T
        acumt_scr[...] = acumt * LOG2E
        arow_scr[...] = (acumt - jnp.log(dt.T)) * LOG2E
        aend_scr[...] = aend
        ea_scr[...] = jnp.exp(acum)
        we_scr[...] = jnp.exp(aend - acum) * dt

    def diag_group(g, xs, bg, cg):
        cb = _dot_nt(cg.astype(BF16), bg.astype(BF16))
        mask = mask_ref[...]
        lane = lax.broadcasted_iota(jnp.int32, (q, 128), 1)
        ys = []
        for pair in range(HEADS_PER_GROUP // 2):
            ws = []
            for half in range(2):
                h = g * HEADS_PER_GROUP + 2 * pair + half
                colb = jnp.broadcast_to(acumt_scr[pl.ds(h, 1), :], (q, q)).T
                seg = colb - arow_scr[pl.ds(h, 1), :] + mask
                ws.append((cb * jnp.exp2(seg)).astype(BF16))
            xp = xs[:, pair * 128:(pair + 1) * 128]
            top = jnp.where(lane < SSM_HEAD_DIM, xp, 0.0).astype(BF16)
            bot = jnp.where(lane >= SSM_HEAD_DIM, xp, 0.0).astype(BF16)
            ys.append(_dot(jnp.concatenate(ws, axis=1), jnp.concatenate([top, bot], axis=0)))
        return jnp.concatenate(ys, axis=1)

    def finish_group(g, y, xs):
        cs = slice(g * GROUP_WIDTH, (g + 1) * GROUP_WIDTH)
        y = y + dexp_ref[:, cs] * xs
        y = y * _silu(z_ref[:, cs].astype(F32))
        ms = jnp.mean(y * y, axis=-1, keepdims=True)
        y_ref[:, cs] = (y * lax.rsqrt(ms + EPS) * gn_ref[:, cs]).astype(BF16)

    if not short:
        @pl.when(step == 0)
        def _():
            h_scr[...] = jnp.zeros_like(h_scr)
            tail_xs[...] = jnp.zeros_like(tail_xs)
            tail_bc[...] = jnp.zeros_like(tail_bc)

        dt_prelude()
        for g in range(SSM_GROUPS):
            xs_scr[g], b_scr[g], c_scr[g] = conv_group(g)
        for g in range(SSM_GROUPS):
            xs, bg, cg = xs_scr[g], b_scr[g], c_scr[g]
            expand = exp_ref[g]
            y_diag = diag_group(g, xs, bg, cg)
            hg = h_scr[g]
            y_off = _dot(cg.astype(BF16), hg.astype(BF16))
            y_scr[g] = y_diag + y_off * _dot_sel_right(ea_scr[...], expand, 2)
            xw = (xs * _dot_sel_right(we_scr[...], expand, 2)).astype(BF16)
            st = _dot(bg.T.astype(BF16), xw)
            dec = jnp.exp(_dot_sel_right(aend_scr[:SUBLANES, :], expand, 3))[:1]
            h_scr[g] = dec * hg + st
        for g in range(SSM_GROUPS):
            finish_group(g, y_scr[g], xs_scr[g])

        @pl.when(step == nstep - 1)
        def _():
            for g in range(SSM_GROUPS):
                ht_ref[0, g * GROUP_WIDTH:(g + 1) * GROUP_WIDTH, :] = h_scr[g].T
    else:
        @pl.when(step == 0)
        def _():
            xtail_ref[:, :, :D_INNER] = _seq_tails(xs_ref[...].astype(F32), SSM_CONV)
            xtail_ref[:, :, D_INNER:] = _seq_tails(bc_ref[...].astype(F32), SSM_CONV)
            dt_prelude()
            for g in range(SSM_GROUPS):
                xs, bg, cg = conv_group(g)
                expand = exp_ref[g]
                xs_scr[g] = xs
                b_scr[g] = bg
                c_scr[g] = cg
                y_scr[g] = diag_group(g, xs, bg, cg)
                eae_scr[g] = _dot_sel_right(ea_scr[...], expand, 2)
                xwt_scr[g] = (xs * _dot_sel_right(we_scr[...], expand, 2)).T.astype(BF16)

        state_copy(t).wait()
        h_cur = h_ring.at[t % STATE_SLOTS]
        for u in range(seqs):
            sidx = step * seqs + u
            row = lax.broadcasted_iota(jnp.int32, (q, 128), 0)
            onehot = jnp.where(row == sidx * DEC_SEQ + DEC_SEQ - 1, 1.0, 0.0).astype(BF16)
            cdec = jnp.exp2(_dot_sel_right(acumt_scr[...], onehot, 3))
            in_seq = (row // DEC_SEQ) == sidx
            seq_rows = pl.ds(pl.multiple_of(sidx * DEC_SEQ, DEC_SEQ), DEC_SEQ)
            for g in range(SSM_GROUPS):
                rows = slice(g * GROUP_WIDTH, (g + 1) * GROUP_WIDTH)
                hg = h_cur[u, rows, :]
                cj = c_scr[g, seq_rows, :].astype(BF16)
                y_off = _dot_nt(cj, hg.astype(BF16))
                y_scr[g, seq_rows, :] = y_scr[g, seq_rows, :] + y_off * eae_scr[g, seq_rows, :]
                bj = jnp.where(in_seq, b_scr[g], 0.0).astype(BF16)
                st = _dot(xwt_scr[g], bj)
                dec = [jnp.broadcast_to(cdec[g * HEADS_PER_GROUP + r:g * HEADS_PER_GROUP + r + 1],
                                        (SSM_HEAD_DIM, 128)) for r in range(HEADS_PER_GROUP)]
                ht_ref[u, rows, :] = jnp.concatenate(dec, axis=0) * hg + st

        @pl.when(step == nstep - 1)
        def _():
            for g in range(SSM_GROUPS):
                finish_group(g, y_scr[g], xs_scr[g])


def _ssd(proj, dt_raw, consts, wts, nseq, short, past=None, h0=None):
    m = proj.shape[0]
    q = SSM_CHUNK
    nblk = m // q
    if short:
        seqs = 2
        steps = q // DEC_SEQ // seqs
        grid = (nblk, steps)
        blk = lambda a, b: a
        st_idx = lambda a, b: (a * steps + b, 0, 0)
    else:
        seqs = 1
        steps = m // nseq // q
        grid = (nseq, steps)
        blk = lambda a, b: a * steps + b
        st_idx = lambda a, b: (a, 0, 0)
    const2 = lambda shape: pl.BlockSpec(shape, lambda a, b: (0, 0))
    const3 = lambda shape: pl.BlockSpec(shape, lambda a, b: (0, 0, 0))
    in_specs = [
        pl.BlockSpec((q, D_INNER), lambda a, b: (blk(a, b), 0)),
        pl.BlockSpec((q, D_INNER), lambda a, b: (blk(a, b), 1)),
        pl.BlockSpec((q, BC_DIM), lambda a, b: (blk(a, b), 2 * D_INNER // BC_DIM)),
        pl.BlockSpec((q, DT_PAD), lambda a, b: (blk(a, b), 0)),
        const2((q, q)), const2((q, q)), const2((q, q)),
        const3((SSM_GROUPS, 128, GROUP_WIDTH)),
        const2((SSM_CONV, D_INNER)), const2((1, D_INNER)),
        const2((SSM_CONV, BC_DIM)), const2((1, BC_DIM)),
        const2((1, DT_PAD)), const2((1, DT_PAD)),
        const2((1, D_INNER)), const2((1, D_INNER)),
    ]
    args = [proj, proj, proj, dt_raw, *consts, *wts]
    state_spec = pl.BlockSpec((seqs, D_INNER, SSM_STATE), st_idx)
    scratch = [
        pltpu.VMEM((SSM_GROUPS, q, GROUP_WIDTH), F32),
        pltpu.VMEM((SSM_GROUPS, q, SSM_STATE), F32),
        pltpu.VMEM((SSM_GROUPS, q, SSM_STATE), F32),
        pltpu.VMEM((SSM_GROUPS, q, GROUP_WIDTH), F32),
        pltpu.VMEM((q, 128), F32), pltpu.VMEM((q, 128), F32), pltpu.VMEM((q, 128), F32),
        pltpu.VMEM((128, q), F32), pltpu.VMEM((128, q), F32),
    ]
    out_specs = [pl.BlockSpec((q, D_INNER), lambda a, b: (blk(a, b), 0)), state_spec]
    out_shape = [jax.ShapeDtypeStruct((m, D_INNER), BF16),
                 jax.ShapeDtypeStruct((nseq, D_INNER, SSM_STATE), F32)]
    if short:
        per_blk = q // DEC_SEQ
        in_specs += [
            pl.BlockSpec((per_blk, SSM_CONV - 1, D_INNER), lambda a, b: (a, 0, 0)),
            pl.BlockSpec((per_blk, SSM_CONV - 1, BC_DIM), lambda a, b: (a, 0, D_INNER // BC_DIM)),
            pl.BlockSpec(memory_space=pl.ANY),
        ]
        args += [past, past, h0]
        out_specs.append(pl.BlockSpec((per_blk, SSM_CONV - 1, XBC_DIM), lambda a, b: (a, 0, 0)))
        out_shape.append(jax.ShapeDtypeStruct((nseq, SSM_CONV - 1, XBC_DIM), F32))
        scratch += [pltpu.VMEM((SSM_GROUPS, q, GROUP_WIDTH), F32),
                    pltpu.VMEM((SSM_GROUPS, GROUP_WIDTH, q), BF16),
                    pltpu.VMEM((STATE_SLOTS, seqs, D_INNER, SSM_STATE), F32),
                    pltpu.SemaphoreType.DMA((STATE_SLOTS,))]
    else:
        scratch += [pltpu.VMEM((SSM_GROUPS, SSM_STATE, GROUP_WIDTH), F32),
                    pltpu.VMEM((SUBLANES, D_INNER), F32),
                    pltpu.VMEM((SUBLANES, BC_DIM), F32)]
    return pl.pallas_call(
        functools.partial(_ssd_kernel, short=short, q=q, seqs=seqs),
        grid=grid,
        in_specs=in_specs,
        out_specs=out_specs,
        out_shape=out_shape,
        scratch_shapes=scratch,
        compiler_params=_cparams(("arbitrary", "arbitrary")),
        name="ssd_short" if short else "ssd_long",
    )(*args)


def _mix_kernel(*refs, short, tiles_per_seq):
    yn_ref, scb_ref, scc_ref, sch_ref, ga_ref, gb_ref, cw_ref, wa_ref, wb_ref = refs[:9]
    refs = refs[9:]
    if short:
        past_ref = refs[0]
        refs = refs[1:]
    mixed_ref, utail_ref = refs[:2]
    tm = yn_ref.shape[0]

    u = scc_ref[...].astype(F32) * sch_ref[...].astype(F32)
    if short:
        past = _past_rows(past_ref[...])
        shift = lambda k: _shift_short(u, past, k, SC_CONV)
        utail_ref[...] = _seq_tails(u, SC_CONV)
    else:
        halo_scr = refs[2]
        first = pl.program_id(0) % tiles_per_seq == 0
        halo = jnp.where(first, 0.0, halo_scr[...])
        shift = lambda k: _shift_long(u, halo, k)
        halo_scr[...] = u[tm - SUBLANES:]
        utail_ref[...] = u[tm - SUBLANES:]
    s = (scb_ref[...].astype(F32) * _causal_conv(u, cw_ref[...], shift)).astype(BF16)

    a = _dot(yn_ref[...], wa_ref[...])
    b = _dot(s, wb_ref[...])
    mixed = _sigmoid(ga_ref[...].astype(F32)) * a + _sigmoid(gb_ref[...].astype(F32)) * b
    mixed_ref[...] = mixed.astype(BF16)


def _mix(yn, proj, cw, wa, wb, seq_len, tm, short, past=None):
    m = yn.shape[0]
    col = lambda c: (D_INNER + XBC_DIM) // D_MODEL + c
    rows = lambda c: pl.BlockSpec((tm, D_MODEL), lambda i: (i, col(c)))
    resident = lambda shape: pl.BlockSpec(shape, lambda i: (0, 0), pipeline_mode=pl.Buffered(1))
    in_specs = [
        pl.BlockSpec((tm, D_INNER), lambda i: (i, 0)),
        rows(0), rows(1), rows(2), rows(3), rows(4),
        resident((SC_CONV, SC_DIM)),
        resident((D_INNER, D_MODEL)),
        resident((SC_DIM, D_MODEL)),
    ]
    args = [yn, proj, proj, proj, proj, proj, cw, wa, wb]
    scratch = []
    if short:
        seq_blk = (tm // DEC_SEQ, SC_CONV - 1, SC_DIM)
        in_specs.append(pl.BlockSpec(seq_blk, lambda i: (i, 0, 0)))
        args.append(past)
        tail_spec = pl.BlockSpec(seq_blk, lambda i: (i, 0, 0))
        tail_shape = (m // DEC_SEQ, SC_CONV - 1, SC_DIM)
    else:
        scratch.append(pltpu.VMEM((SUBLANES, SC_DIM), F32))
        tail_spec = pl.BlockSpec((SUBLANES, SC_DIM), lambda i: (i, 0))
        tail_shape = (m // tm * SUBLANES, SC_DIM)
    return pl.pallas_call(
        functools.partial(_mix_kernel, short=short, tiles_per_seq=max(seq_len // tm, 1)),
        grid=(m // tm,),
        in_specs=in_specs,
        out_specs=[pl.BlockSpec((tm, D_MODEL), lambda i: (i, 0)), tail_spec],
        out_shape=[jax.ShapeDtypeStruct((m, D_MODEL), BF16),
                   jax.ShapeDtypeStruct(tail_shape, F32)],
        scratch_shapes=scratch,
        compiler_params=_cparams(("arbitrary",)),
        name="mix",
    )(*args)


def _row_halves(ref):
    half = ref.shape[0] // 2
    return [slice(0, half), slice(half, 2 * half)]


def _out_proj_kernel(mixed_ref, x_ref, wo_ref, g_ref, x1_ref, n2_ref):
    for rs in _row_halves(x_ref):
        x1 = x_ref[rs, :] + _dot(mixed_ref[rs, :], wo_ref[...])
        x1_ref[rs, :] = x1
        n2_ref[rs, :] = _rms(x1, g_ref[...]).astype(BF16)


def _out_proj(mixed, x, wo, g, tm):
    m = x.shape[0]
    row = lambda i: (i, 0)
    const = lambda i: (0, 0)
    return pl.pallas_call(
        _out_proj_kernel,
        grid=(m // tm,),
        in_specs=[pl.BlockSpec((tm, D_MODEL), row), pl.BlockSpec((tm, D_MODEL), row),
                  pl.BlockSpec((D_MODEL, D_MODEL), const, pipeline_mode=pl.Buffered(1)),
                  pl.BlockSpec((1, D_MODEL), const)],
        out_specs=[pl.BlockSpec((tm, D_MODEL), row), pl.BlockSpec((tm, D_MODEL), row)],
        out_shape=[jax.ShapeDtypeStruct((m, D_MODEL), F32),
                   jax.ShapeDtypeStruct((m, D_MODEL), BF16)],
        compiler_params=_cparams(("parallel",)),
        name="out_proj",
    )(mixed, x, wo, g)


def _ffn_up_kernel(*refs, short, tiles_per_seq, splits):
    n2_ref, wg_ref, wu_ref, cw_ref, cb_ref = refs[:5]
    refs = refs[5:]
    if short:
        past_ref = refs[0]
        refs = refs[1:]
    act_ref, gtail_ref = refs[:2]
    tm = n2_ref.shape[0]
    n2 = n2_ref[...]
    j = pl.program_id(1)
    first = pl.program_id(0) % tiles_per_seq == 0
    lo = 0
    for width in splits:
        cs = slice(lo, lo + width)
        lo += width
        gate = _dot(n2, wg_ref[:, cs].astype(BF16))
        up = _dot(n2, wu_ref[:, cs].astype(BF16))
        if short:
            past = _past_rows(past_ref[:, :, cs])
            shift = lambda k, gate=gate, past=past: _shift_short(gate, past, k, FFN_CONV)
            gtail_ref[:, :, cs] = _seq_tails(gate, FFN_CONV)
        else:
            halo_scr = refs[2]
            halo = jnp.where(first, 0.0, halo_scr[j, :, cs])
            shift = lambda k, gate=gate, halo=halo: _shift_long(gate, halo, k)
            halo_scr[j, :, cs] = gate[tm - SUBLANES:]
            gtail_ref[:, cs] = gate[tm - SUBLANES:]
        conv = cb_ref[:, cs] + _causal_conv(gate, cw_ref[:, cs], shift)
        act_ref[:, cs] = (_silu(conv) * up).astype(BF16)


def _ffn_up(n2, wg, wu, cw, cb, seq_len, tm, tn, splits, short, past=None):
    m = n2.shape[0]
    nj = D_FF // tn
    assert sum(splits) == tn
    in_specs = [
        pl.BlockSpec((tm, D_MODEL), lambda i, j: (i, 0)),
        pl.BlockSpec((D_MODEL, tn), lambda i, j: (0, j)),
        pl.BlockSpec((D_MODEL, tn), lambda i, j: (0, j)),
        pl.BlockSpec((FFN_CONV, tn), lambda i, j: (0, j)),
        pl.BlockSpec((1, tn), lambda i, j: (0, j)),
    ]
    args = [n2, wg, wu, cw, cb]
    scratch = []
    if short:
        tail_spec = pl.BlockSpec((tm // DEC_SEQ, FFN_CONV - 1, tn), lambda i, j: (i, 0, j))
        in_specs.append(tail_spec)
        args.append(past)
        tail_shape = (m // DEC_SEQ, FFN_CONV - 1, D_FF)
    else:
        scratch.append(pltpu.VMEM((nj, SUBLANES, tn), F32))
        tail_spec = pl.BlockSpec((SUBLANES, tn), lambda i, j: (i, j))
        tail_shape = (m // tm * SUBLANES, D_FF)
    return pl.pallas_call(
        functools.partial(_ffn_up_kernel, short=short, tiles_per_seq=max(seq_len // tm, 1),
                          splits=splits),
        grid=(m // tm, nj),
        in_specs=in_specs,
        out_specs=[pl.BlockSpec((tm, tn), lambda i, j: (i, j)), tail_spec],
        out_shape=[jax.ShapeDtypeStruct((m, D_FF), BF16),
                   jax.ShapeDtypeStruct(tail_shape, F32)],
        scratch_shapes=scratch,
        compiler_params=_cparams(("arbitrary", "arbitrary")),
        name="ffn_up",
    )(*args)


def _ffn_down_kernel(act_ref, wd_ref, x1_ref, x2_ref):
    x2_ref[...] = x1_ref[...] + _dot(act_ref[...], wd_ref[...])


def _ffn_down(act, wd, x1, tm):
    m = x1.shape[0]
    return pl.pallas_call(
        _ffn_down_kernel,
        grid=(m // tm,),
        in_specs=[pl.BlockSpec((tm, D_FF), lambda i: (i, 0)),
                  pl.BlockSpec((D_FF, D_MODEL), lambda i: (0, 0), pipeline_mode=pl.Buffered(1)),
                  pl.BlockSpec((tm, D_MODEL), lambda i: (i, 0))],
        out_specs=pl.BlockSpec((tm, D_MODEL), lambda i: (i, 0)),
        out_shape=jax.ShapeDtypeStruct((m, D_MODEL), F32),
        compiler_params=_cparams(("parallel",)),
        name="ffn_down",
    )(act, wd, x1)


def _ple_kernel(x2_ref, pe_ref, wple_ref, wgate_ref, gp_ref, gf_ref, y_ref):
    for rs in _row_halves(x2_ref):
        x2 = x2_ref[rs, :]
        n3 = _rms(x2, gp_ref[...]).astype(BF16)
        gate = _sigmoid(_dot(n3, wgate_ref[...]))
        emb = _dot(pe_ref[rs, :].astype(BF16), wple_ref[...])
        y_ref[rs, :] = _rms(x2 + emb * gate, gf_ref[...])


def _ple(x2, pe, wple, wgate, gp, gf, tm):
    m = x2.shape[0]
    row = lambda i: (i, 0)
    const = lambda i: (0, 0)
    return pl.pallas_call(
        _ple_kernel,
        grid=(m // tm,),
        in_specs=[pl.BlockSpec((tm, D_MODEL), row), pl.BlockSpec((tm, PLE_DIM), row),
                  pl.BlockSpec((PLE_DIM, D_MODEL), const),
                  pl.BlockSpec((D_MODEL, D_MODEL), const, pipeline_mode=pl.Buffered(1)),
                  pl.BlockSpec((1, D_MODEL), const), pl.BlockSpec((1, D_MODEL), const)],
        out_specs=pl.BlockSpec((tm, D_MODEL), row),
        out_shape=jax.ShapeDtypeStruct((m, D_MODEL), F32),
        compiler_params=_cparams(("parallel",)),
        name="ple",
    )(x2, pe, wple, wgate, gp, gf)


def _tile(m, want):
    return min(m, want)


def _layer(x, pe, w, short, states=None):
    nseq, seq_len, _ = x.shape
    m = nseq * seq_len
    x2d = x.reshape(m, D_MODEL)
    pe2d = pe.reshape(m, PLE_DIM)
    if short:
        ssm0, xbc0, sc0, ffn0 = states
        h0 = ssm0.reshape(nseq, D_INNER, SSM_STATE)

    proj, dt_raw = _in_proj(x2d, w["norm_mix"], w["w_in_t"], _tile(m, 1024), 1024)

    ssd_w = (w["cw_xs"], w["cb_xs"], w["cw_bc"], w["cb_bc"], w["dt_bias"], w["a_neg"],
             w["d_exp"], w["ssm_norm"])
    consts = _ssd_consts(SSM_CHUNK, short)
    if short:
        yn, ssm_new, xbc_new = _ssd(proj, dt_raw, consts, ssd_w, nseq, True, xbc0, h0)
    else:
        yn, ssm_new = _ssd(proj, dt_raw, consts, ssd_w, nseq, False)
        proj3 = proj.reshape(nseq, seq_len, PROJ_DIM)
        xbc_new = proj3[:, seq_len - (SSM_CONV - 1):, D_INNER:D_INNER + XBC_DIM].astype(F32)

    tm = _tile(m, 512)
    tm_mix = _tile(m, 256)
    mixed, u_tail = _mix(yn, proj, w["sc_conv_w"], w["w_ssm_out"], w["w_sc_out"], seq_len,
                         tm_mix, short, sc0 if short else None)
    x1, n2 = _out_proj(mixed, x2d, w["w_o"], w["norm_ffn"], tm)

    tm_ffn = _tile(m, 1024)
    act, g_tail = _ffn_up(n2, w["w_ffn_gate"], w["w_ffn_up"], w["ffn_conv_w"], w["ffn_conv_b"],
                          seq_len, tm_ffn, 512, (256, 256), short, ffn0 if short else None)
    x2 = _ffn_down(act, w["w_ffn_down"], x1, _tile(m, 256))
    y = _ple(x2, pe2d, w["w_ple"], w["w_ple_gate"], w["norm_ple"], w["norm_final"], tm)

    def last_rows(tail, tile_rows, taps, width):
        if short:
            return tail
        per_seq = seq_len // tile_rows
        t = tail.reshape(nseq, per_seq, SUBLANES, width)[:, per_seq - 1]
        return t[:, SUBLANES - (taps - 1):]

    sc_new = last_rows(u_tail, tm_mix, SC_CONV, SC_DIM)
    ffn_new = last_rows(g_tail, tm_ffn, FFN_CONV, D_FF)
    ssm_new = ssm_new.reshape(nseq, SSM_HEADS, SSM_HEAD_DIM, SSM_STATE)
    return y.reshape(nseq, seq_len, D_MODEL), ssm_new, xbc_new, sc_new, ffn_new


def _prep_weights(norm_mix, w_in, ssm_conv_w, ssm_conv_b, ssm_dt_bias, ssm_a_log, ssm_d, ssm_norm,
                  w_ssm_out, sc_conv_w, w_sc_out, w_o, norm_ffn, w_ffn_gate, w_ffn_up,
                  ffn_conv_w, ffn_conv_b, w_ffn_down, norm_ple, w_ple, w_ple_gate, norm_final):
    row = lambda v: v.reshape(1, -1).astype(F32)
    pad_heads = lambda v: jnp.pad(v.astype(F32), (0, DT_PAD - SSM_HEADS)).reshape(1, DT_PAD)
    return {
        "norm_mix": row(norm_mix),
        "w_in_t": w_in.T,
        "cw_xs": ssm_conv_w[:, :D_INNER].astype(F32),
        "cb_xs": row(ssm_conv_b[:D_INNER]),
        "cw_bc": ssm_conv_w[:, D_INNER:].astype(F32),
        "cb_bc": row(ssm_conv_b[D_INNER:]),
        "dt_bias": pad_heads(ssm_dt_bias),
        "a_neg": pad_heads(-jnp.exp(ssm_a_log.astype(F32))),
        "d_exp": row(jnp.repeat(ssm_d, SSM_HEAD_DIM)),
        "ssm_norm": row(ssm_norm),
        "w_ssm_out": w_ssm_out.astype(BF16),
        "sc_conv_w": sc_conv_w.astype(F32),
        "w_sc_out": w_sc_out.astype(BF16),
        "w_o": w_o.astype(BF16),
        "norm_ffn": row(norm_ffn),
        "w_ffn_gate": w_ffn_gate,
        "w_ffn_up": w_ffn_up,
        "ffn_conv_w": ffn_conv_w.astype(F32),
        "ffn_conv_b": row(ffn_conv_b),
        "w_ffn_down": w_ffn_down.astype(BF16),
        "norm_ple": row(norm_ple),
        "w_ple": w_ple.astype(BF16),
        "w_ple_gate": w_ple_gate.astype(BF16),
        "norm_final": row(norm_final),
    }


def kernel(x_prompt, x_sample, state_ssm, state_ssm_conv, state_short_conv, state_ffn_conv, p_prompt, p_sample, norm_mix, w_in, ssm_conv_w, ssm_conv_b, ssm_dt_bias, ssm_a_log, ssm_d, ssm_norm, w_ssm_out, sc_conv_w, w_sc_out, w_o, norm_ffn, w_ffn_gate, w_ffn_up, ffn_conv_w, ffn_conv_b, w_ffn_down, norm_ple, w_ple, w_ple_gate, norm_final):
    depth = w_in.shape[0]
    assert depth == 1, "the final norm is fused into the layer's last kernel"
    assert x_sample.shape[1] == DEC_SEQ
    layer_w = (norm_mix, w_in, ssm_conv_w, ssm_conv_b, ssm_dt_bias, ssm_a_log, ssm_d, ssm_norm,
               w_ssm_out, sc_conv_w, w_sc_out, w_o, norm_ffn, w_ffn_gate, w_ffn_up,
               ffn_conv_w, ffn_conv_b, w_ffn_down, norm_ple, w_ple, w_ple_gate)
    w = _prep_weights(*[t[0] for t in layer_w], norm_final)
    yp, *prompt_states = _layer(x_prompt, p_prompt[0], w, short=False)
    ys, *sample_states = _layer(
        x_sample, p_sample[0], w, short=True,
        states=(state_ssm[0], state_ssm_conv[0], state_short_conv[0], state_ffn_conv[0]))
    outs = [yp, ys]
    for sp, ss in zip(prompt_states, sample_states):
        outs += [sp[None], ss[None]]
    return tuple(outs)
```
